```python
import math
import jax
import jax.numpy as jnp
from jax import lax
import numpy as np

D_MODEL = 1024
BATCH = 2
SEQ = 8192
DEPTH = 4
DEC_BATCH = 128
DEC_SEQ = 4
PAST_LEN = 8192
PAGE_SIZE = 128

HEAD_DIM = 64
A_HEADS = 8
A_IDX_HEADS = 4
A_IDX_DIM = 64
A_TOPK = 256
B_HEADS = 8
B_KV_HEADS = 2
B_BLOCK = 64
B_TOPN = 16
B_WINDOW = 512
C_HEADS = 16
C_KV_HEADS = 2
C_WINDOW = 128
X_HEADS = 4
X_HEAD_DIM = 128
N_MEM = 256
D_FF = 2816
N_BUCKETS = 32
BUCKET_MAX_DIST = 128
N_BIAS_HEADS = A_HEADS + B_HEADS
QBLOCK = 128
LN_EPS = 1e-5
ALPHA = (2 * DEPTH) ** 0.25
BETA = (8 * DEPTH) ** -0.25
N_EVEN = (DEPTH + 1) // 2
N_ODD = DEPTH // 2
NEG = -1e30
FORCE_SCORE = 1e4
EVEN_SPLITS = [A_HEADS * HEAD_DIM, HEAD_DIM, HEAD_DIM, A_IDX_HEADS * A_IDX_DIM, A_IDX_DIM, A_IDX_HEADS,
               B_HEADS * HEAD_DIM] + [B_KV_HEADS * HEAD_DIM] * 6 + [B_HEADS * 3]
EVEN_IN = sum(EVEN_SPLITS)
ODD_SPLITS = [C_HEADS * HEAD_DIM, C_KV_HEADS * HEAD_DIM, C_KV_HEADS * HEAD_DIM]
ODD_IN = sum(ODD_SPLITS)
A_MIX = A_HEADS * HEAD_DIM
B_MIX = B_HEADS * HEAD_DIM
C_MIX = C_HEADS * HEAD_DIM
X_WIDTH = X_HEADS * X_HEAD_DIM

kernel_name = 'hybrid_dsa_nsa_swa_macaron_step'


def layer_norm(x, g, b):
    xf = x.astype(jnp.float32)
    mu = jnp.mean(xf, axis=-1, keepdims=True)
    var = jnp.mean(jnp.square(xf - mu), axis=-1, keepdims=True)
    return ((xf - mu) * lax.rsqrt(var + LN_EPS) * g.astype(jnp.float32) + b.astype(jnp.float32)).astype(x.dtype)


def swiglu(x, wg, wu, wd):
    return (jax.nn.silu(x @ wg) * (x @ wu)) @ wd


def masked_softmax(logits, mask):
    return jax.nn.softmax(jnp.where(mask, logits.astype(jnp.float32), NEG), axis=-1)


def split_cols(a, sizes):
    offs = np.cumsum(sizes)[:-1].tolist()
    return jnp.split(a, offs, axis=-1)


def t5_bucket(dist):
    exact = N_BUCKETS // 2
    d = jnp.maximum(dist, 0)
    rel = jnp.log(jnp.maximum(d, 1).astype(jnp.float32) / exact) / math.log(BUCKET_MAX_DIST / exact)
    large = jnp.minimum(exact + (rel * (N_BUCKETS - exact)).astype(jnp.int32), N_BUCKETS - 1)
    return jnp.where(d < exact, d, large)


def map_query_blocks(fn, tensors, T):
    qb = min(QBLOCK, T)
    nb = T // qb
    def split(a):
        return jnp.moveaxis(a.reshape((a.shape[0], nb, qb) + a.shape[2:]), 1, 0)
    out = lax.map(lambda args: fn(*args), (jnp.arange(nb),) + tuple(split(a) for a in tensors))
    out = jnp.moveaxis(out, 0, 1)
    return out.reshape((out.shape[0], T) + out.shape[3:])


def fetch(src, b, idx, extra):
    pool, table, new, past, layer = src
    if pool is None:
        return new[(b, jnp.minimum(idx, new.shape[1] - 1)) + extra]
    pi = jnp.minimum(idx, past - 1)
    phys = table[b, pi // PAGE_SIZE]
    old = pool[(layer, phys, pi % PAGE_SIZE) + extra]
    cur = new[(b, jnp.clip(idx - past, 0, new.shape[1] - 1)) + extra]
    return jnp.where((idx < past)[..., None], old, cur)


def window_context(kv_new, buf, window):
    T = kv_new.shape[1]
    if buf is None:
        full = kv_new
        keep = min(window, T)
    else:
        full = jnp.concatenate([buf, kv_new], axis=1)
        keep = buf.shape[1]
    new_state = full[:, full.shape[1] - keep:]
    ctx = jnp.pad(full, ((0, 0), (window + T - full.shape[1], 0), (0, 0), (0, 0), (0, 0)))
    return ctx, new_state


def window_attend(qb, win_k, win_v, i, q0, window, bias_gr, sinks):
    nq = qb.shape[1]
    span = window + nq
    k = lax.dynamic_slice_in_dim(win_k, i * nq, span, axis=1)
    v = lax.dynamic_slice_in_dim(win_v, i * nq, span, axis=1)
    tpos = q0 + i * nq + jnp.arange(nq)
    kpos = q0 - window + i * nq + jnp.arange(span)
    dist = tpos[:, None] - kpos[None, :]
    valid = (kpos[None, :] >= 0) & (dist >= 0) & (dist < window)
    logits = jnp.einsum('bqgrd,bkgd->bqgrk', qb, k) * HEAD_DIM ** -0.5
    logits = logits + jnp.transpose(bias_gr[t5_bucket(dist)], (0, 2, 3, 1))[None]
    logits = jnp.where(valid[None, :, None, None, :], logits.astype(jnp.float32), NEG)
    if sinks is None:
        p = jax.nn.softmax(logits, axis=-1)
    else:
        sink = jnp.broadcast_to(sinks.astype(jnp.float32)[None, None, :, :, None], logits.shape[:-1] + (1,))
        p = jax.nn.softmax(jnp.concatenate([logits, sink], axis=-1), axis=-1)[..., :-1]
    return jnp.einsum('bqgrk,bkgd->bqgrd', p.astype(v.dtype), v)


def dsa_mixer(q, q_idx, w_idx, src, L, q0, bias_a):
    Bn, T = q.shape[:2]
    n_keep = min(A_TOPK, L // 4)
    bidx = jnp.arange(Bn)
    kpos = jnp.arange(L)
    k_idx_all = fetch(src, bidx[:, None], kpos[None, :], (2,))
    def block(i, qb, qib, wb):
        nq = qb.shape[1]
        tpos = q0 + i * nq + jnp.arange(nq)
        s = jax.nn.relu(jnp.einsum('bqhd,bsd->bqhs', qib, k_idx_all) * A_IDX_DIM ** -0.5)
        score = jnp.einsum('bqh,bqhs->bqs', wb, s).astype(jnp.float32) * A_IDX_HEADS ** -0.5
        score = jnp.where(kpos[None, None, :] <= tpos[None, :, None], score, NEG)
        _, sel = lax.top_k(score, n_keep)
        valid = sel <= tpos[None, :, None]
        k = fetch(src, bidx[:, None, None], sel, (0,))
        v = fetch(src, bidx[:, None, None], sel, (1,))
        logits = jnp.einsum('bqhd,bqkd->bqhk', qb, k) * HEAD_DIM ** -0.5
        logits = logits + jnp.moveaxis(bias_a[t5_bucket(tpos[None, :, None] - sel)], -1, 2)
        p = masked_softmax(logits, valid[:, :, None, :]).astype(v.dtype)
        return jnp.einsum('bqhk,bqkd->bqhd', p, v)
    return map_query_blocks(block, (q, q_idx, w_idx), T)


def nsa_compress(raw, pos_w, w1, w2):
    Bn, n, G, d = raw.shape
    blocks = raw.reshape(Bn, n // B_BLOCK, B_BLOCK, G, d)
    pooled = jnp.einsum('bnlgd,ld->bngd', blocks, pos_w)
    return jax.nn.gelu(pooled @ w1) @ w2


def nsa_mixer(q, gate_logits, src, win_ctx, L, q0, cmp_pos, cmp_w1, cmp_w2, bias_b):
    Bn, T = q.shape[:2]
    R = B_HEADS // B_KV_HEADS
    qg = q.reshape(Bn, T, B_KV_HEADS, R, HEAD_DIM)
    gates = jax.nn.sigmoid(gate_logits.astype(jnp.float32)).astype(q.dtype).reshape(Bn, T, B_KV_HEADS, R, 3)
    bidx = jnp.arange(Bn)
    gidx = jnp.arange(B_KV_HEADS)
    nblk = L // B_BLOCK
    nsb = -(-L // B_BLOCK)
    n_sel = min(B_TOPN, nsb)
    kpos = jnp.arange(nblk * B_BLOCK)
    raw_k = fetch(src, bidx[:, None, None], kpos[None, :, None], (0, gidx[None, None, :]))
    raw_v = fetch(src, bidx[:, None, None], kpos[None, :, None], (1, gidx[None, None, :]))
    k_cmp = nsa_compress(raw_k, cmp_pos[0], cmp_w1[0], cmp_w2[0])
    v_cmp = nsa_compress(raw_v, cmp_pos[1], cmp_w1[1], cmp_w2[1])
    c_end = (jnp.arange(nblk) + 1) * B_BLOCK - 1
    win_k, win_v = win_ctx[:, :, 0], win_ctx[:, :, 1]
    scale = HEAD_DIM ** -0.5
    jb = jnp.arange(nsb)
    def block(i, qb, gb):
        nq = qb.shape[1]
        tpos = q0 + i * nq + jnp.arange(nq)
        dist_c = tpos[:, None] - c_end[None, :]
        valid_c = (dist_c >= 0)[None, :, None, None, :]
        lc = jnp.einsum('bqgrd,bngd->bqgrn', qb, k_cmp) * scale
        lc = lc + jnp.moveaxis(bias_b[t5_bucket(dist_c)], 1, -1)[None]
        pc = masked_softmax(lc, valid_c) * valid_c
        o_cmp = jnp.einsum('bqgrn,bngd->bqgrd', pc.astype(v_cmp.dtype), v_cmp)
        cur = tpos // B_BLOCK
        imp = jnp.pad(pc.sum(axis=3), ((0, 0), (0, 0), (0, 0), (0, nsb - nblk)))
        forced = (jb[None, :] == 0) | (jb[None, :] == cur[:, None]) | (jb[None, :] == cur[:, None] - 1)
        adm = jb[None, :] <= cur[:, None]
        imp = jnp.where(adm[None, :, None, :], imp + FORCE_SCORE * forced[None, :, None, :], NEG)
        _, selb = lax.top_k(imp, n_sel)
        tok = (selb[..., None] * B_BLOCK + jnp.arange(B_BLOCK)).reshape(Bn, nq, B_KV_HEADS, n_sel * B_BLOCK)
        valid_s = tok <= tpos[None, :, None, None]
        ks = fetch(src, bidx[:, None, None, None], tok, (2, gidx[None, None, :, None]))
        vs = fetch(src, bidx[:, None, None, None], tok, (3, gidx[None, None, :, None]))
        ls = jnp.einsum('bqgrd,bqgkd->bqgrk', qb, ks) * scale
        bs = bias_b[t5_bucket(tpos[None, :, None, None] - tok), gidx[None, None, :, None]]
        ls = ls + jnp.moveaxis(bs, -1, 3)
        ps = masked_softmax(ls, valid_s[:, :, :, None, :])
        o_sel = jnp.einsum('bqgrk,bqgkd->bqgrd', ps.astype(vs.dtype), vs)
        o_win = window_attend(qb, win_k, win_v, i, q0, B_WINDOW, bias_b, None)
        return gb[..., 0:1] * o_cmp + gb[..., 1:2] * o_sel + gb[..., 2:3] * o_win
    return map_query_blocks(block, (qg, gates), T)


def swa_mixer(q, win_ctx, q0, sinks, bias_c):
    Bn, T = q.shape[:2]
    R = C_HEADS // C_KV_HEADS
    qg = q.reshape(Bn, T, C_KV_HEADS, R, HEAD_DIM)
    sink_gr = sinks.reshape(C_KV_HEADS, R)
    win_k, win_v = win_ctx[:, :, 0], win_ctx[:, :, 1]
    def block(i, qb):
        return window_attend(qb, win_k, win_v, i, q0, C_WINDOW, bias_c, sink_gr)
    return map_query_blocks(block, (qg,), T)


def cross_attn(x, mem_k, mem_v, wq, wo):
    Bn, T, _ = x.shape
    q = (x @ wq).reshape(Bn, T, X_HEADS, X_HEAD_DIM)
    s = jnp.einsum('bthd,bmhd->bthm', q, mem_k) * X_HEAD_DIM ** -0.5
    p = jax.nn.softmax(s.astype(jnp.float32), axis=-1).astype(mem_v.dtype)
    return jnp.einsum('bthm,bmhd->bthd', p, mem_v).reshape(Bn, T, X_WIDTH) @ wo


def trunk(x, mem_kv, a_pool, b_pool, b_win, c_win, page_table, ln_g, ln_b, ffn_wg, ffn_wu, ffn_wd,
          even_w_in, even_w_out, nsa_cmp_pos, nsa_cmp_w1, nsa_cmp_w2, odd_w_in, odd_w_out, c_sinks,
          x_wq, x_wo, rel_table):
    prompt = a_pool is None
    Bn, T, _ = x.shape
    q0 = 0 if prompt else PAST_LEN
    L = q0 + T
    bias_a = rel_table[:, :A_HEADS]
    bias_b = rel_table[:, A_HEADS:].reshape(N_BUCKETS, B_KV_HEADS, B_HEADS // B_KV_HEADS)
    bias_c = rel_table.reshape(N_BUCKETS, C_KV_HEADS, C_HEADS // C_KV_HEADS)
    a_rows, b_rows, b_states, c_states = [], [], [], []
    h = x
    for l in range(DEPTH):
        g, bb = ln_g[l], ln_b[l]
        h = layer_norm(ALPHA * h + 0.5 * swiglu(h, ffn_wg[l, 0], ffn_wu[l, 0], ffn_wd[l, 0]), g[0], bb[0])
        if l % 2 == 0:
            e = l // 2
            (q_a, k_a, v_a, qi, ki, wi, q_b, kc, vc, ksl, vsl, kw, vw, gl) = split_cols(h @ even_w_in[e], EVEN_SPLITS)
            kvs = (Bn, T, B_KV_HEADS, HEAD_DIM)
            new_a = jnp.stack([k_a, v_a, ki], axis=2)
            new_b = jnp.stack([t.reshape(kvs) for t in (kc, vc, ksl, vsl)], axis=2)
            win_ctx, b_state = window_context(jnp.stack([kw.reshape(kvs), vw.reshape(kvs)], axis=2),
                                              None if prompt else b_win[e], B_WINDOW)
            src_a = (a_pool, page_table, new_a, q0, e)
            src_b = (b_pool, page_table, new_b, q0, e)
            o_a = dsa_mixer(q_a.reshape(Bn, T, A_HEADS, HEAD_DIM), qi.reshape(Bn, T, A_IDX_HEADS, A_IDX_DIM),
                            wi, src_a, L, q0, bias_a)
            o_b = nsa_mixer(q_b.reshape(Bn, T, B_HEADS, HEAD_DIM), gl, src_b, win_ctx, L, q0,
                            nsa_cmp_pos[e], nsa_cmp_w1[e], nsa_cmp_w2[e], bias_b)
            mix = jnp.concatenate([o_a.reshape(Bn, T, A_MIX), o_b.reshape(Bn, T, B_MIX)], axis=-1) @ even_w_out[e]
            a_rows.append(new_a)
            b_rows.append(new_b)
            b_states.append(b_state)
        else:
            oi = l // 2
            q_c, kcw, vcw = split_cols(h @ odd_w_in[oi], ODD_SPLITS)
            kvs = (Bn, T, C_KV_HEADS, HEAD_DIM)
            win_ctx, c_state = window_context(jnp.stack([kcw.reshape(kvs), vcw.reshape(kvs)], axis=2),
                                              None if prompt else c_win[oi], C_WINDOW)
            o_c = swa_mixer(q_c.reshape(Bn, T, C_HEADS, HEAD_DIM), win_ctx, q0, c_sinks[oi], bias_c)
            mix = o_c.reshape(Bn, T, C_MIX) @ odd_w_out[oi]
            c_states.append(c_state)
        h = layer_norm(ALPHA * h + mix, g[1], bb[1])
        h = layer_norm(ALPHA * h + cross_attn(h, mem_kv[l, :, :, 0], mem_kv[l, :, :, 1], x_wq[l], x_wo[l]), g[2], bb[2])
        h = layer_norm(ALPHA * h + 0.5 * swiglu(h, ffn_wg[l, 1], ffn_wu[l, 1], ffn_wd[l, 1]), g[3], bb[3])
    return (h, jnp.stack(a_rows), jnp.stack(b_rows), jnp.stack(b_states), jnp.stack(c_states))


def setup_inputs(seed: int = 0) -> dict:
    key = jax.random.key(seed)
    ks = jax.random.split(key, 32)
    def nrm(i, shape, scale):
        return jax.random.normal(ks[i], shape, jnp.float32) * scale
    n_pages = PAST_LEN // PAGE_SIZE
    n_used = DEC_BATCH * n_pages
    n_phys = n_used + max(1, n_used // 4)
    wb = min(B_WINDOW, PAST_LEN)
    wc = min(C_WINDOW, PAST_LEN)
    page_table = jax.random.permutation(ks[0], n_phys)[:n_used].reshape(DEC_BATCH, n_pages).astype(jnp.int32)
    return {
        'x_prompt': nrm(1, (BATCH, SEQ, D_MODEL), 1.0),
        'x_sample': nrm(2, (DEC_BATCH, DEC_SEQ, D_MODEL), 1.0),
        'cache_a_kv': nrm(3, (N_EVEN, n_phys, PAGE_SIZE, 3, HEAD_DIM), 1.0),
        'cache_b_kv': nrm(4, (N_EVEN, n_phys, PAGE_SIZE, 4, B_KV_HEADS, HEAD_DIM), 1.0),
        'cache_b_win': nrm(5, (N_EVEN, DEC_BATCH, wb, 2, B_KV_HEADS, HEAD_DIM), 1.0),
        'cache_c_win': nrm(6, (N_ODD, DEC_BATCH, wc, 2, C_KV_HEADS, HEAD_DIM), 1.0),
        'cache_mem_kv': nrm(7, (DEPTH, DEC_BATCH, N_MEM, 2, X_HEADS, X_HEAD_DIM), 1.0),
        'page_table': page_table,
        'mem_prompt': nrm(8, (BATCH, N_MEM, D_MODEL), 1.0),
        'ln_g': 1.0 + nrm(9, (DEPTH, 4, D_MODEL), 0.05),
        'ln_b': nrm(10, (DEPTH, 4, D_MODEL), 0.02),
        'ffn_wg': nrm(11, (DEPTH, 2, D_MODEL, D_FF), D_MODEL ** -0.5),
        'ffn_wu': nrm(12, (DEPTH, 2, D_MODEL, D_FF), D_MODEL ** -0.5),
        'ffn_wd': nrm(13, (DEPTH, 2, D_FF, D_MODEL), BETA * D_FF ** -0.5),
        'even_w_in': nrm(14, (N_EVEN, D_MODEL, EVEN_IN), D_MODEL ** -0.5),
        'even_w_out': nrm(15, (N_EVEN, A_MIX + B_MIX, D_MODEL), BETA * (A_MIX + B_MIX) ** -0.5),
        'nsa_cmp_pos': (1.0 + nrm(16, (N_EVEN, 2, B_BLOCK, HEAD_DIM), 0.1)) * B_BLOCK ** -0.5,
        'nsa_cmp_w1': nrm(17, (N_EVEN, 2, HEAD_DIM, HEAD_DIM), HEAD_DIM ** -0.5),
        'nsa_cmp_w2': nrm(18, (N_EVEN, 2, HEAD_DIM, HEAD_DIM), HEAD_DIM ** -0.5),
        'odd_w_in': nrm(19, (N_ODD, D_MODEL, ODD_IN), D_MODEL ** -0.5),
        'odd_w_out': nrm(20, (N_ODD, C_MIX, D_MODEL), BETA * C_MIX ** -0.5),
        'c_sinks': nrm(21, (N_ODD, C_HEADS), 0.5),
        'x_wq': nrm(22, (DEPTH, D_MODEL, X_WIDTH), D_MODEL ** -0.5),
        'x_wk': nrm(23, (DEPTH, D_MODEL, X_WIDTH), D_MODEL ** -0.5),
        'x_wv': nrm(24, (DEPTH, D_MODEL, X_WIDTH), D_MODEL ** -0.5),
        'x_wo': nrm(25, (DEPTH, X_WIDTH, D_MODEL), BETA * X_WIDTH ** -0.5),
        'rel_table': nrm(26, (N_BUCKETS, N_BIAS_HEADS), 0.5),
    }


def reference(x_prompt, x_sample, cache_a_kv, cache_b_kv, cache_b_win, cache_c_win, cache_mem_kv, page_table,
              mem_prompt, ln_g, ln_b, ffn_wg, ffn_wu, ffn_wd, even_w_in, even_w_out, nsa_cmp_pos, nsa_cmp_w1,
              nsa_cmp_w2, odd_w_in, odd_w_out, c_sinks, x_wq, x_wk, x_wv, x_wo, rel_table):
    nb = mem_prompt.shape[0]
    mk = jnp.einsum('bmd,ldf->lbmf', mem_prompt, x_wk).reshape(DEPTH, nb, N_MEM, X_HEADS, X_HEAD_DIM)
    mv = jnp.einsum('bmd,ldf->lbmf', mem_prompt, x_wv).reshape(DEPTH, nb, N_MEM, X_HEADS, X_HEAD_DIM)
    mem_kv_prompt = jnp.stack([mk, mv], axis=3)
    weights = (ln_g, ln_b, ffn_wg, ffn_wu, ffn_wd, even_w_in, even_w_out, nsa_cmp_pos, nsa_cmp_w1, nsa_cmp_w2,
               odd_w_in, odd_w_out, c_sinks, x_wq, x_wo, rel_table)
    y_prompt, a_rows_p, b_rows_p, b_win_p, c_win_p = trunk(
        x_prompt, mem_kv_prompt, None, None, None, None, None, *weights)
    y_sample, a_rows_s, b_rows_s, b_win_s, c_win_s = trunk(
        x_sample, cache_mem_kv, cache_a_kv, cache_b_kv, cache_b_win, cache_c_win, page_table, *weights)
    return (y_prompt, y_sample, a_rows_p, a_rows_s, b_rows_p, b_rows_s, b_win_p, b_win_s, c_win_p, c_win_s, mem_kv_prompt)
```

```python
import functools
import math

import jax
import jax.numpy as jnp
import numpy as np
from jax import lax
from jax.experimental import pallas as pl
from jax.experimental.pallas import tpu as pltpu

D_MODEL = 1024
DEPTH = 4
PAGE_SIZE = 128
HEAD_DIM = 64
A_HEADS = 8
A_IDX_HEADS = 4
A_IDX_DIM = 64
A_TOPK = 256
B_HEADS = 8
B_KV_HEADS = 2
B_BLOCK = 64
B_TOPN = 16
B_WINDOW = 512
C_HEADS = 16
C_KV_HEADS = 2
C_WINDOW = 128
X_HEADS = 4
X_HEAD_DIM = 128
D_FF = 2816
N_BUCKETS = 32
BUCKET_MAX_DIST = 128
QBLOCK = 128
LN_EPS = 1e-5
ALPHA = (2 * DEPTH) ** 0.25
NEG = -1e30
FORCE_SCORE = 1e4
EVEN_SPLITS = [A_HEADS * HEAD_DIM, HEAD_DIM, HEAD_DIM, A_IDX_HEADS * A_IDX_DIM, A_IDX_DIM, A_IDX_HEADS,
               B_HEADS * HEAD_DIM] + [B_KV_HEADS * HEAD_DIM] * 6 + [B_HEADS * 3]
ODD_SPLITS = [C_HEADS * HEAD_DIM, C_KV_HEADS * HEAD_DIM, C_KV_HEADS * HEAD_DIM]
A_MIX = A_HEADS * HEAD_DIM
B_MIX = B_HEADS * HEAD_DIM
C_MIX = C_HEADS * HEAD_DIM
X_WIDTH = X_HEADS * X_HEAD_DIM

LANES = 128
VMEM_LIMIT = 56 << 20
BF16 = jnp.bfloat16
F32 = jnp.float32


def _params(*sem):
    return pltpu.CompilerParams(dimension_semantics=sem, vmem_limit_bytes=VMEM_LIMIT)


def _resident(shape):
    return pl.BlockSpec(shape, lambda *_: (0,) * len(shape), pipeline_mode=pl.Buffered(1))


def _row_tile(n):
    return min(n, 512)


def _layer_norm_rows(y, g, b):
    mu = jnp.mean(y, axis=-1, keepdims=True)
    d = y - mu
    var = jnp.mean(d * d, axis=-1, keepdims=True)
    return d * lax.rsqrt(var + LN_EPS) * g + b


FF_CHUNKS = 2
FF_CHUNK = D_FF // FF_CHUNKS


def _ffn_ln_kernel(x_ref, wg_ref, wu_ref, wd_ref, g_ref, b_ref, o_ref):
    x = x_ref[...]
    xb = x.astype(BF16)
    acc = jnp.zeros(x.shape, F32)
    for c in range(FF_CHUNKS):
        cols = slice(c * FF_CHUNK, (c + 1) * FF_CHUNK)
        gate = jnp.dot(xb, wg_ref[:, cols], preferred_element_type=F32)
        up = jnp.dot(xb, wu_ref[:, cols], preferred_element_type=F32)
        act = (gate * jax.nn.sigmoid(gate) * up).astype(BF16)
        acc = acc + jnp.dot(act, wd_ref[cols, :], preferred_element_type=F32)
    o_ref[...] = _layer_norm_rows(ALPHA * x + 0.5 * acc, g_ref[...], b_ref[...])


def _ffn_ln(x, wg, wu, wd, g, b):
    n = x.shape[0]
    tm = _row_tile(n)
    return pl.pallas_call(
        _ffn_ln_kernel,
        grid=(n // tm,),
        in_specs=[pl.BlockSpec((tm, D_MODEL), lambda i: (i, 0)),
                  _resident((D_MODEL, D_FF)), _resident((D_MODEL, D_FF)), _resident((D_FF, D_MODEL)),
                  _resident((1, D_MODEL)), _resident((1, D_MODEL))],
        out_specs=pl.BlockSpec((tm, D_MODEL), lambda i: (i, 0)),
        out_shape=jax.ShapeDtypeStruct((n, D_MODEL), F32),
        compiler_params=_params("parallel"),
        name="ffn_ln",
    )(x, wg, wu, wd, g.reshape(1, D_MODEL), b.reshape(1, D_MODEL))


def _linear_kernel(x_ref, w_ref, o_ref):
    o_ref[...] = jnp.dot(x_ref[...].astype(BF16), w_ref[...], preferred_element_type=F32)


def _linear(x, w):
    n, k = x.shape
    m = w.shape[1]
    tm = _row_tile(n)
    return pl.pallas_call(
        _linear_kernel,
        grid=(n // tm,),
        in_specs=[pl.BlockSpec((tm, k), lambda i: (i, 0)), _resident((k, m))],
        out_specs=pl.BlockSpec((tm, m), lambda i: (i, 0)),
        out_shape=jax.ShapeDtypeStruct((n, m), F32),
        compiler_params=_params("parallel"),
        name="linear",
    )(x, w)


def _linear_res_ln_kernel(x_ref, w_ref, h_ref, g_ref, b_ref, o_ref):
    y = jnp.dot(x_ref[...].astype(BF16), w_ref[...], preferred_element_type=F32)
    o_ref[...] = _layer_norm_rows(ALPHA * h_ref[...] + y, g_ref[...], b_ref[...])


def _linear_res_ln(x, w, h, g, b):
    n, k = x.shape
    tm = _row_tile(n)
    return pl.pallas_call(
        _linear_res_ln_kernel,
        grid=(n // tm,),
        in_specs=[pl.BlockSpec((tm, k), lambda i: (i, 0)), _resident((k, D_MODEL)),
                  pl.BlockSpec((tm, D_MODEL), lambda i: (i, 0)),
                  _resident((1, D_MODEL)), _resident((1, D_MODEL))],
        out_specs=pl.BlockSpec((tm, D_MODEL), lambda i: (i, 0)),
        out_shape=jax.ShapeDtypeStruct((n, D_MODEL), F32),
        compiler_params=_params("parallel"),
        name="linear_res_ln",
    )(x, w, h, g.reshape(1, D_MODEL), b.reshape(1, D_MODEL))


def _pad_cols(w, mult=LANES):
    m = w.shape[-1]
    pad = (-m) % mult
    return jnp.pad(w, ((0, 0),) * (w.ndim - 1) + ((0, pad),)) if pad else w


def masked_softmax(logits, mask):
    return jax.nn.softmax(jnp.where(mask, logits.astype(jnp.float32), NEG), axis=-1)


def split_cols(a, sizes):
    offs = np.cumsum(sizes)[:-1].tolist()
    return jnp.split(a, offs, axis=-1)


def t5_bucket(dist):
    exact = N_BUCKETS // 2
    d = jnp.maximum(dist, 0)
    rel = jnp.log(jnp.maximum(d, 1).astype(jnp.float32) / exact) / math.log(BUCKET_MAX_DIST / exact)
    large = jnp.minimum(exact + (rel * (N_BUCKETS - exact)).astype(jnp.int32), N_BUCKETS - 1)
    return jnp.where(d < exact, d, large)


def map_query_blocks(fn, tensors, T):
    qb = min(QBLOCK, T)
    nb = T // qb

    def split(a):
        return jnp.moveaxis(a.reshape((a.shape[0], nb, qb) + a.shape[2:]), 1, 0)
    out = lax.map(lambda args: fn(*args), (jnp.arange(nb),) + tuple(split(a) for a in tensors))
    out = jnp.moveaxis(out, 0, 1)
    return out.reshape((out.shape[0], T) + out.shape[3:])


def fetch(src, b, idx, extra):
    pool, table, new, past, layer = src
    if pool is None:
        return new[(b, jnp.minimum(idx, new.shape[1] - 1)) + extra]
    pi = jnp.minimum(idx, past - 1)
    phys = table[b, pi // PAGE_SIZE]
    old = pool[(layer, phys, pi % PAGE_SIZE) + extra]
    cur = new[(b, jnp.clip(idx - past, 0, new.shape[1] - 1)) + extra]
    return jnp.where((idx < past)[..., None], old, cur)


def window_context(kv_new, buf, window):
    T = kv_new.shape[1]
    if buf is None:
        full = kv_new
        keep = min(window, T)
    else:
        full = jnp.concatenate([buf, kv_new], axis=1)
        keep = buf.shape[1]
    new_state = full[:, full.shape[1] - keep:]
    ctx = jnp.pad(full, ((0, 0), (window + T - full.shape[1], 0), (0, 0), (0, 0), (0, 0)))
    return ctx, new_state


def window_attend(qb, win_k, win_v, i, q0, window, bias_gr, sinks):
    nq = qb.shape[1]
    span = window + nq
    k = lax.dynamic_slice_in_dim(win_k, i * nq, span, axis=1)
    v = lax.dynamic_slice_in_dim(win_v, i * nq, span, axis=1)
    tpos = q0 + i * nq + jnp.arange(nq)
    kpos = q0 - window + i * nq + jnp.arange(span)
    dist = tpos[:, None] - kpos[None, :]
    valid = (kpos[None, :] >= 0) & (dist >= 0) & (dist < window)
    logits = jnp.einsum('bqgrd,bkgd->bqgrk', qb, k) * HEAD_DIM ** -0.5
    logits = logits + jnp.transpose(bias_gr[t5_bucket(dist)], (0, 2, 3, 1))[None]
    logits = jnp.where(valid[None, :, None, None, :], logits.astype(jnp.float32), NEG)
    if sinks is None:
        p = jax.nn.softmax(logits, axis=-1)
    else:
        sink = jnp.broadcast_to(sinks.astype(jnp.float32)[None, None, :, :, None], logits.shape[:-1] + (1,))
        p = jax.nn.softmax(jnp.concatenate([logits, sink], axis=-1), axis=-1)[..., :-1]
    return jnp.einsum('bqgrk,bkgd->bqgrd', p.astype(v.dtype), v)


def dsa_mixer(q, q_idx, w_idx, src, L, q0, bias_a):
    Bn, T = q.shape[:2]
    n_keep = min(A_TOPK, L // 4)
    bidx = jnp.arange(Bn)
    kpos = jnp.arange(L)
    k_idx_all = fetch(src, bidx[:, None], kpos[None, :], (2,))

    def block(i, qb, qib, wb):
        nq = qb.shape[1]
        tpos = q0 + i * nq + jnp.arange(nq)
        s = jax.nn.relu(jnp.einsum('bqhd,bsd->bqhs', qib, k_idx_all) * A_IDX_DIM ** -0.5)
        score = jnp.einsum('bqh,bqhs->bqs', wb, s).astype(jnp.float32) * A_IDX_HEADS ** -0.5
        score = jnp.where(kpos[None, None, :] <= tpos[None, :, None], score, NEG)
        _, sel = lax.top_k(score, n_keep)
        valid = sel <= tpos[None, :, None]
        k = fetch(src, bidx[:, None, None], sel, (0,))
        v = fetch(src, bidx[:, None, None], sel, (1,))
        logits = jnp.einsum('bqhd,bqkd->bqhk', qb, k) * HEAD_DIM ** -0.5
        logits = logits + jnp.moveaxis(bias_a[t5_bucket(tpos[None, :, None] - sel)], -1, 2)
        p = masked_softmax(logits, valid[:, :, None, :]).astype(v.dtype)
        return jnp.einsum('bqhk,bqkd->bqhd', p, v)
    return map_query_blocks(block, (q, q_idx, w_idx), T)


def nsa_compress(raw, pos_w, w1, w2):
    Bn, n, G, d = raw.shape
    blocks = raw.reshape(Bn, n // B_BLOCK, B_BLOCK, G, d)
    pooled = jnp.einsum('bnlgd,ld->bngd', blocks, pos_w)
    return jax.nn.gelu(pooled @ w1) @ w2


def nsa_mixer(q, gate_logits, src, win_ctx, L, q0, cmp_pos, cmp_w1, cmp_w2, bias_b):
    Bn, T = q.shape[:2]
    R = B_HEADS // B_KV_HEADS
    qg = q.reshape(Bn, T, B_KV_HEADS, R, HEAD_DIM)
    gates = jax.nn.sigmoid(gate_logits.astype(jnp.float32)).astype(q.dtype).reshape(Bn, T, B_KV_HEADS, R, 3)
    bidx = jnp.arange(Bn)
    gidx = jnp.arange(B_KV_HEADS)
    nblk = L // B_BLOCK
    nsb = -(-L // B_BLOCK)
    n_sel = min(B_TOPN, nsb)
    kpos = jnp.arange(nblk * B_BLOCK)
    raw_k = fetch(src, bidx[:, None, None], kpos[None, :, None], (0, gidx[None, None, :]))
    raw_v = fetch(src, bidx[:, None, None], kpos[None, :, None], (1, gidx[None, None, :]))
    k_cmp = nsa_compress(raw_k, cmp_pos[0], cmp_w1[0], cmp_w2[0])
    v_cmp = nsa_compress(raw_v, cmp_pos[1], cmp_w1[1], cmp_w2[1])
    c_end = (jnp.arange(nblk) + 1) * B_BLOCK - 1
    win_k, win_v = win_ctx[:, :, 0], win_ctx[:, :, 1]
    scale = HEAD_DIM ** -0.5
    jb = jnp.arange(nsb)

    def block(i, qb, gb):
        nq = qb.shape[1]
        tpos = q0 + i * nq + jnp.arange(nq)
        dist_c = tpos[:, None] - c_end[None, :]
        valid_c = (dist_c >= 0)[None, :, None, None, :]
        lc = jnp.einsum('bqgrd,bngd->bqgrn', qb, k_cmp) * scale
        lc = lc + jnp.moveaxis(bias_b[t5_bucket(dist_c)], 1, -1)[None]
        pc = masked_softmax(lc, valid_c) * valid_c
        o_cmp = jnp.einsum('bqgrn,bngd->bqgrd', pc.astype(v_cmp.dtype), v_cmp)
        cur = tpos // B_BLOCK
        imp = jnp.pad(pc.sum(axis=3), ((0, 0), (0, 0), (0, 0), (0, nsb - nblk)))
        forced = (jb[None, :] == 0) | (jb[None, :] == cur[:, None]) | (jb[None, :] == cur[:, None] - 1)
        adm = jb[None, :] <= cur[:, None]
        imp = jnp.where(adm[None, :, None, :], imp + FORCE_SCORE * forced[None, :, None, :], NEG)
        _, selb = lax.top_k(imp, n_sel)
        tok = (selb[..., None] * B_BLOCK + jnp.arange(B_BLOCK)).reshape(Bn, nq, B_KV_HEADS, n_sel * B_BLOCK)
        valid_s = tok <= tpos[None, :, None, None]
        ks = fetch(src, bidx[:, None, None, None], tok, (2, gidx[None, None, :, None]))
        vs = fetch(src, bidx[:, None, None, None], tok, (3, gidx[None, None, :, None]))
        ls = jnp.einsum('bqgrd,bqgkd->bqgrk', qb, ks) * scale
        bs = bias_b[t5_bucket(tpos[None, :, None, None] - tok), gidx[None, None, :, None]]
        ls = ls + jnp.moveaxis(bs, -1, 3)
        ps = masked_softmax(ls, valid_s[:, :, :, None, :])
        o_sel = jnp.einsum('bqgrk,bqgkd->bqgrd', ps.astype(vs.dtype), vs)
        o_win = window_attend(qb, win_k, win_v, i, q0, B_WINDOW, bias_b, None)
        return gb[..., 0:1] * o_cmp + gb[..., 1:2] * o_sel + gb[..., 2:3] * o_win
    return map_query_blocks(block, (qg, gates), T)


def swa_mixer(q, win_ctx, q0, sinks, bias_c):
    Bn, T = q.shape[:2]
    R = C_HEADS // C_KV_HEADS
    qg = q.reshape(Bn, T, C_KV_HEADS, R, HEAD_DIM)
    sink_gr = sinks.reshape(C_KV_HEADS, R)
    win_k, win_v = win_ctx[:, :, 0], win_ctx[:, :, 1]

    def block(i, qb):
        return window_attend(qb, win_k, win_v, i, q0, C_WINDOW, bias_c, sink_gr)
    return map_query_blocks(block, (qg,), T)


def cross_attn_heads(q, mem_k, mem_v):
    s = jnp.einsum('bthd,bmhd->bthm', q, mem_k) * X_HEAD_DIM ** -0.5
    p = jax.nn.softmax(s.astype(jnp.float32), axis=-1).astype(mem_v.dtype)
    return jnp.einsum('bthm,bmhd->bthd', p, mem_v)


def _trunk(x, mem_kv, a_pool, b_pool, b_win, c_win, page_table, past_len, w):
    prompt = a_pool is None
    Bn, T, _ = x.shape
    n = Bn * T
    q0 = 0 if prompt else past_len
    L = q0 + T
    rel_table = w['rel_table']
    bias_a = rel_table[:, :A_HEADS]
    bias_b = rel_table[:, A_HEADS:].reshape(N_BUCKETS, B_KV_HEADS, B_HEADS // B_KV_HEADS)
    bias_c = rel_table.reshape(N_BUCKETS, C_KV_HEADS, C_HEADS // C_KV_HEADS)
    a_rows, b_rows, b_states, c_states = [], [], [], []
    h = x.reshape(n, D_MODEL)
    for l in range(DEPTH):
        g, bb = w['ln_g'][l], w['ln_b'][l]
        h = _ffn_ln(h, w['ffn_wg'][l, 0], w['ffn_wu'][l, 0], w['ffn_wd'][l, 0], g[0], bb[0])
        if l % 2 == 0:
            e = l // 2
            proj = _linear(h, w['even_w_in'][e])[:, :sum(EVEN_SPLITS)].reshape(Bn, T, -1)
            (q_a, k_a, v_a, qi, ki, wi, q_b, kc, vc, ksl, vsl, kw, vw, gl) = split_cols(proj, EVEN_SPLITS)
            kvs = (Bn, T, B_KV_HEADS, HEAD_DIM)
            new_a = jnp.stack([k_a, v_a, ki], axis=2)
            new_b = jnp.stack([t.reshape(kvs) for t in (kc, vc, ksl, vsl)], axis=2)
            win_ctx, b_state = window_context(jnp.stack([kw.reshape(kvs), vw.reshape(kvs)], axis=2),
                                              None if prompt else b_win[e], B_WINDOW)
            src_a = (a_pool, page_table, new_a, q0, e)
            src_b = (b_pool, page_table, new_b, q0, e)
            o_a = dsa_mixer(q_a.reshape(Bn, T, A_HEADS, HEAD_DIM), qi.reshape(Bn, T, A_IDX_HEADS, A_IDX_DIM),
                            wi, src_a, L, q0, bias_a)
            o_b = nsa_mixer(q_b.reshape(Bn, T, B_HEADS, HEAD_DIM), gl, src_b, win_ctx, L, q0,
                            w['nsa_cmp_pos'][e], w['nsa_cmp_w1'][e], w['nsa_cmp_w2'][e], bias_b)
            mix_in = jnp.concatenate([o_a.reshape(n, A_MIX), o_b.reshape(n, B_MIX)], axis=-1)
            h = _linear_res_ln(mix_in, w['even_w_out'][e], h, g[1], bb[1])
            a_rows.append(new_a)
            b_rows.append(new_b)
            b_states.append(b_state)
        else:
            oi = l // 2
            proj = _linear(h, w['odd_w_in'][oi]).reshape(Bn, T, -1)
            q_c, kcw, vcw = split_cols(proj, ODD_SPLITS)
            kvs = (Bn, T, C_KV_HEADS, HEAD_DIM)
            win_ctx, c_state = window_context(jnp.stack([kcw.reshape(kvs), vcw.reshape(kvs)], axis=2),
                                              None if prompt else c_win[oi], C_WINDOW)
            o_c = swa_mixer(q_c.reshape(Bn, T, C_HEADS, HEAD_DIM), win_ctx, q0, w['c_sinks'][oi], bias_c)
            h = _linear_res_ln(o_c.reshape(n, C_MIX), w['odd_w_out'][oi], h, g[1], bb[1])
            c_states.append(c_state)
        xq = _linear(h, w['x_wq'][l]).reshape(Bn, T, X_HEADS, X_HEAD_DIM)
        xo = cross_attn_heads(xq, mem_kv[l, :, :, 0], mem_kv[l, :, :, 1]).reshape(n, X_WIDTH)
        h = _linear_res_ln(xo, w['x_wo'][l], h, g[2], bb[2])
        h = _ffn_ln(h, w['ffn_wg'][l, 1], w['ffn_wu'][l, 1], w['ffn_wd'][l, 1], g[3], bb[3])
    return (h.reshape(Bn, T, D_MODEL), jnp.stack(a_rows), jnp.stack(b_rows), jnp.stack(b_states),
            jnp.stack(c_states))


def kernel(x_prompt, x_sample, cache_a_kv, cache_b_kv, cache_b_win, cache_c_win, cache_mem_kv, page_table,
           mem_prompt, ln_g, ln_b, ffn_wg, ffn_wu, ffn_wd, even_w_in, even_w_out, nsa_cmp_pos, nsa_cmp_w1,
           nsa_cmp_w2, odd_w_in, odd_w_out, c_sinks, x_wq, x_wk, x_wv, x_wo, rel_table):
    nb, n_mem = mem_prompt.shape[:2]
    past_len = page_table.shape[1] * PAGE_SIZE
    w = dict(ln_g=ln_g, ln_b=ln_b, ffn_wg=ffn_wg.astype(BF16), ffn_wu=ffn_wu.astype(BF16),
             ffn_wd=ffn_wd.astype(BF16), even_w_in=_pad_cols(even_w_in).astype(BF16),
             even_w_out=even_w_out.astype(BF16), nsa_cmp_pos=nsa_cmp_pos, nsa_cmp_w1=nsa_cmp_w1,
             nsa_cmp_w2=nsa_cmp_w2, odd_w_in=odd_w_in.astype(BF16), odd_w_out=odd_w_out.astype(BF16),
             c_sinks=c_sinks, x_wq=x_wq.astype(BF16), x_wo=x_wo.astype(BF16), rel_table=rel_table)
    w_kv = jnp.concatenate([x_wk, x_wv], axis=-1).astype(BF16)
    mem2d = mem_prompt.reshape(nb * n_mem, D_MODEL)
    mem_kv_prompt = jnp.stack([_linear(mem2d, w_kv[l]) for l in range(DEPTH)])
    mem_kv_prompt = mem_kv_prompt.reshape(DEPTH, nb, n_mem, 2, X_HEADS, X_HEAD_DIM)
    y_p, a_p, b_p, bw_p, cw_p = _trunk(x_prompt, mem_kv_prompt, None, None, None, None, None, past_len, w)
    y_s, a_s, b_s, bw_s, cw_s = _trunk(x_sample, cache_mem_kv, cache_a_kv, cache_b_kv, cache_b_win,
                                       cache_c_win, page_table, past_len, w)
    return (y_p, y_s, a_p, a_s, b_p, b_s, bw_p, bw_s, cw_p, cw_s, mem_kv_prompt)
```

```python
import functools
import math

import jax
import jax.numpy as jnp
import numpy as np
from jax import lax
from jax.experimental import pallas as pl
from jax.experimental.pallas import tpu as pltpu

D_MODEL = 1024
DEPTH = 4
PAGE_SIZE = 128
HEAD_DIM = 64
A_HEADS = 8
A_IDX_HEADS = 4
A_IDX_DIM = 64
A_TOPK = 256
B_HEADS = 8
B_KV_HEADS = 2
B_BLOCK = 64
B_TOPN = 16
B_WINDOW = 512
C_HEADS = 16
C_KV_HEADS = 2
C_WINDOW = 128
X_HEADS = 4
X_HEAD_DIM = 128
D_FF = 2816
N_BUCKETS = 32
BUCKET_MAX_DIST = 128
QBLOCK = 128
LN_EPS = 1e-5
ALPHA = (2 * DEPTH) ** 0.25
NEG = -1e30
FORCE_SCORE = 1e4
EVEN_SPLITS = [A_HEADS * HEAD_DIM, HEAD_DIM, HEAD_DIM, A_IDX_HEADS * A_IDX_DIM, A_IDX_DIM, A_IDX_HEADS,
               B_HEADS * HEAD_DIM] + [B_KV_HEADS * HEAD_DIM] * 6 + [B_HEADS * 3]
ODD_SPLITS = [C_HEADS * HEAD_DIM, C_KV_HEADS * HEAD_DIM, C_KV_HEADS * HEAD_DIM]
A_MIX = A_HEADS * HEAD_DIM
B_MIX = B_HEADS * HEAD_DIM
C_MIX = C_HEADS * HEAD_DIM
X_WIDTH = X_HEADS * X_HEAD_DIM

LANES = 128
VMEM_LIMIT = 56 << 20
BF16 = jnp.bfloat16
F32 = jnp.float32


def _params(*sem):
    return pltpu.CompilerParams(dimension_semantics=sem, vmem_limit_bytes=VMEM_LIMIT)


def _resident(shape):
    return pl.BlockSpec(shape, lambda *_: (0,) * len(shape), pipeline_mode=pl.Buffered(1))


def _per_batch(shape, col_block):
    return pl.BlockSpec(shape, lambda b, i: (b, 0, col_block), pipeline_mode=pl.Buffered(1))


def _row_tile(n):
    return min(n, 512)


def _layer_norm_rows(y, g, b):
    mu = jnp.mean(y, axis=-1, keepdims=True)
    d = y - mu
    var = jnp.mean(d * d, axis=-1, keepdims=True)
    return d * lax.rsqrt(var + LN_EPS) * g + b


FF_CHUNKS = 2
FF_CHUNK = D_FF // FF_CHUNKS


def _ffn_ln_kernel(x_ref, wg_ref, wu_ref, wd_ref, g_ref, b_ref, o_ref):
    x = x_ref[...]
    xb = x.astype(BF16)
    acc = jnp.zeros(x.shape, F32)
    for c in range(FF_CHUNKS):
        cols = slice(c * FF_CHUNK, (c + 1) * FF_CHUNK)
        gate = jnp.dot(xb, wg_ref[:, cols], preferred_element_type=F32)
        up = jnp.dot(xb, wu_ref[:, cols], preferred_element_type=F32)
        act = (gate * jax.nn.sigmoid(gate) * up).astype(BF16)
        acc = acc + jnp.dot(act, wd_ref[cols, :], preferred_element_type=F32)
    o_ref[...] = _layer_norm_rows(ALPHA * x + 0.5 * acc, g_ref[...], b_ref[...])


def _ffn_ln(x, wg, wu, wd, g, b):
    n = x.shape[0]
    tm = _row_tile(n)
    return pl.pallas_call(
        _ffn_ln_kernel,
        grid=(n // tm,),
        in_specs=[pl.BlockSpec((tm, D_MODEL), lambda i: (i, 0)),
                  _resident((D_MODEL, D_FF)), _resident((D_MODEL, D_FF)), _resident((D_FF, D_MODEL)),
                  _resident((1, D_MODEL)), _resident((1, D_MODEL))],
        out_specs=pl.BlockSpec((tm, D_MODEL), lambda i: (i, 0)),
        out_shape=jax.ShapeDtypeStruct((n, D_MODEL), F32),
        compiler_params=_params("parallel"),
        name="ffn_ln",
    )(x, wg, wu, wd, g.reshape(1, D_MODEL), b.reshape(1, D_MODEL))


def _linear_kernel(x_ref, w_ref, o_ref):
    o_ref[...] = jnp.dot(x_ref[...].astype(BF16), w_ref[...], preferred_element_type=F32)


def _linear(x, w):
    n, k = x.shape
    m = w.shape[1]
    tm = _row_tile(n)
    return pl.pallas_call(
        _linear_kernel,
        grid=(n // tm,),
        in_specs=[pl.BlockSpec((tm, k), lambda i: (i, 0)), _resident((k, m))],
        out_specs=pl.BlockSpec((tm, m), lambda i: (i, 0)),
        out_shape=jax.ShapeDtypeStruct((n, m), F32),
        compiler_params=_params("parallel"),
        name="linear",
    )(x, w)


def _linear_res_ln_kernel(x_ref, w_ref, h_ref, g_ref, b_ref, o_ref):
    y = jnp.dot(x_ref[...].astype(BF16), w_ref[...], preferred_element_type=F32)
    o_ref[...] = _layer_norm_rows(ALPHA * h_ref[...] + y, g_ref[...], b_ref[...])


def _linear_res_ln(x, w, h, g, b):
    n, k = x.shape
    tm = _row_tile(n)
    return pl.pallas_call(
        _linear_res_ln_kernel,
        grid=(n // tm,),
        in_specs=[pl.BlockSpec((tm, k), lambda i: (i, 0)), _resident((k, D_MODEL)),
                  pl.BlockSpec((tm, D_MODEL), lambda i: (i, 0)),
                  _resident((1, D_MODEL)), _resident((1, D_MODEL))],
        out_specs=pl.BlockSpec((tm, D_MODEL), lambda i: (i, 0)),
        out_shape=jax.ShapeDtypeStruct((n, D_MODEL), F32),
        compiler_params=_params("parallel"),
        name="linear_res_ln",
    )(x, w, h, g.reshape(1, D_MODEL), b.reshape(1, D_MODEL))


def t5_bucket(dist):
    exact = N_BUCKETS // 2
    d = jnp.maximum(dist, 0)
    rel = jnp.log(jnp.maximum(d, 1).astype(jnp.float32) / exact) / math.log(BUCKET_MAX_DIST / exact)
    large = jnp.minimum(exact + (rel * (N_BUCKETS - exact)).astype(jnp.int32), N_BUCKETS - 1)
    return jnp.where(d < exact, d, large)


TK = 128
INT_MIN = -2 ** 31
NT = (((1,), (1,)), ((), ()))


def _ordered_key(x):
    bits = pltpu.bitcast(x, jnp.int32)
    key = jnp.where(bits < 0, bits ^ jnp.int32(0x7FFFFFFF), bits)
    return jnp.where(x == 0.0, 0, key)


def _count(key_scr, nkt, rows, pred):
    def body(j, c):
        return c + jnp.where(pred(key_scr[j], j), 1.0, 0.0)
    c = lax.fori_loop(0, nkt, body, jnp.zeros((rows, TK), F32))
    return jnp.sum(c, axis=1, keepdims=True)


def _topk_threshold(key_scr, nkt, rows, k, idx_bits):
    lane = lax.broadcasted_iota(jnp.int32, (rows, TK), 1)

    def value_bit(it, ans):
        cand = ans | lax.shift_left(jnp.int32(1), 31 - it)
        cand_key = cand ^ jnp.int32(INT_MIN)
        cnt = _count(key_scr, nkt, rows, lambda t, j: t >= cand_key)
        return jnp.where(cnt >= k, cand, ans)
    tau = lax.fori_loop(0, 32, value_bit, jnp.zeros((rows, 1), jnp.int32)) ^ jnp.int32(INT_MIN)
    need = k - _count(key_scr, nkt, rows, lambda t, j: t > tau)

    def index_bit(it, cut):
        cand = cut | lax.shift_left(jnp.int32(1), idx_bits - 1 - it)
        cnt = _count(key_scr, nkt, rows,
                     lambda t, j: jnp.where(t == tau, j * TK + lane, jnp.int32(2 ** 30)) < cand)
        return jnp.where(cnt < need, cand, cut)
    cut = lax.fori_loop(0, idx_bits, index_bit, jnp.zeros((rows, 1), jnp.int32))
    return tau, cut


def _in_topk(t, j, tau, cut):
    lane = lax.broadcasted_iota(jnp.int32, t.shape, 1)
    return (t > tau) | ((t == tau) & (j * TK + lane <= cut))


def _heads_to_rows(x, n_heads):
    return jnp.concatenate([x[:, h * HEAD_DIM:(h + 1) * HEAD_DIM] for h in range(n_heads)], axis=0)


def _flash_init(m_scr, l_scr, acc_scr):
    m_scr[...] = jnp.full(m_scr.shape, NEG, F32)
    l_scr[...] = jnp.zeros(l_scr.shape, F32)
    acc_scr[...] = jnp.zeros(acc_scr.shape, F32)


def _flash_update(m_scr, l_scr, acc_scr, rows, lg, v):
    m_old = m_scr[rows, :]
    m_new = jnp.maximum(m_old, jnp.max(lg, axis=1, keepdims=True))
    alpha = jnp.exp(m_old - m_new)
    p = jnp.exp(lg - m_new)
    l_scr[rows, :] = alpha * l_scr[rows, :] + jnp.sum(p, axis=1, keepdims=True)
    acc_scr[rows, :] = alpha * acc_scr[rows, :] + jnp.dot(p.astype(BF16), v, preferred_element_type=F32)
    m_scr[rows, :] = m_new


def _window_attend(qg, n_r, qr, tiles, band_ref, head0, sink=None):
    row = lax.broadcasted_iota(jnp.int32, (qr, TK), 0)
    lane = lax.broadcasted_iota(jnp.int32, (qr, TK), 1)
    heads = slice(head0, head0 + n_r)
    pieces = []
    for d, (k, _, exists) in enumerate(tiles):
        lg = lax.dot_general(qg, k, NT, preferred_element_type=F32).reshape(n_r, qr, TK)
        ok = jnp.broadcast_to(exists, (qr, TK))
        if d == 0:
            ok = ok & (lane <= row)
            lg = lg + band_ref[heads, 0:qr, TK:2 * TK]
        if d == 1:
            lg = lg + band_ref[heads, 0:qr, 0:TK]
        if d == len(tiles) - 1:
            ok = ok & (lane > row)
        pieces.append(jnp.where(ok[None], lg, NEG).reshape(n_r * qr, TK))
    m = functools.reduce(jnp.maximum, [jnp.max(x, axis=1, keepdims=True) for x in pieces])
    if sink is not None:
        m = jnp.maximum(m, sink)
    ps = [jnp.exp(x - m) for x in pieces]
    den = functools.reduce(lambda a, b: a + b, [jnp.sum(p, axis=1, keepdims=True) for p in ps])
    if sink is not None:
        den = den + jnp.exp(sink - m)
    o = functools.reduce(lambda a, b: a + b,
                         [jnp.dot(p.astype(BF16), t[1], preferred_element_type=F32) for p, t in zip(ps, tiles)])
    return o / den


def _distance_bias(rel_table):
    tb = rel_table[t5_bucket(jnp.arange(2 * TK))]
    return tb - rel_table[N_BUCKETS - 1][None, :]


def _band(dist_bias):
    a = np.arange(TK)[:, None]
    c = np.arange(2 * TK)[None, :]
    idx = np.clip(a - c + TK, 0, 2 * TK - 1)
    return jnp.transpose(dist_bias[idx], (2, 0, 1))


def _dsa_prompt_kernel(q_ref, qi_ref, misc_ref, kv_ref, band_ref, o_ref, key_scr, m_scr, l_scr, acc_scr):
    i = pl.program_id(1)
    nkt = i + 1
    row = lax.broadcasted_iota(jnp.int32, (TK, TK), 0)
    lane = lax.broadcasted_iota(jnp.int32, (TK, TK), 1)
    qi = qi_ref[0].astype(BF16)
    wi = misc_ref[0][:, MISC_WI:MISC_WI + A_IDX_HEADS]

    def score_tile(j, carry):
        rows = pl.ds(pl.multiple_of(j * TK, TK), TK)
        kt = kv_ref[0, rows, 2 * HEAD_DIM:3 * HEAD_DIM].astype(BF16)
        sc = jnp.zeros((TK, TK), F32)
        for h in range(A_IDX_HEADS):
            s = lax.dot_general(qi[:, h * A_IDX_DIM:(h + 1) * A_IDX_DIM], kt, NT, preferred_element_type=F32)
            sc = sc + wi[:, h:h + 1] * jnp.maximum(s * A_IDX_DIM ** -0.5, 0.0)
        sc = sc * A_IDX_HEADS ** -0.5
        sc = jnp.where(j * TK + lane <= i * TK + row, sc, NEG)
        key_scr[j] = _ordered_key(sc)
        return carry
    lax.fori_loop(0, nkt, score_tile, 0)
    tau, cut = _topk_threshold(key_scr, nkt, TK, float(A_TOPK), DSA_IDX_BITS)

    qh = _heads_to_rows((q_ref[0] * HEAD_DIM ** -0.5).astype(BF16), A_HEADS)
    _flash_init(m_scr, l_scr, acc_scr)
    all_rows = slice(None)

    def attend(j, bias, causal):
        rows = pl.ds(pl.multiple_of(j * TK, TK), TK)
        k = kv_ref[0, rows, 0:HEAD_DIM].astype(BF16)
        v = kv_ref[0, rows, HEAD_DIM:2 * HEAD_DIM].astype(BF16)
        lg = lax.dot_general(qh, k, NT, preferred_element_type=F32).reshape(A_HEADS, TK, TK)
        sel = _in_topk(key_scr[j], j, tau, cut)
        if causal:
            sel = sel & (lane <= row)
        if bias is not None:
            lg = lg + bias
        lg = jnp.where(sel[None], lg, NEG).reshape(A_HEADS * TK, TK)
        _flash_update(m_scr, l_scr, acc_scr, all_rows, lg, v)

    def far_tile(j, carry):
        attend(j, None, False)
        return carry
    lax.fori_loop(0, i - 1, far_tile, 0)

    @pl.when(i >= 1)
    def _():
        attend(i - 1, band_ref[:, :, 0:TK], False)
    attend(i, band_ref[:, :, TK:2 * TK], True)
    o = acc_scr[...] / l_scr[...]
    o_ref[0] = o.reshape(A_HEADS, TK, HEAD_DIM)


def _dsa_prompt(proj, band_a):
    bn, t, _ = proj.shape
    nqb = t // TK
    return pl.pallas_call(
        _dsa_prompt_kernel,
        grid=(bn, nqb),
        in_specs=[pl.BlockSpec((1, TK, A_MIX), lambda b, i: (b, i, COL_QA // A_MIX)),
                  pl.BlockSpec((1, TK, 256), lambda b, i: (b, i, COL_QI // 256)),
                  pl.BlockSpec((1, TK, 256), lambda b, i: (b, i, COL_A // 256)),
                  _per_batch((1, t, 256), COL_A // 256),
                  _resident((A_HEADS, TK, 2 * TK))],
        out_specs=pl.BlockSpec((1, A_HEADS, TK, HEAD_DIM), lambda b, i: (b, 0, i, 0)),
        out_shape=jax.ShapeDtypeStruct((bn, A_HEADS, t, HEAD_DIM), F32),
        scratch_shapes=[pltpu.VMEM((nqb, TK, TK), jnp.int32),
                        pltpu.VMEM((A_HEADS * TK, 1), F32), pltpu.VMEM((A_HEADS * TK, 1), F32),
                        pltpu.VMEM((A_HEADS * TK, HEAD_DIM), F32)],
        compiler_params=_params("parallel", "arbitrary"),
        name="dsa_prompt",
    )(proj, proj, proj, proj, band_a)


COL_QA, COL_QB, COL_QI, COL_A = 0, 512, 1024, 1280
COL_B, COL_WIN, EVEN_COLS = 1536, 2048, 2304
MISC_WI = 3 * HEAD_DIM
MISC_GL = MISC_WI + A_IDX_HEADS
DSA_IDX_BITS = 14


def _even_in_perm():
    offs = np.concatenate([[0], np.cumsum(EVEN_SPLITS)])
    seg = lambda k: np.arange(offs[k], offs[k + 1])
    (q_a, k_a, v_a, qi, ki, wi, q_b, kc, vc, ksl, vsl, kw, vw, gl) = [seg(k) for k in range(14)]
    order = [q_a, q_b, qi, k_a, v_a, ki, wi, gl]
    pad = EVEN_COLS - sum(EVEN_SPLITS)
    return np.concatenate(order), pad, np.concatenate([kc, vc, ksl, vsl, kw, vw])


def _permute_even_w_in(w):
    head, pad, tail = _even_in_perm()
    zeros = jnp.zeros(w.shape[:-1] + (pad,), w.dtype)
    return jnp.concatenate([w[..., head], zeros, w[..., tail]], axis=-1)


G_COLS = B_KV_HEADS * HEAD_DIM


def _compress_rows(x, pos_w, w1, w2):
    nb = x.shape[0] // B_BLOCK
    pooled = jnp.sum(x.reshape(nb, B_BLOCK, G_COLS) * pos_w[None], axis=1)
    hid = jax.nn.gelu(jnp.dot(pooled.astype(BF16), w1, preferred_element_type=F32))
    return jnp.dot(hid.astype(BF16), w2, preferred_element_type=F32)


def _compress_kernel(kv_ref, pos_ref, w1_ref, w2_ref, o_ref):
    for s in range(2):
        x = kv_ref[0, :, s * G_COLS:(s + 1) * G_COLS]
        o_ref[0, :, s * G_COLS:(s + 1) * G_COLS] = _compress_rows(x, pos_ref[s], w1_ref[s], w2_ref[s])


def _compress_weights(cmp_pos, cmp_w1, cmp_w2):
    pos = jnp.concatenate([cmp_pos] * B_KV_HEADS, axis=-1)
    eye = jnp.eye(B_KV_HEADS, dtype=cmp_w1.dtype)
    bd = lambda w: jnp.einsum('gh,sde->sgdhe', eye, w).reshape(2, G_COLS, G_COLS).astype(BF16)
    return pos, bd(cmp_w1), bd(cmp_w2)


def _compress_prompt(proj, pos, w1, w2):
    bn, t, _ = proj.shape
    tr = min(t, 2048)
    return pl.pallas_call(
        _compress_kernel,
        grid=(bn, t // tr),
        in_specs=[pl.BlockSpec((1, tr, 2 * G_COLS), lambda b, i: (b, i, COL_B // (2 * G_COLS))),
                  _resident((2, B_BLOCK, G_COLS)), _resident((2, G_COLS, G_COLS)), _resident((2, G_COLS, G_COLS))],
        out_specs=pl.BlockSpec((1, tr // B_BLOCK, 2 * G_COLS), lambda b, i: (b, i, 0)),
        out_shape=jax.ShapeDtypeStruct((bn, t // B_BLOCK, 2 * G_COLS), F32),
        compiler_params=_params("parallel", "parallel"),
        name="nsa_compress",
    )(proj, pos, w1, w2)


B_R = B_HEADS // B_KV_HEADS
NSA_WIN_TILES = B_WINDOW // TK


def _softmax_pieces(pieces):
    m = functools.reduce(jnp.maximum, [jnp.max(x, axis=1, keepdims=True) for x in pieces])
    ps = [jnp.exp(x - m) for x in pieces]
    den = functools.reduce(lambda a, b: a + b, [jnp.sum(p, axis=1, keepdims=True) for p in ps])
    return ps, den, m


def _select_blocks(imp, n_sel, lanef=None):
    if lanef is None:
        lanef = lax.broadcasted_iota(jnp.int32, imp.shape, 1).astype(F32)
    sel = jnp.zeros(imp.shape, F32)
    for _ in range(n_sel):
        mx = jnp.max(imp, axis=1, keepdims=True)
        first = jnp.min(jnp.where(imp == mx, lanef, float(2 ** 24)), axis=1, keepdims=True)
        hit = lanef == first
        sel = jnp.where(hit, 1.0, sel)
        imp = jnp.where(hit, -jnp.inf, imp)
    return sel


def _nsa_prompt_kernel(q_ref, misc_ref, kv_ref, win_ref, cmp_ref, band_ref, cband_ref, o_ref,
                       selm_scr, m_scr, l_scr, acc_scr):
    i = pl.program_id(1)
    n_blk = cmp_ref.shape[1]
    row = lax.broadcasted_iota(jnp.int32, (TK, TK), 0)
    lane = lax.broadcasted_iota(jnp.int32, (TK, TK), 1)
    tpos = i * TK + row
    q = (q_ref[0] * HEAD_DIM ** -0.5).astype(BF16)
    gates = jax.nn.sigmoid(misc_ref[0][:, MISC_GL:MISC_GL + 3 * B_HEADS])
    qg = [_heads_to_rows(q[:, g * B_R * HEAD_DIM:(g + 1) * B_R * HEAD_DIM], B_R) for g in range(B_KV_HEADS)]

    blk = lax.broadcasted_iota(jnp.int32, (TK, n_blk), 1)
    brow = lax.broadcasted_iota(jnp.int32, (TK, n_blk), 0)
    bpos = i * TK + brow
    valid_c = (blk + 1) * B_BLOCK - 1 <= bpos
    cur = bpos // B_BLOCK
    forced = jnp.where((blk == 0) | (blk == cur) | (blk == cur - 1), FORCE_SCORE, 0.0)
    o_cmp = []
    for g in range(B_KV_HEADS):
        kc = cmp_ref[0, :, g * HEAD_DIM:(g + 1) * HEAD_DIM].astype(BF16)
        vc = cmp_ref[0, :, G_COLS + g * HEAD_DIM:G_COLS + (g + 1) * HEAD_DIM].astype(BF16)
        lc = lax.dot_general(qg[g], kc, NT, preferred_element_type=F32).reshape(B_R, TK, n_blk)
        bias = []
        for r in range(B_R):
            cb = cband_ref[g * B_R + r]
            b = jnp.zeros((TK, n_blk), F32)
            for u in range(4):
                b = jnp.where(blk == 2 * i + u - 2, cb[:, u:u + 1], b)
            bias.append(b)
        lc = jnp.where(valid_c[None], lc + jnp.stack(bias), NEG).reshape(B_R * TK, n_blk)
        (p,), den, _ = _softmax_pieces([lc])
        pc = ((p / den).reshape(B_R, TK, n_blk) * jnp.where(valid_c, 1.0, 0.0)[None])
        o_cmp.append(jnp.dot(pc.reshape(B_R * TK, n_blk).astype(BF16), vc, preferred_element_type=F32))
        imp = jnp.sum(pc, axis=0)
        imp = jnp.where(blk <= cur, imp + forced, NEG)
        selm_scr[g] = _select_blocks(imp, min(B_TOPN, n_blk)).astype(BF16)

    _flash_init(m_scr, l_scr, acc_scr)
    ebk = lax.broadcasted_iota(jnp.int32, (n_blk, TK), 0)
    elane = lax.broadcasted_iota(jnp.int32, (n_blk, TK), 1)

    def attend(j, bias_cols, causal):
        rows = pl.ds(pl.multiple_of(j * TK, TK), TK)
        expand = jnp.where(ebk == (j * TK + elane) // B_BLOCK, 1.0, 0.0).astype(BF16)
        for g in range(B_KV_HEADS):
            k = kv_ref[0, rows, g * HEAD_DIM:(g + 1) * HEAD_DIM].astype(BF16)
            v = kv_ref[0, rows, G_COLS + g * HEAD_DIM:G_COLS + (g + 1) * HEAD_DIM].astype(BF16)
            sel = jnp.dot(selm_scr[g], expand, preferred_element_type=F32) > 0.5
            if causal:
                sel = sel & (lane <= row)
            lg = lax.dot_general(qg[g], k, NT, preferred_element_type=F32).reshape(B_R, TK, TK)
            if bias_cols is not None:
                lg = lg + band_ref[g * B_R:(g + 1) * B_R, :, bias_cols]
            lg = jnp.where(sel[None], lg, NEG).reshape(B_R * TK, TK)
            _flash_update(m_scr, l_scr, acc_scr, slice(g * B_R * TK, (g + 1) * B_R * TK), lg, v)

    def far_tile(j, carry):
        attend(j, None, False)
        return carry
    lax.fori_loop(0, i - 1, far_tile, 0)

    @pl.when(i >= 1)
    def _():
        attend(i - 1, slice(0, TK), False)
    attend(i, slice(TK, 2 * TK), True)

    for g in range(B_KV_HEADS):
        tiles = []
        for d in range(NSA_WIN_TILES + 1):
            rows = pl.ds(pl.multiple_of(jnp.maximum(i - d, 0) * TK, TK), TK)
            tiles.append((win_ref[0, rows, g * HEAD_DIM:(g + 1) * HEAD_DIM].astype(BF16),
                          win_ref[0, rows, G_COLS + g * HEAD_DIM:G_COLS + (g + 1) * HEAD_DIM].astype(BF16),
                          i - d >= 0))
        o_win = _window_attend(qg[g], B_R, TK, tiles, band_ref, g * B_R)
        sl = slice(g * B_R * TK, (g + 1) * B_R * TK)
        o_sel = acc_scr[sl, :] / l_scr[sl, :]
        for r in range(B_R):
            h = g * B_R + r
            rr = slice(r * TK, (r + 1) * TK)
            o_ref[0, h] = (gates[:, 3 * h:3 * h + 1] * o_cmp[g][rr] + gates[:, 3 * h + 1:3 * h + 2] * o_sel[rr]
                           + gates[:, 3 * h + 2:3 * h + 3] * o_win[rr])


def _cmp_band(dist_bias):
    a = np.arange(TK)[:, None]
    u = np.arange(4)[None, :] - 2
    idx = np.clip(a - B_BLOCK * u - (B_BLOCK - 1), 0, 2 * TK - 1)
    return jnp.transpose(dist_bias[idx], (2, 0, 1))


def _nsa_prompt(proj, cmp_kv, band_b, cband_b):
    bn, t, _ = proj.shape
    nqb = t // TK
    n_blk = cmp_kv.shape[1]
    return pl.pallas_call(
        _nsa_prompt_kernel,
        grid=(bn, nqb),
        in_specs=[pl.BlockSpec((1, TK, B_MIX), lambda b, i: (b, i, COL_QB // B_MIX)),
                  pl.BlockSpec((1, TK, 256), lambda b, i: (b, i, COL_A // 256)),
                  _per_batch((1, t, 2 * G_COLS), (COL_B + 2 * G_COLS) // (2 * G_COLS)),
                  _per_batch((1, t, 2 * G_COLS), COL_WIN // (2 * G_COLS)),
                  _per_batch((1, n_blk, 2 * G_COLS), 0),
                  _resident((B_HEADS, TK, 2 * TK)), _resident((B_HEADS, TK, 4))],
        out_specs=pl.BlockSpec((1, B_HEADS, TK, HEAD_DIM), lambda b, i: (b, 0, i, 0)),
        out_shape=jax.ShapeDtypeStruct((bn, B_HEADS, t, HEAD_DIM), F32),
        scratch_shapes=[pltpu.VMEM((B_KV_HEADS, TK, n_blk), BF16),
                        pltpu.VMEM((B_HEADS * TK, 1), F32), pltpu.VMEM((B_HEADS * TK, 1), F32),
                        pltpu.VMEM((B_HEADS * TK, HEAD_DIM), F32)],
        compiler_params=_params("parallel", "arbitrary"),
        name="nsa_prompt",
    )(proj, proj, proj, proj, cmp_kv, band_b, cband_b)


C_R = C_HEADS // C_KV_HEADS
COL_QC, COL_CWIN = 0, C_MIX


def _swa_prompt_kernel(q_ref, kv_ref, band_ref, sink_ref, o_ref):
    i = pl.program_id(1)
    row = lax.broadcasted_iota(jnp.int32, (TK, TK), 0)
    lane = lax.broadcasted_iota(jnp.int32, (TK, TK), 1)
    q = (q_ref[0] * HEAD_DIM ** -0.5).astype(BF16)
    prev_rows = pl.ds(pl.multiple_of(jnp.maximum(i - 1, 0) * TK, TK), TK)
    own_rows = pl.ds(pl.multiple_of(i * TK, TK), TK)
    prev_ok = jnp.broadcast_to(i >= 1, (TK, TK)) & (lane > row)
    for g in range(C_KV_HEADS):
        qg = _heads_to_rows(q[:, g * C_R * HEAD_DIM:(g + 1) * C_R * HEAD_DIM], C_R)
        pieces, vals = [], []
        for rows, ok, cols in ((prev_rows, prev_ok, slice(0, TK)), (own_rows, lane <= row, slice(TK, 2 * TK))):
            k = kv_ref[0, rows, g * HEAD_DIM:(g + 1) * HEAD_DIM].astype(BF16)
            vals.append(kv_ref[0, rows, G_COLS + g * HEAD_DIM:G_COLS + (g + 1) * HEAD_DIM].astype(BF16))
            lg = lax.dot_general(qg, k, NT, preferred_element_type=F32).reshape(C_R, TK, TK)
            lg = lg + band_ref[g * C_R:(g + 1) * C_R, :, cols]
            pieces.append(jnp.where(ok[None], lg, NEG).reshape(C_R * TK, TK))
        sink = jnp.concatenate([jnp.broadcast_to(sink_ref[g * C_R + r], (TK, 1)) for r in range(C_R)], axis=0)
        m = jnp.maximum(jnp.maximum(jnp.max(pieces[0], axis=1, keepdims=True),
                                    jnp.max(pieces[1], axis=1, keepdims=True)), sink)
        ps = [jnp.exp(x - m) for x in pieces]
        den = jnp.sum(ps[0], axis=1, keepdims=True) + jnp.sum(ps[1], axis=1, keepdims=True) + jnp.exp(sink - m)
        o = (jnp.dot(ps[0].astype(BF16), vals[0], preferred_element_type=F32)
             + jnp.dot(ps[1].astype(BF16), vals[1], preferred_element_type=F32)) / den
        o_ref[0, g * C_R:(g + 1) * C_R] = o.reshape(C_R, TK, HEAD_DIM)


def _swa_prompt(proj, band_c, sinks):
    bn, t, _ = proj.shape
    return pl.pallas_call(
        _swa_prompt_kernel,
        grid=(bn, t // TK),
        in_specs=[pl.BlockSpec((1, TK, C_MIX), lambda b, i: (b, i, COL_QC // C_MIX)),
                  _per_batch((1, t, 2 * G_COLS), COL_CWIN // (2 * G_COLS)),
                  _resident((C_HEADS, TK, 2 * TK)), _resident((C_HEADS, 1, 1))],
        out_specs=pl.BlockSpec((1, C_HEADS, TK, HEAD_DIM), lambda b, i: (b, 0, i, 0)),
        out_shape=jax.ShapeDtypeStruct((bn, C_HEADS, t, HEAD_DIM), F32),
        compiler_params=_params("parallel", "arbitrary"),
        name="swa_prompt",
    )(proj, proj, band_c, sinks)


def _cross_attn_kernel(q_ref, mem_ref, o_ref):
    q = (q_ref[0] * X_HEAD_DIM ** -0.5).astype(BF16)
    for h in range(X_HEADS):
        cols = slice(h * X_HEAD_DIM, (h + 1) * X_HEAD_DIM)
        k = mem_ref[0, :, cols].astype(BF16)
        v = mem_ref[0, :, X_WIDTH + h * X_HEAD_DIM:X_WIDTH + (h + 1) * X_HEAD_DIM].astype(BF16)
        s = lax.dot_general(q[:, cols], k, NT, preferred_element_type=F32)
        p = jnp.exp(s - jnp.max(s, axis=1, keepdims=True))
        den = jnp.sum(p, axis=1, keepdims=True)
        o_ref[0, :, cols] = jnp.dot(p.astype(BF16), v, preferred_element_type=F32) / den


def _cross_attn(q, mem):
    bn, t, _ = q.shape
    n_mem = mem.shape[1]
    tq = min(t, 512)
    return pl.pallas_call(
        _cross_attn_kernel,
        grid=(bn, t // tq),
        in_specs=[pl.BlockSpec((1, tq, X_WIDTH), lambda b, i: (b, i, 0)),
                  pl.BlockSpec((1, n_mem, 2 * X_WIDTH), lambda b, i: (b, 0, 0))],
        out_specs=pl.BlockSpec((1, tq, X_WIDTH), lambda b, i: (b, i, 0)),
        out_shape=jax.ShapeDtypeStruct((bn, t, X_WIDTH), F32),
        compiler_params=_params("parallel", "arbitrary"),
        name="cross_attn",
    )(q, mem)


def _heads_out_ln_kernel(*refs):
    o_refs, (w_ref, h_ref, g_ref, b_ref, out_ref) = refs[:-5], refs[-5:]
    y = jnp.zeros(h_ref.shape[1:], F32)
    hh = 0
    for o_ref in o_refs:
        for h in range(o_ref.shape[1]):
            y = y + jnp.dot(o_ref[0, h].astype(BF16), w_ref[hh], preferred_element_type=F32)
            hh += 1
    out_ref[0] = _layer_norm_rows(ALPHA * h_ref[0] + y, g_ref[...], b_ref[...])


def _heads_out_ln(outs, w, h, g, b):
    bn, t, _ = h.shape
    tm = min(t, 512)
    return pl.pallas_call(
        _heads_out_ln_kernel,
        grid=(bn, t // tm),
        in_specs=[pl.BlockSpec((1, o.shape[1], tm, HEAD_DIM), lambda b, i: (b, 0, i, 0)) for o in outs]
        + [_resident(w.shape), pl.BlockSpec((1, tm, D_MODEL), lambda b, i: (b, i, 0)),
           _resident((1, D_MODEL)), _resident((1, D_MODEL))],
        out_specs=pl.BlockSpec((1, tm, D_MODEL), lambda b, i: (b, i, 0)),
        out_shape=jax.ShapeDtypeStruct((bn, t, D_MODEL), F32),
        compiler_params=_params("parallel", "parallel"),
        name="heads_out_ln",
    )(*outs, w, h, g.reshape(1, D_MODEL), b.reshape(1, D_MODEL))


QS = 8


def _own_tile(new_rows):
    return jnp.concatenate([new_rows, jnp.zeros((TK - new_rows.shape[0], new_rows.shape[1]), F32)], axis=0)


def _topk_threshold_dense(keys, pos, k, idx_bits):
    count = lambda pred: jnp.sum(jnp.where(pred, 1.0, 0.0), axis=1, keepdims=True)

    def value_bit(it, ans):
        cand = ans | lax.shift_left(jnp.int32(1), 31 - it)
        return jnp.where(count(keys >= (cand ^ jnp.int32(INT_MIN))) >= k, cand, ans)
    tau = lax.fori_loop(0, 32, value_bit, jnp.zeros((keys.shape[0], 1), jnp.int32)) ^ jnp.int32(INT_MIN)
    need = k - count(keys > tau)
    tie_pos = jnp.where(keys == tau, pos, jnp.int32(2 ** 30))

    def index_bit(it, cut):
        cand = cut | lax.shift_left(jnp.int32(1), idx_bits - 1 - it)
        return jnp.where(count(tie_pos < cand) < need, cand, cut)
    cut = lax.fori_loop(0, idx_bits, index_bit, jnp.zeros((keys.shape[0], 1), jnp.int32))
    return tau, cut


def _near_far_softmax(lg, ok, band, n_h, v_ref):
    nk = lg.shape[1]
    lg = lg.reshape(n_h, QS, nk)
    far = jnp.where(ok[None, :, :nk - 2 * TK], lg[:, :, :nk - 2 * TK], NEG).reshape(n_h * QS, nk - 2 * TK)
    near = jnp.where(ok[None, :, nk - 2 * TK:], lg[:, :, nk - 2 * TK:] + band, NEG).reshape(n_h * QS, 2 * TK)
    (p_far, p_near), den, _ = _softmax_pieces([far, near])
    o = (jnp.dot(p_far.astype(BF16), v_ref[0:nk - 2 * TK, :], preferred_element_type=F32)
         + jnp.dot(p_near.astype(BF16), v_ref[nk - 2 * TK:nk, :], preferred_element_type=F32))
    return o / den


def _dsa_sample_kernel(tbl_ref, q_ref, page_ref, band_ref, o_ref, k_scr, v_scr, ki_scr):
    p = pl.program_id(1)
    n_pages = pl.num_programs(1)
    rows = pl.ds(pl.multiple_of(p * TK, TK), TK)
    page = page_ref[0, 0]
    k_scr[rows, :] = page[:, 0:HEAD_DIM].astype(BF16)
    v_scr[rows, :] = page[:, HEAD_DIM:2 * HEAD_DIM].astype(BF16)
    ki_scr[rows, :] = page[:, 2 * HEAD_DIM:3 * HEAD_DIM].astype(BF16)

    @pl.when(p == n_pages - 1)
    def _():
        nk = k_scr.shape[0]
        past = nk - TK
        x = q_ref[0]
        own = _own_tile(x[:, COL_A:COL_A + 3 * HEAD_DIM]).astype(BF16)
        k_scr[past:nk, :] = own[:, 0:HEAD_DIM]
        v_scr[past:nk, :] = own[:, HEAD_DIM:2 * HEAD_DIM]
        ki_scr[past:nk, :] = own[:, 2 * HEAD_DIM:3 * HEAD_DIM]
        pos = lax.broadcasted_iota(jnp.int32, (QS, nk), 1)
        causal = pos <= past + lax.broadcasted_iota(jnp.int32, (QS, nk), 0)
        qi = _heads_to_rows(x[:, COL_QI:COL_QI + A_IDX_HEADS * A_IDX_DIM].astype(BF16), A_IDX_HEADS)
        wi = x[:, COL_A + MISC_WI:COL_A + MISC_WI + A_IDX_HEADS]
        s = lax.dot_general(qi, ki_scr[...], NT, preferred_element_type=F32).reshape(A_IDX_HEADS, QS, nk)
        sc = jnp.zeros((QS, nk), F32)
        for h in range(A_IDX_HEADS):
            sc = sc + wi[:, h:h + 1] * jnp.maximum(s[h] * A_IDX_DIM ** -0.5, 0.0)
        keys = _ordered_key(jnp.where(causal, sc * A_IDX_HEADS ** -0.5, NEG))
        tau, cut = _topk_threshold_dense(keys, pos, float(A_TOPK), DSA_IDX_BITS)
        sel = ((keys > tau) | ((keys == tau) & (pos <= cut))) & causal
        qh = _heads_to_rows((x[:, COL_QA:COL_QA + A_MIX] * HEAD_DIM ** -0.5).astype(BF16), A_HEADS)
        lg = lax.dot_general(qh, k_scr[...], NT, preferred_element_type=F32)
        o = _near_far_softmax(lg, sel, band_ref[:, 0:QS, :], A_HEADS, v_scr)
        o_ref[0] = o.reshape(A_HEADS, QS, HEAD_DIM)


def _dsa_sample(proj, pool, page_table, layer, band_a):
    ns = proj.shape[0]
    n_pages = page_table.shape[1]
    nk = (n_pages + 1) * TK
    grid_spec = pltpu.PrefetchScalarGridSpec(
        num_scalar_prefetch=1, grid=(ns, n_pages),
        in_specs=[pl.BlockSpec((1, QS, EVEN_COLS), lambda b, p, tbl: (b, 0, 0)),
                  pl.BlockSpec((1, 1, PAGE_SIZE, 3 * HEAD_DIM), lambda b, p, tbl: (layer, tbl[b * n_pages + p], 0, 0)),
                  pl.BlockSpec((A_HEADS, TK, 2 * TK), lambda b, p, tbl: (0, 0, 0))],
        out_specs=pl.BlockSpec((1, A_HEADS, QS, HEAD_DIM), lambda b, p, tbl: (b, 0, 0, 0)),
        scratch_shapes=[pltpu.VMEM((nk, HEAD_DIM), BF16)] * 3)
    return pl.pallas_call(
        _dsa_sample_kernel, grid_spec=grid_spec,
        out_shape=jax.ShapeDtypeStruct((ns, A_HEADS, QS, HEAD_DIM), F32),
        compiler_params=_params("parallel", "arbitrary"),
        name="dsa_sample",
    )(page_table.reshape(-1), proj, pool, band_a)


def _nsa_sample_kernel(tbl_ref, q_ref, page_ref, win_ref, pos_ref, w1_ref, w2_ref, band_ref, cband_ref, expand_ref,
                       o_ref, ks_scr, vs_scr, pool_even, pool_odd):
    p = pl.program_id(1)
    n_pages = pl.num_programs(1)
    rows = pl.ds(pl.multiple_of(p * TK, TK), TK)
    page = page_ref[0, 0]
    for g in range(B_KV_HEADS):
        ks_scr[g, rows, :] = page[:, 2 * G_COLS + g * HEAD_DIM:2 * G_COLS + (g + 1) * HEAD_DIM].astype(BF16)
        vs_scr[g, rows, :] = page[:, 3 * G_COLS + g * HEAD_DIM:3 * G_COLS + (g + 1) * HEAD_DIM].astype(BF16)
    pos_w =jnp.concatenate([pos_ref[0], pos_ref[1]], axis=1)
    pool_even[pl.ds(p, 1), :] = jnp.sum(page[0:B_BLOCK, 0:2 * G_COLS] * pos_w, axis=0, keepdims=True)
    pool_odd[pl.ds(p, 1), :] = jnp.sum(page[B_BLOCK:2 * B_BLOCK, 0:2 * G_COLS] * pos_w, axis=0, keepdims=True)

    @pl.when(p == n_pages - 1)
    def _():
        nk = ks_scr.shape[1]
        past = nk - TK
        n_blk = pool_even.shape[0] * 2
        nb_w = expand_ref.shape[0]
        x = q_ref[0]
        own = _own_tile(x[:, COL_B + 2 * G_COLS:COL_B + 4 * G_COLS]).astype(BF16)
        for g in range(B_KV_HEADS):
            ks_scr[g, past:nk, :] = own[:, g * HEAD_DIM:(g + 1) * HEAD_DIM]
            vs_scr[g, past:nk, :] = own[:, G_COLS + g * HEAD_DIM:G_COLS + (g + 1) * HEAD_DIM]
        q = (x[:, COL_QB:COL_QB + B_MIX] * HEAD_DIM ** -0.5).astype(BF16)
        gates = jax.nn.sigmoid(x[:, COL_A + MISC_GL:COL_A + MISC_GL + 3 * B_HEADS])
        qg = [_heads_to_rows(q[:, g * B_R * HEAD_DIM:(g + 1) * B_R * HEAD_DIM], B_R) for g in range(B_KV_HEADS)]
        pooled = jnp.concatenate([pool_even[...], pool_odd[...]], axis=0)
        cmp_kv = []
        for s in range(2):
            hid = jax.nn.gelu(jnp.dot(pooled[:, s * G_COLS:(s + 1) * G_COLS].astype(BF16), w1_ref[s],
                                      preferred_element_type=F32))
            cmp_kv.append(jnp.dot(hid.astype(BF16), w2_ref[s], preferred_element_type=F32).astype(BF16))
        lane = lax.broadcasted_iota(jnp.int32, (QS, nb_w), 1)
        half = n_blk // 2
        blk = jnp.where(lane < half, 2 * lane, jnp.where(lane < n_blk, 2 * (lane - half) + 1,
                                                         jnp.where(lane == n_blk, n_blk, 2 ** 20)))
        tpos = past + lax.broadcasted_iota(jnp.int32, (QS, nb_w), 0)
        cur = tpos // B_BLOCK
        valid_c = ((blk + 1) * B_BLOCK - 1 <= tpos)[:, :n_blk]
        forced = jnp.where((blk == 0) | (blk == cur) | (blk == cur - 1), FORCE_SCORE, 0.0)
        i_own = past // TK
        pos = lax.broadcasted_iota(jnp.int32, (QS, nk), 1)
        causal = pos <= past + lax.broadcasted_iota(jnp.int32, (QS, nk), 0)
        own_win = _own_tile(x[:, COL_WIN:COL_WIN + 2 * G_COLS]).astype(BF16)
        for g in range(B_KV_HEADS):
            gc = slice(g * HEAD_DIM, (g + 1) * HEAD_DIM)
            lc = lax.dot_general(qg[g], cmp_kv[0][:, gc], NT, preferred_element_type=F32).reshape(B_R, QS, n_blk)
            bias = []
            for r in range(B_R):
                cb = cband_ref[g * B_R + r][0:QS, :]
                b = jnp.zeros((QS, n_blk), F32)
                for u in range(4):
                    b = jnp.where(blk[:, :n_blk] == 2 * i_own + u - 2, cb[:, u:u + 1], b)
                bias.append(b)
            lc = jnp.where(valid_c[None], lc + jnp.stack(bias), NEG).reshape(B_R * QS, n_blk)
            (pr,), den, _ = _softmax_pieces([lc])
            pc = (pr / den).reshape(B_R, QS, n_blk) * jnp.where(valid_c, 1.0, 0.0)[None]
            o_cmp = jnp.dot(pc.reshape(B_R * QS, n_blk).astype(BF16), cmp_kv[1][:, gc], preferred_element_type=F32)
            imp = jnp.concatenate([jnp.sum(pc, axis=0), jnp.zeros((QS, nb_w - n_blk), F32)], axis=1)
            imp = jnp.where(blk <= cur, imp + forced, jnp.where(blk < 2 ** 20, NEG, -jnp.inf))
            selm = _select_blocks(imp, B_TOPN, blk.astype(F32)).astype(BF16)
            sel = (jnp.dot(selm, expand_ref[...], preferred_element_type=F32) > 0.5) & causal
            lg = lax.dot_general(qg[g], ks_scr[g], NT, preferred_element_type=F32)
            o_sel = _near_far_softmax(lg, sel, band_ref[g * B_R:(g + 1) * B_R, 0:QS, :], B_R, vs_scr.at[g])
            tiles = [(own_win[:, gc], own_win[:, G_COLS + g * HEAD_DIM:G_COLS + (g + 1) * HEAD_DIM], True)]
            for d in range(1, NSA_WIN_TILES + 1):
                wr = slice((NSA_WIN_TILES - d) * TK, (NSA_WIN_TILES - d + 1) * TK)
                tiles.append((win_ref[0, wr, gc].astype(BF16),
                              win_ref[0, wr, G_COLS + g * HEAD_DIM:G_COLS + (g + 1) * HEAD_DIM].astype(BF16), True))
            o_win = _window_attend(qg[g], B_R, QS, tiles, band_ref, g * B_R)
            for r in range(B_R):
                h = g * B_R + r
                rr = slice(r * QS, (r + 1) * QS)
                o_ref[0, h] = (gates[:, 3 * h:3 * h + 1] * o_cmp[rr] + gates[:, 3 * h + 1:3 * h + 2] * o_sel[rr]
                               + gates[:, 3 * h + 2:3 * h + 3] * o_win[rr])


def _sample_expand(n_pages):
    n_blk = 2 * n_pages
    nb_w = -(-(n_blk + 1) // LANES) * LANES
    lane = np.arange(nb_w)
    blk = np.where(lane < n_pages, 2 * lane, np.where(lane < n_blk, 2 * (lane - n_pages) + 1,
                                                      np.where(lane == n_blk, n_blk, -1)))
    tok_blk = np.arange((n_pages + 1) * TK) // B_BLOCK
    return jnp.asarray(blk[:, None] == tok_blk[None, :], BF16)


def _nsa_sample(proj, pool, win, page_table, layer, cmp_w, band_b, cband_b, expand):
    ns = proj.shape[0]
    n_pages = page_table.shape[1]
    nk = (n_pages + 1) * TK
    const = lambda shape: pl.BlockSpec(shape, lambda b, p, tbl: (0,) * len(shape))
    grid_spec = pltpu.PrefetchScalarGridSpec(
        num_scalar_prefetch=1, grid=(ns, n_pages),
        in_specs=[pl.BlockSpec((1, QS, EVEN_COLS), lambda b, p, tbl: (b, 0, 0)),
                  pl.BlockSpec((1, 1, PAGE_SIZE, 4 * G_COLS), lambda b, p, tbl: (layer, tbl[b * n_pages + p], 0, 0)),
                  pl.BlockSpec((1, B_WINDOW, 2 * G_COLS), lambda b, p, tbl: (b, 0, 0)),
                  const((2, B_BLOCK, G_COLS)), const((2, G_COLS, G_COLS)), const((2, G_COLS, G_COLS)),
                  const((B_HEADS, TK, 2 * TK)), const((B_HEADS, TK, 4)), const(expand.shape)],
        out_specs=pl.BlockSpec((1, B_HEADS, QS, HEAD_DIM), lambda b, p, tbl: (b, 0, 0, 0)),
        scratch_shapes=[pltpu.VMEM((B_KV_HEADS, nk, HEAD_DIM), BF16), pltpu.VMEM((B_KV_HEADS, nk, HEAD_DIM), BF16),
                        pltpu.VMEM((n_pages, 2 * G_COLS), F32), pltpu.VMEM((n_pages, 2 * G_COLS), F32)])
    return pl.pallas_call(
        _nsa_sample_kernel, grid_spec=grid_spec,
        out_shape=jax.ShapeDtypeStruct((ns, B_HEADS, QS, HEAD_DIM), F32),
        compiler_params=_params("parallel", "arbitrary"),
        name="nsa_sample",
    )(page_table.reshape(-1), proj, pool, win, *cmp_w, band_b, cband_b, expand)


def _swa_sample_kernel(q_ref, buf_ref, band_ref, sink_ref, o_ref):
    x = q_ref[0]
    q = (x[:, COL_QC:COL_QC + C_MIX] * HEAD_DIM ** -0.5).astype(BF16)
    own = _own_tile(x[:, COL_CWIN:COL_CWIN + 2 * G_COLS]).astype(BF16)
    for g in range(C_KV_HEADS):
        gc = slice(g * HEAD_DIM, (g + 1) * HEAD_DIM)
        vc = slice(G_COLS + g * HEAD_DIM, G_COLS + (g + 1) * HEAD_DIM)
        qg = _heads_to_rows(q[:, g * C_R * HEAD_DIM:(g + 1) * C_R * HEAD_DIM], C_R)
        tiles = [(own[:, gc], own[:, vc], True),
                 (buf_ref[0, :, gc].astype(BF16), buf_ref[0, :, vc].astype(BF16), True)]
        sink = jnp.concatenate([jnp.broadcast_to(sink_ref[g * C_R + r], (QS, 1)) for r in range(C_R)], axis=0)
        o = _window_attend(qg, C_R, QS, tiles, band_ref, g * C_R, sink)
        o_ref[0, g * C_R:(g + 1) * C_R] = o.reshape(C_R, QS, HEAD_DIM)


def _swa_sample(proj, buf, band_c, sinks):
    ns = proj.shape[0]
    return pl.pallas_call(
        _swa_sample_kernel,
        grid=(ns,),
        in_specs=[pl.BlockSpec((1, QS, C_MIX + 2 * G_COLS), lambda b: (b, 0, 0)),
                  pl.BlockSpec((1, C_WINDOW, 2 * G_COLS), lambda b: (b, 0, 0)),
                  _resident((C_HEADS, TK, 2 * TK)), _resident((C_HEADS, 1, 1))],
        out_specs=pl.BlockSpec((1, C_HEADS, QS, HEAD_DIM), lambda b: (b, 0, 0, 0)),
        out_shape=jax.ShapeDtypeStruct((ns, C_HEADS, QS, HEAD_DIM), F32),
        compiler_params=_params("parallel"),
        name="swa_sample",
    )(proj, buf, band_c, sinks)


def _trunk_prompt(x, mem_kv, w):
    bn, t, _ = x.shape
    n = bn * t
    a_rows, b_rows, b_states, c_states = [], [], [], []
    kb, kc = min(B_WINDOW, t), min(C_WINDOW, t)
    h = x.reshape(n, D_MODEL)
    for l in range(DEPTH):
        g, bb = w['ln_g'][l], w['ln_b'][l]
        h = _ffn_ln(h, w['ffn_wg'][l, 0], w['ffn_wu'][l, 0], w['ffn_wd'][l, 0], g[0], bb[0])
        if l % 2 == 0:
            e = l // 2
            proj = _linear(h, w['even_w_in_perm'][e]).reshape(bn, t, EVEN_COLS)
            cmp_kv = _compress_prompt(proj, *w['cmp'][e])
            o_a = _dsa_prompt(proj, w['band'][:A_HEADS])
            o_b = _nsa_prompt(proj, cmp_kv, w['band'][A_HEADS:], w['cband'][A_HEADS:])
            h = _heads_out_ln([o_a, o_b], w['even_w_out_heads'][e], h.reshape(bn, t, D_MODEL), g[1], bb[1])
            a_rows.append(proj[:, :, COL_A:COL_A + 3 * HEAD_DIM].reshape(bn, t, 3, HEAD_DIM))
            b_rows.append(proj[:, :, COL_B:COL_B + 4 * G_COLS].reshape(bn, t, 4, B_KV_HEADS, HEAD_DIM))
            b_states.append(proj[:, t - kb:, COL_WIN:COL_WIN + 2 * G_COLS].reshape(bn, kb, 2, B_KV_HEADS, HEAD_DIM))
        else:
            oi = l // 2
            proj = _linear(h, w['odd_w_in'][oi]).reshape(bn, t, C_MIX + 2 * G_COLS)
            o_c = _swa_prompt(proj, w['band'], w['sinks_shifted'][oi])
            h = _heads_out_ln([o_c], w['odd_w_out_heads'][oi], h.reshape(bn, t, D_MODEL), g[1], bb[1])
            c_states.append(proj[:, t - kc:, COL_CWIN:].reshape(bn, kc, 2, C_KV_HEADS, HEAD_DIM))
        h = h.reshape(n, D_MODEL)
        xq = _linear(h, w['x_wq'][l]).reshape(bn, t, X_WIDTH)
        xo = _cross_attn(xq, mem_kv[l].reshape(bn, -1, 2 * X_WIDTH)).reshape(n, X_WIDTH)
        h = _linear_res_ln(xo, w['x_wo'][l], h, g[2], bb[2])
        h = _ffn_ln(h, w['ffn_wg'][l, 1], w['ffn_wu'][l, 1], w['ffn_wd'][l, 1], g[3], bb[3])
    return (h.reshape(bn, t, D_MODEL), jnp.stack(a_rows), jnp.stack(b_rows), jnp.stack(b_states),
            jnp.stack(c_states))


def _shift_in(buf, new):
    return jnp.concatenate([buf, new], axis=1)[:, new.shape[1]:]


def _trunk_sample(x, mem_kv, a_pool, b_pool, b_win, c_win, page_table, w):
    ns, t, _ = x.shape
    assert t <= QS and b_win.shape[2] == B_WINDOW and c_win.shape[2] == C_WINDOW
    n = ns * QS
    n_pages = page_table.shape[1]
    a_pool = a_pool.reshape(a_pool.shape[:3] + (3 * HEAD_DIM,))
    b_pool = b_pool.reshape(b_pool.shape[:3] + (4 * G_COLS,))
    expand = _sample_expand(n_pages)
    a_rows, b_rows, b_states, c_states = [], [], [], []
    h = jnp.pad(x, ((0, 0), (0, QS - t), (0, 0))).reshape(n, D_MODEL)
    for l in range(DEPTH):
        g, bb = w['ln_g'][l], w['ln_b'][l]
        h = _ffn_ln(h, w['ffn_wg'][l, 0], w['ffn_wu'][l, 0], w['ffn_wd'][l, 0], g[0], bb[0])
        if l % 2 == 0:
            e = l // 2
            proj = _linear(h, w['even_w_in_perm'][e]).reshape(ns, QS, EVEN_COLS)
            win = b_win[e].reshape(ns, B_WINDOW, 2 * G_COLS)
            o_a = _dsa_sample(proj, a_pool, page_table, e, w['band'][:A_HEADS])
            o_b = _nsa_sample(proj, b_pool, win, page_table, e, w['cmp'][e], w['band'][A_HEADS:],
                              w['cband'][A_HEADS:], expand)
            h = _heads_out_ln([o_a, o_b], w['even_w_out_heads'][e], h.reshape(ns, QS, D_MODEL), g[1], bb[1])
            a_rows.append(proj[:, :t, COL_A:COL_A + 3 * HEAD_DIM].reshape(ns, t, 3, HEAD_DIM))
            b_rows.append(proj[:, :t, COL_B:COL_B + 4 * G_COLS].reshape(ns, t, 4, B_KV_HEADS, HEAD_DIM))
            b_states.append(_shift_in(win, proj[:, :t, COL_WIN:COL_WIN + 2 * G_COLS]).reshape(b_win.shape[1:]))
        else:
            oi = l // 2
            proj = _linear(h, w['odd_w_in'][oi]).reshape(ns, QS, C_MIX + 2 * G_COLS)
            buf = c_win[oi].reshape(ns, C_WINDOW, 2 * G_COLS)
            o_c = _swa_sample(proj, buf, w['band'], w['sinks_shifted'][oi])
            h = _heads_out_ln([o_c], w['odd_w_out_heads'][oi], h.reshape(ns, QS, D_MODEL), g[1], bb[1])
            c_states.append(_shift_in(buf, proj[:, :t, COL_CWIN:]).reshape(c_win.shape[1:]))
        h = h.reshape(n, D_MODEL)
        xq = _linear(h, w['x_wq'][l]).reshape(ns, QS, X_WIDTH)
        xo = _cross_attn(xq, mem_kv[l].reshape(ns, -1, 2 * X_WIDTH)).reshape(n, X_WIDTH)
        h = _linear_res_ln(xo, w['x_wo'][l], h, g[2], bb[2])
        h = _ffn_ln(h, w['ffn_wg'][l, 1], w['ffn_wu'][l, 1], w['ffn_wd'][l, 1], g[3], bb[3])
    return (h.reshape(ns, QS, D_MODEL)[:, :t], jnp.stack(a_rows), jnp.stack(b_rows), jnp.stack(b_states),
            jnp.stack(c_states))


def kernel(x_prompt, x_sample, cache_a_kv, cache_b_kv, cache_b_win, cache_c_win, cache_mem_kv, page_table,
           mem_prompt, ln_g, ln_b, ffn_wg, ffn_wu, ffn_wd, even_w_in, even_w_out, nsa_cmp_pos, nsa_cmp_w1,
           nsa_cmp_w2, odd_w_in, odd_w_out, c_sinks, x_wq, x_wk, x_wv, x_wo, rel_table):
    nb, n_mem = mem_prompt.shape[:2]
    dist_bias = _distance_bias(rel_table)
    w = dict(ln_g=ln_g, ln_b=ln_b, ffn_wg=ffn_wg.astype(BF16), ffn_wu=ffn_wu.astype(BF16),
             ffn_wd=ffn_wd.astype(BF16), odd_w_in=odd_w_in.astype(BF16),
             x_wq=x_wq.astype(BF16), x_wo=x_wo.astype(BF16),
             even_w_in_perm=_permute_even_w_in(even_w_in).astype(BF16),
             even_w_out_heads=even_w_out.astype(BF16).reshape(-1, A_HEADS + B_HEADS, HEAD_DIM, D_MODEL),
             odd_w_out_heads=odd_w_out.astype(BF16).reshape(-1, C_HEADS, HEAD_DIM, D_MODEL),
             band=_band(dist_bias), cband=_cmp_band(dist_bias),
             cmp=[_compress_weights(nsa_cmp_pos[e], nsa_cmp_w1[e], nsa_cmp_w2[e]) for e in range(even_w_in.shape[0])],
             sinks_shifted=(c_sinks - rel_table[N_BUCKETS - 1][None, :]).reshape(-1, C_HEADS, 1, 1))
    w_kv = jnp.concatenate([x_wk, x_wv], axis=-1).astype(BF16)
    mem2d = mem_prompt.reshape(nb * n_mem, D_MODEL)
    mem_kv_prompt = jnp.stack([_linear(mem2d, w_kv[l]) for l in range(DEPTH)])
    mem_kv_prompt = mem_kv_prompt.reshape(DEPTH, nb, n_mem, 2, X_HEADS, X_HEAD_DIM)
    y_p, a_p, b_p, bw_p, cw_p = _trunk_prompt(x_prompt, mem_kv_prompt, w)
    y_s, a_s, b_s, bw_s, cw_s = _trunk_sample(x_sample, cache_mem_kv, cache_a_kv, cache_b_kv, cache_b_win,
                                              cache_c_win, page_table, w)
    return (y_p, y_s, a_p, a_s, b_p, b_s, bw_p, bw_s, cw_p, cw_s, mem_kv_prompt)
```

```python
import functools
import math

import jax
import jax.numpy as jnp
import numpy as np
from jax import lax
from jax.experimental import pallas as pl
from jax.experimental.pallas import tpu as pltpu

D_MODEL = 1024
DEPTH = 4
PAGE_SIZE = 128
HEAD_DIM = 64
A_HEADS = 8
A_IDX_HEADS = 4
A_IDX_DIM = 64
A_TOPK = 256
B_HEADS = 8
B_KV_HEADS = 2
B_BLOCK = 64
B_TOPN = 16
B_WINDOW = 512
C_HEADS = 16
C_KV_HEADS = 2
C_WINDOW = 128
X_HEADS = 4
X_HEAD_DIM = 128
D_FF = 2816
N_BUCKETS = 32
BUCKET_MAX_DIST = 128
QBLOCK = 128
LN_EPS = 1e-5
ALPHA = (2 * DEPTH) ** 0.25
NEG = -1e30
FORCE_SCORE = 1e4
EVEN_SPLITS = [A_HEADS * HEAD_DIM, HEAD_DIM, HEAD_DIM, A_IDX_HEADS * A_IDX_DIM, A_IDX_DIM, A_IDX_HEADS,
               B_HEADS * HEAD_DIM] + [B_KV_HEADS * HEAD_DIM] * 6 + [B_HEADS * 3]
ODD_SPLITS = [C_HEADS * HEAD_DIM, C_KV_HEADS * HEAD_DIM, C_KV_HEADS * HEAD_DIM]
A_MIX = A_HEADS * HEAD_DIM
B_MIX = B_HEADS * HEAD_DIM
C_MIX = C_HEADS * HEAD_DIM
X_WIDTH = X_HEADS * X_HEAD_DIM

LANES = 128
VMEM_LIMIT = 56 << 20
BF16 = jnp.bfloat16
F32 = jnp.float32


def _params(*sem):
    return pltpu.CompilerParams(dimension_semantics=sem, vmem_limit_bytes=VMEM_LIMIT)


def _resident(shape):
    return pl.BlockSpec(shape, lambda *_: (0,) * len(shape), pipeline_mode=pl.Buffered(1))


def _per_batch(shape, col_block):
    return pl.BlockSpec(shape, lambda b, i: (b, 0, col_block), pipeline_mode=pl.Buffered(1))


def _row_tile(n):
    return min(n, 512)


def _layer_norm_rows(y, g, b):
    mu = jnp.mean(y, axis=-1, keepdims=True)
    d = y - mu
    var = jnp.mean(d * d, axis=-1, keepdims=True)
    return d * lax.rsqrt(var + LN_EPS) * g + b


FF_CHUNKS = 2
FF_CHUNK = D_FF // FF_CHUNKS


def _ffn_ln_kernel(x_ref, wg_ref, wu_ref, wd_ref, g_ref, b_ref, o_ref):
    x = x_ref[...]
    xb = x.astype(BF16)
    acc = jnp.zeros(x.shape, F32)
    for c in range(FF_CHUNKS):
        cols = slice(c * FF_CHUNK, (c + 1) * FF_CHUNK)
        gate = jnp.dot(xb, wg_ref[:, cols], preferred_element_type=F32)
        up = jnp.dot(xb, wu_ref[:, cols], preferred_element_type=F32)
        act = (gate * jax.nn.sigmoid(gate) * up).astype(BF16)
        acc = acc + jnp.dot(act, wd_ref[cols, :], preferred_element_type=F32)
    o_ref[...] = _layer_norm_rows(ALPHA * x + 0.5 * acc, g_ref[...], b_ref[...])


def _ffn_ln(x, wg, wu, wd, g, b):
    n = x.shape[0]
    tm = _row_tile(n)
    return pl.pallas_call(
        _ffn_ln_kernel,
        grid=(n // tm,),
        in_specs=[pl.BlockSpec((tm, D_MODEL), lambda i: (i, 0)),
                  _resident((D_MODEL, D_FF)), _resident((D_MODEL, D_FF)), _resident((D_FF, D_MODEL)),
                  _resident((1, D_MODEL)), _resident((1, D_MODEL))],
        out_specs=pl.BlockSpec((tm, D_MODEL), lambda i: (i, 0)),
        out_shape=jax.ShapeDtypeStruct((n, D_MODEL), F32),
        compiler_params=_params("parallel"),
        name="ffn_ln",
    )(x, wg, wu, wd, g.reshape(1, D_MODEL), b.reshape(1, D_MODEL))


def _linear_kernel(x_ref, w_ref, o_ref):
    o_ref[...] = jnp.dot(x_ref[...].astype(BF16), w_ref[...], preferred_element_type=F32)


def _linear(x, w):
    n, k = x.shape
    m = w.shape[1]
    tm = _row_tile(n)
    return pl.pallas_call(
        _linear_kernel,
        grid=(n // tm,),
        in_specs=[pl.BlockSpec((tm, k), lambda i: (i, 0)), _resident((k, m))],
        out_specs=pl.BlockSpec((tm, m), lambda i: (i, 0)),
        out_shape=jax.ShapeDtypeStruct((n, m), F32),
        compiler_params=_params("parallel"),
        name="linear",
    )(x, w)


def _linear_res_ln_kernel(x_ref, w_ref, h_ref, g_ref, b_ref, o_ref):
    y = jnp.dot(x_ref[...].astype(BF16), w_ref[...], preferred_element_type=F32)
    o_ref[...] = _layer_norm_rows(ALPHA * h_ref[...] + y, g_ref[...], b_ref[...])


def _linear_res_ln(x, w, h, g, b):
    n, k = x.shape
    tm = _row_tile(n)
    return pl.pallas_call(
        _linear_res_ln_kernel,
        grid=(n // tm,),
        in_specs=[pl.BlockSpec((tm, k), lambda i: (i, 0)), _resident((k, D_MODEL)),
                  pl.BlockSpec((tm, D_MODEL), lambda i: (i, 0)),
                  _resident((1, D_MODEL)), _resident((1, D_MODEL))],
        out_specs=pl.BlockSpec((tm, D_MODEL), lambda i: (i, 0)),
        out_shape=jax.ShapeDtypeStruct((n, D_MODEL), F32),
        compiler_params=_params("parallel"),
        name="linear_res_ln",
    )(x, w, h, g.reshape(1, D_MODEL), b.reshape(1, D_MODEL))


def t5_bucket(dist):
    exact = N_BUCKETS // 2
    d = jnp.maximum(dist, 0)
    rel = jnp.log(jnp.maximum(d, 1).astype(jnp.float32) / exact) / math.log(BUCKET_MAX_DIST / exact)
    large = jnp.minimum(exact + (rel * (N_BUCKETS - exact)).astype(jnp.int32), N_BUCKETS - 1)
    return jnp.where(d < exact, d, large)


TK = 128
INT_MIN = -2 ** 31
NT = (((1,), (1,)), ((), ()))


def _ordered_key(x):
    bits = pltpu.bitcast(x, jnp.int32)
    key = jnp.where(bits < 0, bits ^ jnp.int32(0x7FFFFFFF), bits)
    return jnp.where(x == 0.0, 0, key)


def _count(key_scr, nkt, rows, pred):
    def body(j, c):
        return c + jnp.where(pred(key_scr[j], j), 1.0, 0.0)
    c = lax.fori_loop(0, nkt, body, jnp.zeros((rows, TK), F32))
    return jnp.sum(c, axis=1, keepdims=True)


def _topk_threshold(key_scr, nkt, rows, k, idx_bits):
    lane = lax.broadcasted_iota(jnp.int32, (rows, TK), 1)

    def value_bit(it, ans):
        cand = ans + lax.shift_left(jnp.int32(1), 31 - it)
        cand_key = jnp.broadcast_to(cand, (rows, TK))
        cnt = _count(key_scr, nkt, rows, lambda t, j: t >= cand_key)
        return jnp.where(cnt >= k, cand, ans)
    tau = lax.fori_loop(0, 32, value_bit, jnp.full((rows, 1), INT_MIN, jnp.int32))
    n_ge = _count(key_scr, nkt, rows, lambda t, j: t >= tau)

    def cut_among_ties():
        need = k - _count(key_scr, nkt, rows, lambda t, j: t > tau)

        def index_bit(it, cut):
            cand = cut | lax.shift_left(jnp.int32(1), idx_bits - 1 - it)
            cnt = _count(key_scr, nkt, rows,
                         lambda t, j: jnp.where(t == tau, j * TK + lane, jnp.int32(2 ** 30)) < cand)
            return jnp.where(cnt < need, cand, cut)
        return lax.fori_loop(0, idx_bits, index_bit, jnp.zeros((rows, 1), jnp.int32))

    cut = lax.cond(jnp.max(n_ge) > k, cut_among_ties, lambda: jnp.full((rows, 1), 2 ** 30, jnp.int32))
    return tau, cut


def _in_topk(t, j, tau, cut):
    lane = lax.broadcasted_iota(jnp.int32, t.shape, 1)
    return (t > tau) | ((t == tau) & (j * TK + lane <= cut))


def _heads_to_rows(x, n_heads):
    return jnp.concatenate([x[:, h * HEAD_DIM:(h + 1) * HEAD_DIM] for h in range(n_heads)], axis=0)


FAR_TILES = 4


def _flash_init(m_scr, acc_scr):
    m_scr[...] = jnp.full(m_scr.shape, NEG, F32)
    acc_scr[...] = jnp.zeros(acc_scr.shape, F32)


def _with_ones(v):
    return jnp.concatenate([v, jnp.ones(v.shape, BF16)], axis=1)


def _flash_step(m_scr, acc_scr, heads, q_rows, k, v1, madd, bias):
    n_h = q_rows.shape[0] // TK
    lg = lax.dot_general(q_rows, k, NT, preferred_element_type=F32).reshape(n_h, TK, k.shape[0]) + madd[None]
    if bias is not None:
        lg = lg + bias
    lg = lg.reshape(n_h * TK, k.shape[0])
    m_old = m_scr[heads].reshape(n_h * TK, 1)
    m_new = jnp.maximum(m_old, jnp.max(lg, axis=1, keepdims=True))
    p = jnp.exp(lg - m_new).astype(BF16)
    acc = jnp.exp(m_old - m_new) * acc_scr[heads].reshape(n_h * TK, 2 * HEAD_DIM)
    acc_scr[heads] = (acc + jnp.dot(p, v1, preferred_element_type=F32)).reshape(n_h, TK, 2 * HEAD_DIM)
    m_scr[heads] = m_new.reshape(n_h, TK, 1)


def _flash_far_loop(n_far, attend):
    n_big = n_far // FAR_TILES

    def big(jb, carry):
        attend(jb * FAR_TILES, FAR_TILES)
        return carry
    lax.fori_loop(0, n_big, big, 0)

    def small(j, carry):
        attend(j, 1)
        return carry
    lax.fori_loop(n_big * FAR_TILES, n_far, small, 0)


def _flash_result(acc_scr, h):
    acc = acc_scr[h]
    return acc[:, 0:HEAD_DIM] / acc[:, HEAD_DIM:2 * HEAD_DIM]


def _window_attend(qg, n_r, qr, tiles, band_ref, head0, sink=None):
    row = lax.broadcasted_iota(jnp.int32, (qr, TK), 0)
    lane = lax.broadcasted_iota(jnp.int32, (qr, TK), 1)
    heads = slice(head0, head0 + n_r)
    pieces = []
    for d, (k, _, exists) in enumerate(tiles):
        lg = lax.dot_general(qg, k, NT, preferred_element_type=F32).reshape(n_r, qr, TK)
        ok = jnp.broadcast_to(exists, (qr, TK))
        if d == 0:
            ok = ok & (lane <= row)
            lg = lg + band_ref[heads, 0:qr, TK:2 * TK]
        if d == 1:
            lg = lg + band_ref[heads, 0:qr, 0:TK]
        if d == len(tiles) - 1:
            ok = ok & (lane > row)
        pieces.append(jnp.where(ok[None], lg, NEG).reshape(n_r * qr, TK))
    m = functools.reduce(jnp.maximum, [jnp.max(x, axis=1, keepdims=True) for x in pieces])
    if sink is not None:
        m = jnp.maximum(m, sink)
    ps = [jnp.exp(x - m) for x in pieces]
    den = functools.reduce(lambda a, b: a + b, [jnp.sum(p, axis=1, keepdims=True) for p in ps])
    if sink is not None:
        den = den + jnp.exp(sink - m)
    o = functools.reduce(lambda a, b: a + b,
                         [jnp.dot(p.astype(BF16), t[1], preferred_element_type=F32) for p, t in zip(ps, tiles)])
    return o / den


def _distance_bias(rel_table):
    tb = rel_table[t5_bucket(jnp.arange(2 * TK))]
    return tb - rel_table[N_BUCKETS - 1][None, :]


def _band(dist_bias):
    a = np.arange(TK)[:, None]
    c = np.arange(2 * TK)[None, :]
    idx = np.clip(a - c + TK, 0, 2 * TK - 1)
    return jnp.transpose(dist_bias[idx], (2, 0, 1))


def _dsa_prompt_kernel(q_ref, qi_ref, misc_ref, kv_ref, band_ref, o_ref, key_scr, m_scr, acc_scr):
    i = pl.program_id(1)
    nkt = i + 1
    row = lax.broadcasted_iota(jnp.int32, (TK, TK), 0)
    lane = lax.broadcasted_iota(jnp.int32, (TK, TK), 1)
    qi = qi_ref[0].astype(BF16)
    wi = [jnp.broadcast_to(misc_ref[0][:, MISC_WI + h:MISC_WI + h + 1], (TK, TK)) for h in range(A_IDX_HEADS)]

    def score_tiles(j0, n, causal=False):
        rows = pl.ds(pl.multiple_of(j0 * TK, TK), n * TK)
        kt = kv_ref[0, rows, 2 * HEAD_DIM:3 * HEAD_DIM].astype(BF16)
        sc = jnp.zeros((TK, n * TK), F32)
        for h in range(A_IDX_HEADS):
            s = lax.dot_general(qi[:, h * A_IDX_DIM:(h + 1) * A_IDX_DIM], kt, NT, preferred_element_type=F32)
            sc = sc + jnp.concatenate([wi[h]] * n, axis=1) * jnp.maximum(s * A_IDX_DIM ** -0.5, 0.0)
        sc = sc * A_IDX_HEADS ** -0.5
        if causal:
            sc = jnp.where(lane <= row, sc, NEG)
        for t in range(n):
            key_scr[j0 + t] = _ordered_key(sc[:, t * TK:(t + 1) * TK])
    _flash_far_loop(i, score_tiles)
    score_tiles(i, 1, True)
    tau, cut = _topk_threshold(key_scr, nkt, TK, float(A_TOPK), DSA_IDX_BITS)

    q = (q_ref[0] * HEAD_DIM ** -0.5).astype(BF16)
    qh = _heads_to_rows(q, A_HEADS)
    _flash_init(m_scr, acc_scr)

    def attend(j0, n, bias_cols=None, causal=False):
        rows = pl.ds(pl.multiple_of(j0 * TK, TK), n * TK)
        k = kv_ref[0, rows, 0:HEAD_DIM].astype(BF16)
        v1 = _with_ones(kv_ref[0, rows, HEAD_DIM:2 * HEAD_DIM].astype(BF16))
        sels = [_in_topk(key_scr[j0 + t], j0 + t, tau, cut) for t in range(n)]
        if causal:
            sels = [s & (lane <= row) for s in sels]
        madd = jnp.concatenate([jnp.where(s, 0.0, NEG) for s in sels], axis=1)
        bias = None if bias_cols is None else band_ref[:, :, bias_cols]
        _flash_step(m_scr, acc_scr, slice(0, A_HEADS), qh, k, v1, madd, bias)

    _flash_far_loop(jnp.maximum(i - 1, 0), attend)

    @pl.when(i >= 1)
    def _():
        attend(i - 1, 1, slice(0, TK))
    attend(i, 1, slice(TK, 2 * TK), True)
    for h in range(A_HEADS):
        o_ref[0, h] = _flash_result(acc_scr, h)


def _dsa_prompt(proj, band_a):
    bn, t, _ = proj.shape
    nqb = t // TK
    return pl.pallas_call(
        _dsa_prompt_kernel,
        grid=(bn, nqb),
        in_specs=[pl.BlockSpec((1, TK, A_MIX), lambda b, i: (b, i, COL_QA // A_MIX)),
                  pl.BlockSpec((1, TK, 256), lambda b, i: (b, i, COL_QI // 256)),
                  pl.BlockSpec((1, TK, 256), lambda b, i: (b, i, COL_A // 256)),
                  _per_batch((1, t, 256), COL_A // 256),
                  _resident((A_HEADS, TK, 2 * TK))],
        out_specs=pl.BlockSpec((1, A_HEADS, TK, HEAD_DIM), lambda b, i: (b, 0, i, 0)),
        out_shape=jax.ShapeDtypeStruct((bn, A_HEADS, t, HEAD_DIM), F32),
        scratch_shapes=[pltpu.VMEM((nqb, TK, TK), jnp.int32),
                        pltpu.VMEM((A_HEADS, TK, 1), F32), pltpu.VMEM((A_HEADS, TK, 2 * HEAD_DIM), F32)],
        compiler_params=_params("parallel", "arbitrary"),
        name="dsa_prompt",
    )(proj, proj, proj, proj, band_a)


COL_QA, COL_QB, COL_QI, COL_A = 0, 512, 1024, 1280
COL_B, COL_WIN, EVEN_COLS = 1536, 2048, 2304
MISC_WI = 3 * HEAD_DIM
MISC_GL = MISC_WI + A_IDX_HEADS
DSA_IDX_BITS = 14


def _even_in_perm():
    offs = np.concatenate([[0], np.cumsum(EVEN_SPLITS)])
    seg = lambda k: np.arange(offs[k], offs[k + 1])
    (q_a, k_a, v_a, qi, ki, wi, q_b, kc, vc, ksl, vsl, kw, vw, gl) = [seg(k) for k in range(14)]
    order = [q_a, q_b, qi, k_a, v_a, ki, wi, gl]
    pad = EVEN_COLS - sum(EVEN_SPLITS)
    return np.concatenate(order), pad, np.concatenate([kc, vc, ksl, vsl, kw, vw])


def _permute_even_w_in(w):
    head, pad, tail = _even_in_perm()
    zeros = jnp.zeros(w.shape[:-1] + (pad,), w.dtype)
    return jnp.concatenate([w[..., head], zeros, w[..., tail]], axis=-1)


G_COLS = B_KV_HEADS * HEAD_DIM


def _compress_rows(x, pos_w, w1, w2):
    nb = x.shape[0] // B_BLOCK
    pooled = jnp.sum(x.reshape(nb, B_BLOCK, G_COLS) * pos_w[None], axis=1)
    hid = jax.nn.gelu(jnp.dot(pooled.astype(BF16), w1, preferred_element_type=F32))
    return jnp.dot(hid.astype(BF16), w2, preferred_element_type=F32)


def _compress_kernel(kv_ref, pos_ref, w1_ref, w2_ref, o_ref):
    for s in range(2):
        x = kv_ref[0, :, s * G_COLS:(s + 1) * G_COLS]
        o_ref[0, :, s * G_COLS:(s + 1) * G_COLS] = _compress_rows(x, pos_ref[s], w1_ref[s], w2_ref[s])


def _compress_weights(cmp_pos, cmp_w1, cmp_w2):
    pos = jnp.concatenate([cmp_pos] * B_KV_HEADS, axis=-1)
    eye = jnp.eye(B_KV_HEADS, dtype=cmp_w1.dtype)
    bd = lambda w: jnp.einsum('gh,sde->sgdhe', eye, w).reshape(2, G_COLS, G_COLS).astype(BF16)
    return pos, bd(cmp_w1), bd(cmp_w2)


def _compress_prompt(proj, pos, w1, w2):
    bn, t, _ = proj.shape
    tr = min(t, 2048)
    return pl.pallas_call(
        _compress_kernel,
        grid=(bn, t // tr),
        in_specs=[pl.BlockSpec((1, tr, 2 * G_COLS), lambda b, i: (b, i, COL_B // (2 * G_COLS))),
                  _resident((2, B_BLOCK, G_COLS)), _resident((2, G_COLS, G_COLS)), _resident((2, G_COLS, G_COLS))],
        out_specs=pl.BlockSpec((1, tr // B_BLOCK, 2 * G_COLS), lambda b, i: (b, i, 0)),
        out_shape=jax.ShapeDtypeStruct((bn, t // B_BLOCK, 2 * G_COLS), F32),
        compiler_params=_params("parallel", "parallel"),
        name="nsa_compress",
    )(proj, pos, w1, w2)


B_R = B_HEADS // B_KV_HEADS
NSA_WIN_TILES = B_WINDOW // TK


def _softmax_pieces(pieces):
    m = functools.reduce(jnp.maximum, [jnp.max(x, axis=1, keepdims=True) for x in pieces])
    ps = [jnp.exp(x - m) for x in pieces]
    den = functools.reduce(lambda a, b: a + b, [jnp.sum(p, axis=1, keepdims=True) for p in ps])
    return ps, den, m


def _select_blocks(imp, n_sel, lanef=None):
    if lanef is None:
        lanef = lax.broadcasted_iota(jnp.int32, imp.shape, 1).astype(F32)
    sel = jnp.zeros(imp.shape, F32)
    for _ in range(n_sel):
        mx = jnp.max(imp, axis=1, keepdims=True)
        first = jnp.min(jnp.where(imp == mx, lanef, float(2 ** 24)), axis=1, keepdims=True)
        hit = lanef == first
        sel = jnp.where(hit, 1.0, sel)
        imp = jnp.where(hit, -jnp.inf, imp)
    return sel


def _nsa_prompt_kernel(q_ref, misc_ref, kv_ref, win_ref, cmp_ref, band_ref, cband_ref, o_ref,
                       selm_scr, m_scr, acc_scr):
    i = pl.program_id(1)
    n_blk = cmp_ref.shape[1]
    row = lax.broadcasted_iota(jnp.int32, (TK, TK), 0)
    lane = lax.broadcasted_iota(jnp.int32, (TK, TK), 1)
    tpos = i * TK + row
    q = (q_ref[0] * HEAD_DIM ** -0.5).astype(BF16)
    gates = jax.nn.sigmoid(misc_ref[0][:, MISC_GL:MISC_GL + 3 * B_HEADS])
    qg = [_heads_to_rows(q[:, g * B_R * HEAD_DIM:(g + 1) * B_R * HEAD_DIM], B_R) for g in range(B_KV_HEADS)]

    blk = lax.broadcasted_iota(jnp.int32, (TK, n_blk), 1)
    brow = lax.broadcasted_iota(jnp.int32, (TK, n_blk), 0)
    bpos = i * TK + brow
    valid_c = (blk + 1) * B_BLOCK - 1 <= bpos
    cur = bpos // B_BLOCK
    forced = jnp.where((blk == 0) | (blk == cur) | (blk == cur - 1), FORCE_SCORE, 0.0)
    o_cmp = []
    for g in range(B_KV_HEADS):
        kc = cmp_ref[0, :, g * HEAD_DIM:(g + 1) * HEAD_DIM].astype(BF16)
        vc = cmp_ref[0, :, G_COLS + g * HEAD_DIM:G_COLS + (g + 1) * HEAD_DIM].astype(BF16)
        lc = lax.dot_general(qg[g], kc, NT, preferred_element_type=F32).reshape(B_R, TK, n_blk)
        bias = []
        for r in range(B_R):
            cb = cband_ref[g * B_R + r]
            b = jnp.zeros((TK, n_blk), F32)
            for u in range(4):
                b = jnp.where(blk == 2 * i + u - 2, cb[:, u:u + 1], b)
            bias.append(b)
        lc = jnp.where(valid_c[None], lc + jnp.stack(bias), NEG).reshape(B_R * TK, n_blk)
        (p,), den, _ = _softmax_pieces([lc])
        pc = ((p / den).reshape(B_R, TK, n_blk) * jnp.where(valid_c, 1.0, 0.0)[None])
        o_cmp.append(jnp.dot(pc.reshape(B_R * TK, n_blk).astype(BF16), vc, preferred_element_type=F32))
        imp = jnp.sum(pc, axis=0)
        imp = jnp.where(blk <= cur, imp + forced, NEG)
        selm_scr[g] = _select_blocks(imp, min(B_TOPN, n_blk)).astype(BF16)

    _flash_init(m_scr, acc_scr)

    def attend(j0, n, bias_cols=None, causal=False):
        rows = pl.ds(pl.multiple_of(j0 * TK, TK), n * TK)
        ebk = lax.broadcasted_iota(jnp.int32, (n_blk, n * TK), 0)
        etok = j0 * TK + lax.broadcasted_iota(jnp.int32, (n_blk, n * TK), 1)
        expand = jnp.where(ebk == etok // B_BLOCK, 1.0, 0.0).astype(BF16)
        for g in range(B_KV_HEADS):
            k = kv_ref[0, rows, g * HEAD_DIM:(g + 1) * HEAD_DIM].astype(BF16)
            v1 = _with_ones(kv_ref[0, rows, G_COLS + g * HEAD_DIM:G_COLS + (g + 1) * HEAD_DIM].astype(BF16))
            sel = jnp.dot(selm_scr[g], expand, preferred_element_type=F32) > 0.5
            if causal:
                sel = sel & (lane <= row)
            madd = jnp.where(sel, 0.0, NEG)
            heads = slice(g * B_R, (g + 1) * B_R)
            bias = None if bias_cols is None else band_ref[heads, :, bias_cols]
            _flash_step(m_scr, acc_scr, heads, qg[g], k, v1, madd, bias)

    _flash_far_loop(jnp.maximum(i - 1, 0), attend)

    @pl.when(i >= 1)
    def _():
        attend(i - 1, 1, slice(0, TK))
    attend(i, 1, slice(TK, 2 * TK), True)

    for g in range(B_KV_HEADS):
        tiles = []
        for d in range(NSA_WIN_TILES + 1):
            rows = pl.ds(pl.multiple_of(jnp.maximum(i - d, 0) * TK, TK), TK)
            tiles.append((win_ref[0, rows, g * HEAD_DIM:(g + 1) * HEAD_DIM].astype(BF16),
                          win_ref[0, rows, G_COLS + g * HEAD_DIM:G_COLS + (g + 1) * HEAD_DIM].astype(BF16),
                          i - d >= 0))
        o_win = _window_attend(qg[g], B_R, TK, tiles, band_ref, g * B_R)
        for r in range(B_R):
            h = g * B_R + r
            rr = slice(r * TK, (r + 1) * TK)
            o_ref[0, h] = (gates[:, 3 * h:3 * h + 1] * o_cmp[g][rr]
                           + gates[:, 3 * h + 1:3 * h + 2] * _flash_result(acc_scr, h)
                           + gates[:, 3 * h + 2:3 * h + 3] * o_win[rr])


def _cmp_band(dist_bias):
    a = np.arange(TK)[:, None]
    u = np.arange(4)[None, :] - 2
    idx = np.clip(a - B_BLOCK * u - (B_BLOCK - 1), 0, 2 * TK - 1)
    return jnp.transpose(dist_bias[idx], (2, 0, 1))


def _nsa_prompt(proj, cmp_kv, band_b, cband_b):
    bn, t, _ = proj.shape
    nqb = t // TK
    n_blk = cmp_kv.shape[1]
    return pl.pallas_call(
        _nsa_prompt_kernel,
        grid=(bn, nqb),
        in_specs=[pl.BlockSpec((1, TK, B_MIX), lambda b, i: (b, i, COL_QB // B_MIX)),
                  pl.BlockSpec((1, TK, 256), lambda b, i: (b, i, COL_A // 256)),
                  _per_batch((1, t, 2 * G_COLS), (COL_B + 2 * G_COLS) // (2 * G_COLS)),
                  _per_batch((1, t, 2 * G_COLS), COL_WIN // (2 * G_COLS)),
                  _per_batch((1, n_blk, 2 * G_COLS), 0),
                  _resident((B_HEADS, TK, 2 * TK)), _resident((B_HEADS, TK, 4))],
        out_specs=pl.BlockSpec((1, B_HEADS, TK, HEAD_DIM), lambda b, i: (b, 0, i, 0)),
        out_shape=jax.ShapeDtypeStruct((bn, B_HEADS, t, HEAD_DIM), F32),
        scratch_shapes=[pltpu.VMEM((B_KV_HEADS, TK, n_blk), BF16),
                        pltpu.VMEM((B_HEADS, TK, 1), F32), pltpu.VMEM((B_HEADS, TK, 2 * HEAD_DIM), F32)],
        compiler_params=_params("parallel", "arbitrary"),
        name="nsa_prompt",
    )(proj, proj, proj, proj, cmp_kv, band_b, cband_b)


C_R = C_HEADS // C_KV_HEADS
COL_QC, COL_CWIN = 0, C_MIX


def _swa_prompt_kernel(q_ref, kv_ref, band_ref, sink_ref, o_ref):
    i = pl.program_id(1)
    row = lax.broadcasted_iota(jnp.int32, (TK, TK), 0)
    lane = lax.broadcasted_iota(jnp.int32, (TK, TK), 1)
    q = (q_ref[0] * HEAD_DIM ** -0.5).astype(BF16)
    prev_rows = pl.ds(pl.multiple_of(jnp.maximum(i - 1, 0) * TK, TK), TK)
    own_rows = pl.ds(pl.multiple_of(i * TK, TK), TK)
    prev_ok = jnp.broadcast_to(i >= 1, (TK, TK)) & (lane > row)
    for g in range(C_KV_HEADS):
        qg = _heads_to_rows(q[:, g * C_R * HEAD_DIM:(g + 1) * C_R * HEAD_DIM], C_R)
        pieces, vals = [], []
        for rows, ok, cols in ((prev_rows, prev_ok, slice(0, TK)), (own_rows, lane <= row, slice(TK, 2 * TK))):
            k = kv_ref[0, rows, g * HEAD_DIM:(g + 1) * HEAD_DIM].astype(BF16)
            vals.append(kv_ref[0, rows, G_COLS + g * HEAD_DIM:G_COLS + (g + 1) * HEAD_DIM].astype(BF16))
            lg = lax.dot_general(qg, k, NT, preferred_element_type=F32).reshape(C_R, TK, TK)
            lg = lg + band_ref[g * C_R:(g + 1) * C_R, :, cols]
            pieces.append(jnp.where(ok[None], lg, NEG).reshape(C_R * TK, TK))
        sink = jnp.concatenate([jnp.broadcast_to(sink_ref[g * C_R + r], (TK, 1)) for r in range(C_R)], axis=0)
        m = jnp.maximum(jnp.maximum(jnp.max(pieces[0], axis=1, keepdims=True),
                                    jnp.max(pieces[1], axis=1, keepdims=True)), sink)
        ps = [jnp.exp(x - m) for x in pieces]
        den = jnp.sum(ps[0], axis=1, keepdims=True) + jnp.sum(ps[1], axis=1, keepdims=True) + jnp.exp(sink - m)
        o = (jnp.dot(ps[0].astype(BF16), vals[0], preferred_element_type=F32)
             + jnp.dot(ps[1].astype(BF16), vals[1], preferred_element_type=F32)) / den
        o_ref[0, g * C_R:(g + 1) * C_R] = o.reshape(C_R, TK, HEAD_DIM)


def _swa_prompt(proj, band_c, sinks):
    bn, t, _ = proj.shape
    return pl.pallas_call(
        _swa_prompt_kernel,
        grid=(bn, t // TK),
        in_specs=[pl.BlockSpec((1, TK, C_MIX), lambda b, i: (b, i, COL_QC // C_MIX)),
                  _per_batch((1, t, 2 * G_COLS), COL_CWIN // (2 * G_COLS)),
                  _resident((C_HEADS, TK, 2 * TK)), _resident((C_HEADS, 1, 1))],
        out_specs=pl.BlockSpec((1, C_HEADS, TK, HEAD_DIM), lambda b, i: (b, 0, i, 0)),
        out_shape=jax.ShapeDtypeStruct((bn, C_HEADS, t, HEAD_DIM), F32),
        compiler_params=_params("parallel", "arbitrary"),
        name="swa_prompt",
    )(proj, proj, band_c, sinks)


def _cross_attn_kernel(q_ref, mem_ref, o_ref):
    q = (q_ref[0] * X_HEAD_DIM ** -0.5).astype(BF16)
    for h in range(X_HEADS):
        cols = slice(h * X_HEAD_DIM, (h + 1) * X_HEAD_DIM)
        k = mem_ref[0, :, cols].astype(BF16)
        v = mem_ref[0, :, X_WIDTH + h * X_HEAD_DIM:X_WIDTH + (h + 1) * X_HEAD_DIM].astype(BF16)
        s = lax.dot_general(q[:, cols], k, NT, preferred_element_type=F32)
        p = jnp.exp(s - jnp.max(s, axis=1, keepdims=True))
        den = jnp.sum(p, axis=1, keepdims=True)
        o_ref[0, :, cols] = jnp.dot(p.astype(BF16), v, preferred_element_type=F32) / den


def _cross_attn(q, mem):
    bn, t, _ = q.shape
    n_mem = mem.shape[1]
    tq = min(t, 512)
    return pl.pallas_call(
        _cross_attn_kernel,
        grid=(bn, t // tq),
        in_specs=[pl.BlockSpec((1, tq, X_WIDTH), lambda b, i: (b, i, 0)),
                  pl.BlockSpec((1, n_mem, 2 * X_WIDTH), lambda b, i: (b, 0, 0))],
        out_specs=pl.BlockSpec((1, tq, X_WIDTH), lambda b, i: (b, i, 0)),
        out_shape=jax.ShapeDtypeStruct((bn, t, X_WIDTH), F32),
        compiler_params=_params("parallel", "arbitrary"),
        name="cross_attn",
    )(q, mem)


def _heads_out_ln_kernel(*refs):
    o_refs, (w_ref, h_ref, g_ref, b_ref, out_ref) = refs[:-5], refs[-5:]
    y = jnp.zeros(h_ref.shape[1:], F32)
    hh = 0
    for o_ref in o_refs:
        for h in range(o_ref.shape[1]):
            y = y + jnp.dot(o_ref[0, h].astype(BF16), w_ref[hh], preferred_element_type=F32)
            hh += 1
    out_ref[0] = _layer_norm_rows(ALPHA * h_ref[0] + y, g_ref[...], b_ref[...])


def _heads_out_ln(outs, w, h, g, b):
    bn, t, _ = h.shape
    tm = min(t, 512)
    return pl.pallas_call(
        _heads_out_ln_kernel,
        grid=(bn, t // tm),
        in_specs=[pl.BlockSpec((1, o.shape[1], tm, HEAD_DIM), lambda b, i: (b, 0, i, 0)) for o in outs]
        + [_resident(w.shape), pl.BlockSpec((1, tm, D_MODEL), lambda b, i: (b, i, 0)),
           _resident((1, D_MODEL)), _resident((1, D_MODEL))],
        out_specs=pl.BlockSpec((1, tm, D_MODEL), lambda b, i: (b, i, 0)),
        out_shape=jax.ShapeDtypeStruct((bn, t, D_MODEL), F32),
        compiler_params=_params("parallel", "parallel"),
        name="heads_out_ln",
    )(*outs, w, h, g.reshape(1, D_MODEL), b.reshape(1, D_MODEL))


QS = 8


def _own_tile(new_rows):
    return jnp.concatenate([new_rows, jnp.zeros((TK - new_rows.shape[0], new_rows.shape[1]), F32)], axis=0)


def _topk_threshold_dense(keys, pos, k, idx_bits):
    count = lambda pred: jnp.sum(jnp.where(pred, 1.0, 0.0), axis=1, keepdims=True)

    def value_bit(it, ans):
        cand = ans + lax.shift_left(jnp.int32(1), 31 - it)
        return jnp.where(count(keys >= cand) >= k, cand, ans)
    tau = lax.fori_loop(0, 32, value_bit, jnp.full((keys.shape[0], 1), INT_MIN, jnp.int32))
    need = k - count(keys > tau)
    tie_pos = jnp.where(keys == tau, pos, jnp.int32(2 ** 30))

    def index_bit(it, cut):
        cand = cut | lax.shift_left(jnp.int32(1), idx_bits - 1 - it)
        return jnp.where(count(tie_pos < cand) < need, cand, cut)
    cut = lax.fori_loop(0, idx_bits, index_bit, jnp.zeros((keys.shape[0], 1), jnp.int32))
    return tau, cut


def _near_far_softmax(lg, ok, band, n_h, v_ref):
    nk = lg.shape[1]
    lg = lg.reshape(n_h, QS, nk)
    far = jnp.where(ok[None, :, :nk - 2 * TK], lg[:, :, :nk - 2 * TK], NEG).reshape(n_h * QS, nk - 2 * TK)
    near = jnp.where(ok[None, :, nk - 2 * TK:], lg[:, :, nk - 2 * TK:] + band, NEG).reshape(n_h * QS, 2 * TK)
    (p_far, p_near), den, _ = _softmax_pieces([far, near])
    o = (jnp.dot(p_far.astype(BF16), v_ref[0:nk - 2 * TK, :], preferred_element_type=F32)
         + jnp.dot(p_near.astype(BF16), v_ref[nk - 2 * TK:nk, :], preferred_element_type=F32))
    return o / den


def _pages_per_step(n_pages):
    return next(c for c in (8, 4, 2, 1) if n_pages % c == 0)


def _page_specs(block, layer, n_pages, pps):
    return [pl.BlockSpec(block, lambda b, p, tbl, k=k: (layer, tbl[b * n_pages + p * pps + k], 0, 0))
            for k in range(pps)]


def _dsa_sample_kernel(tbl_ref, q_ref, *refs):
    pps = len(refs) - 5
    page_refs, (band_ref, o_ref, k_scr, v_scr, ki_scr) = refs[:pps], refs[pps:]
    p = pl.program_id(1)
    for c, page_ref in enumerate(page_refs):
        rows = pl.ds(pl.multiple_of((p * pps + c) * TK, TK), TK)
        page = page_ref[0, 0]
        k_scr[rows, :] = page[:, 0:HEAD_DIM].astype(BF16)
        v_scr[rows, :] = page[:, HEAD_DIM:2 * HEAD_DIM].astype(BF16)
        ki_scr[rows, :] = page[:, 2 * HEAD_DIM:3 * HEAD_DIM].astype(BF16)

    @pl.when(p == pl.num_programs(1) - 1)
    def _():
        nk = k_scr.shape[0]
        past = nk - TK
        x = q_ref[0]
        own = _own_tile(x[:, COL_A:COL_A + 3 * HEAD_DIM]).astype(BF16)
        k_scr[past:nk, :] = own[:, 0:HEAD_DIM]
        v_scr[past:nk, :] = own[:, HEAD_DIM:2 * HEAD_DIM]
        ki_scr[past:nk, :] = own[:, 2 * HEAD_DIM:3 * HEAD_DIM]
        pos = lax.broadcasted_iota(jnp.int32, (QS, nk), 1)
        causal = pos <= past + lax.broadcasted_iota(jnp.int32, (QS, nk), 0)
        qi = _heads_to_rows(x[:, COL_QI:COL_QI + A_IDX_HEADS * A_IDX_DIM].astype(BF16), A_IDX_HEADS)
        wi = x[:, COL_A + MISC_WI:COL_A + MISC_WI + A_IDX_HEADS]
        s = lax.dot_general(qi, ki_scr[...], NT, preferred_element_type=F32).reshape(A_IDX_HEADS, QS, nk)
        sc = jnp.zeros((QS, nk), F32)
        for h in range(A_IDX_HEADS):
            sc = sc + wi[:, h:h + 1] * jnp.maximum(s[h] * A_IDX_DIM ** -0.5, 0.0)
        keys = _ordered_key(jnp.where(causal, sc * A_IDX_HEADS ** -0.5, NEG))
        tau, cut = _topk_threshold_dense(keys, pos, float(A_TOPK), DSA_IDX_BITS)
        sel = ((keys > tau) | ((keys == tau) & (pos <= cut))) & causal
        qh = _heads_to_rows((x[:, COL_QA:COL_QA + A_MIX] * HEAD_DIM ** -0.5).astype(BF16), A_HEADS)
        lg = lax.dot_general(qh, k_scr[...], NT, preferred_element_type=F32)
        o = _near_far_softmax(lg, sel, band_ref[:, 0:QS, :], A_HEADS, v_scr)
        o_ref[0] = o.reshape(A_HEADS, QS, HEAD_DIM)


def _dsa_sample(proj, pool, page_table, layer, band_a):
    ns = proj.shape[0]
    n_pages = page_table.shape[1]
    nk = (n_pages + 1) * TK
    pps = _pages_per_step(n_pages)
    grid_spec = pltpu.PrefetchScalarGridSpec(
        num_scalar_prefetch=1, grid=(ns, n_pages // pps),
        in_specs=[pl.BlockSpec((1, QS, EVEN_COLS), lambda b, p, tbl: (b, 0, 0))]
        + _page_specs((1, 1, PAGE_SIZE, 3 * HEAD_DIM), layer, n_pages, pps)
        + [pl.BlockSpec((A_HEADS, TK, 2 * TK), lambda b, p, tbl: (0, 0, 0))],
        out_specs=pl.BlockSpec((1, A_HEADS, QS, HEAD_DIM), lambda b, p, tbl: (b, 0, 0, 0)),
        scratch_shapes=[pltpu.VMEM((nk, HEAD_DIM), BF16)] * 3)
    return pl.pallas_call(
        _dsa_sample_kernel, grid_spec=grid_spec,
        out_shape=jax.ShapeDtypeStruct((ns, A_HEADS, QS, HEAD_DIM), F32),
        compiler_params=_params("parallel", "arbitrary"),
        name="dsa_sample",
    )(page_table.reshape(-1), proj, *([pool] * pps), band_a)


def _nsa_sample_kernel(tbl_ref, q_ref, *refs):
    pps = len(refs) - 12
    page_refs = refs[:pps]
    (win_ref, pos_ref, w1_ref, w2_ref, band_ref, cband_ref, expand_ref,
     o_ref, ks_scr, vs_scr, pool_even, pool_odd) = refs[pps:]
    p = pl.program_id(1)
    pos_w = jnp.concatenate([pos_ref[0], pos_ref[1]], axis=1)
    for c, page_ref in enumerate(page_refs):
        lp = p * pps + c
        rows = pl.ds(pl.multiple_of(lp * TK, TK), TK)
        page = page_ref[0, 0]
        for g in range(B_KV_HEADS):
            ks_scr[g, rows, :] = page[:, 2 * G_COLS + g * HEAD_DIM:2 * G_COLS + (g + 1) * HEAD_DIM].astype(BF16)
            vs_scr[g, rows, :] = page[:, 3 * G_COLS + g * HEAD_DIM:3 * G_COLS + (g + 1) * HEAD_DIM].astype(BF16)
        pool_even[pl.ds(lp, 1), :] = jnp.sum(page[0:B_BLOCK, 0:2 * G_COLS] * pos_w, axis=0, keepdims=True)
        pool_odd[pl.ds(lp, 1), :] = jnp.sum(page[B_BLOCK:2 * B_BLOCK, 0:2 * G_COLS] * pos_w, axis=0, keepdims=True)

    @pl.when(p == pl.num_programs(1) - 1)
    def _():
        nk = ks_scr.shape[1]
        past = nk - TK
        n_blk = pool_even.shape[0] * 2
        nb_w = expand_ref.shape[0]
        x = q_ref[0]
        own = _own_tile(x[:, COL_B + 2 * G_COLS:COL_B + 4 * G_COLS]).astype(BF16)
        for g in range(B_KV_HEADS):
            ks_scr[g, past:nk, :] = own[:, g * HEAD_DIM:(g + 1) * HEAD_DIM]
            vs_scr[g, past:nk, :] = own[:, G_COLS + g * HEAD_DIM:G_COLS + (g + 1) * HEAD_DIM]
        q = (x[:, COL_QB:COL_QB + B_MIX] * HEAD_DIM ** -0.5).astype(BF16)
        gates = jax.nn.sigmoid(x[:, COL_A + MISC_GL:COL_A + MISC_GL + 3 * B_HEADS])
        qg = [_heads_to_rows(q[:, g * B_R * HEAD_DIM:(g + 1) * B_R * HEAD_DIM], B_R) for g in range(B_KV_HEADS)]
        pooled = jnp.concatenate([pool_even[...], pool_odd[...]], axis=0)
        cmp_kv = []
        for s in range(2):
            hid = jax.nn.gelu(jnp.dot(pooled[:, s * G_COLS:(s + 1) * G_COLS].astype(BF16), w1_ref[s],
                                      preferred_element_type=F32))
            cmp_kv.append(jnp.dot(hid.astype(BF16), w2_ref[s], preferred_element_type=F32).astype(BF16))
        lane = lax.broadcasted_iota(jnp.int32, (QS, nb_w), 1)
        half = n_blk // 2
        blk = jnp.where(lane < half, 2 * lane, jnp.where(lane < n_blk, 2 * (lane - half) + 1,
                                                         jnp.where(lane == n_blk, n_blk, 2 ** 20)))
        tpos = past + lax.broadcasted_iota(jnp.int32, (QS, nb_w), 0)
        cur = tpos // B_BLOCK
        valid_c = ((blk + 1) * B_BLOCK - 1 <= tpos)[:, :n_blk]
        forced = jnp.where((blk == 0) | (blk == cur) | (blk == cur - 1), FORCE_SCORE, 0.0)
        i_own = past // TK
        pos = lax.broadcasted_iota(jnp.int32, (QS, nk), 1)
        causal = pos <= past + lax.broadcasted_iota(jnp.int32, (QS, nk), 0)
        own_win = _own_tile(x[:, COL_WIN:COL_WIN + 2 * G_COLS]).astype(BF16)
        for g in range(B_KV_HEADS):
            gc = slice(g * HEAD_DIM, (g + 1) * HEAD_DIM)
            lc = lax.dot_general(qg[g], cmp_kv[0][:, gc], NT, preferred_element_type=F32).reshape(B_R, QS, n_blk)
            bias = []
            for r in range(B_R):
                cb = cband_ref[g * B_R + r][0:QS, :]
                b = jnp.zeros((QS, n_blk), F32)
                for u in range(4):
                    b = jnp.where(blk[:, :n_blk] == 2 * i_own + u - 2, cb[:, u:u + 1], b)
                bias.append(b)
            lc = jnp.where(valid_c[None], lc + jnp.stack(bias), NEG).reshape(B_R * QS, n_blk)
            (pr,), den, _ = _softmax_pieces([lc])
            pc = (pr / den).reshape(B_R, QS, n_blk) * jnp.where(valid_c, 1.0, 0.0)[None]
            o_cmp = jnp.dot(pc.reshape(B_R * QS, n_blk).astype(BF16), cmp_kv[1][:, gc], preferred_element_type=F32)
            imp = jnp.concatenate([jnp.sum(pc, axis=0), jnp.zeros((QS, nb_w - n_blk), F32)], axis=1)
            imp = jnp.where(blk <= cur, imp + forced, jnp.where(blk < 2 ** 20, NEG, -jnp.inf))
            selm = _select_blocks(imp, B_TOPN, blk.astype(F32)).astype(BF16)
            sel = (jnp.dot(selm, expand_ref[...], preferred_element_type=F32) > 0.5) & causal
            lg = lax.dot_general(qg[g], ks_scr[g], NT, preferred_element_type=F32)
            o_sel = _near_far_softmax(lg, sel, band_ref[g * B_R:(g + 1) * B_R, 0:QS, :], B_R, vs_scr.at[g])
            tiles = [(own_win[:, gc], own_win[:, G_COLS + g * HEAD_DIM:G_COLS + (g + 1) * HEAD_DIM], True)]
            for d in range(1, NSA_WIN_TILES + 1):
                wr = slice((NSA_WIN_TILES - d) * TK, (NSA_WIN_TILES - d + 1) * TK)
                tiles.append((win_ref[0, wr, gc].astype(BF16),
                              win_ref[0, wr, G_COLS + g * HEAD_DIM:G_COLS + (g + 1) * HEAD_DIM].astype(BF16), True))
            o_win = _window_attend(qg[g], B_R, QS, tiles, band_ref, g * B_R)
            for r in range(B_R):
                h = g * B_R + r
                rr = slice(r * QS, (r + 1) * QS)
                o_ref[0, h] = (gates[:, 3 * h:3 * h + 1] * o_cmp[rr] + gates[:, 3 * h + 1:3 * h + 2] * o_sel[rr]
                               + gates[:, 3 * h + 2:3 * h + 3] * o_win[rr])


def _sample_expand(n_pages):
    n_blk = 2 * n_pages
    nb_w = -(-(n_blk + 1) // LANES) * LANES
    lane = np.arange(nb_w)
    blk = np.where(lane < n_pages, 2 * lane, np.where(lane < n_blk, 2 * (lane - n_pages) + 1,
                                                      np.where(lane == n_blk, n_blk, -1)))
    tok_blk = np.arange((n_pages + 1) * TK) // B_BLOCK
    return jnp.asarray(blk[:, None] == tok_blk[None, :], BF16)


def _nsa_sample(proj, pool, win, page_table, layer, cmp_w, band_b, cband_b, expand):
    ns = proj.shape[0]
    n_pages = page_table.shape[1]
    nk = (n_pages + 1) * TK
    const = lambda shape: pl.BlockSpec(shape, lambda b, p, tbl: (0,) * len(shape))
    pps = _pages_per_step(n_pages)
    grid_spec = pltpu.PrefetchScalarGridSpec(
        num_scalar_prefetch=1, grid=(ns, n_pages // pps),
        in_specs=[pl.BlockSpec((1, QS, EVEN_COLS), lambda b, p, tbl: (b, 0, 0))]
        + _page_specs((1, 1, PAGE_SIZE, 4 * G_COLS), layer, n_pages, pps)
        + [pl.BlockSpec((1, B_WINDOW, 2 * G_COLS), lambda b, p, tbl: (b, 0, 0)),
                  const((2, B_BLOCK, G_COLS)), const((2, G_COLS, G_COLS)), const((2, G_COLS, G_COLS)),
                  const((B_HEADS, TK, 2 * TK)), const((B_HEADS, TK, 4)), const(expand.shape)],
        out_specs=pl.BlockSpec((1, B_HEADS, QS, HEAD_DIM), lambda b, p, tbl: (b, 0, 0, 0)),
        scratch_shapes=[pltpu.VMEM((B_KV_HEADS, nk, HEAD_DIM), BF16), pltpu.VMEM((B_KV_HEADS, nk, HEAD_DIM), BF16),
                        pltpu.VMEM((n_pages, 2 * G_COLS), F32), pltpu.VMEM((n_pages, 2 * G_COLS), F32)])
    return pl.pallas_call(
        _nsa_sample_kernel, grid_spec=grid_spec,
        out_shape=jax.ShapeDtypeStruct((ns, B_HEADS, QS, HEAD_DIM), F32),
        compiler_params=_params("parallel", "arbitrary"),
        name="nsa_sample",
    )(page_table.reshape(-1), proj, *([pool] * pps), win, *cmp_w, band_b, cband_b, expand)


def _swa_sample_kernel(q_ref, buf_ref, band_ref, sink_ref, o_ref):
    x = q_ref[0]
    q = (x[:, COL_QC:COL_QC + C_MIX] * HEAD_DIM ** -0.5).astype(BF16)
    own = _own_tile(x[:, COL_CWIN:COL_CWIN + 2 * G_COLS]).astype(BF16)
    for g in range(C_KV_HEADS):
        gc = slice(g * HEAD_DIM, (g + 1) * HEAD_DIM)
        vc = slice(G_COLS + g * HEAD_DIM, G_COLS + (g + 1) * HEAD_DIM)
        qg = _heads_to_rows(q[:, g * C_R * HEAD_DIM:(g + 1) * C_R * HEAD_DIM], C_R)
        tiles = [(own[:, gc], own[:, vc], True),
                 (buf_ref[0, :, gc].astype(BF16), buf_ref[0, :, vc].astype(BF16), True)]
        sink = jnp.concatenate([jnp.broadcast_to(sink_ref[g * C_R + r], (QS, 1)) for r in range(C_R)], axis=0)
        o = _window_attend(qg, C_R, QS, tiles, band_ref, g * C_R, sink)
        o_ref[0, g * C_R:(g + 1) * C_R] = o.reshape(C_R, QS, HEAD_DIM)


def _swa_sample(proj, buf, band_c, sinks):
    ns = proj.shape[0]
    return pl.pallas_call(
        _swa_sample_kernel,
        grid=(ns,),
        in_specs=[pl.BlockSpec((1, QS, C_MIX + 2 * G_COLS), lambda b: (b, 0, 0)),
                  pl.BlockSpec((1, C_WINDOW, 2 * G_COLS), lambda b: (b, 0, 0)),
                  _resident((C_HEADS, TK, 2 * TK)), _resident((C_HEADS, 1, 1))],
        out_specs=pl.BlockSpec((1, C_HEADS, QS, HEAD_DIM), lambda b: (b, 0, 0, 0)),
        out_shape=jax.ShapeDtypeStruct((ns, C_HEADS, QS, HEAD_DIM), F32),
        compiler_params=_params("parallel"),
        name="swa_sample",
    )(proj, buf, band_c, sinks)


def _trunk_prompt(x, mem_kv, w):
    bn, t, _ = x.shape
    n = bn * t
    a_rows, b_rows, b_states, c_states = [], [], [], []
    kb, kc = min(B_WINDOW, t), min(C_WINDOW, t)
    h = x.reshape(n, D_MODEL)
    for l in range(DEPTH):
        g, bb = w['ln_g'][l], w['ln_b'][l]
        h = _ffn_ln(h, w['ffn_wg'][l, 0], w['ffn_wu'][l, 0], w['ffn_wd'][l, 0], g[0], bb[0])
        if l % 2 == 0:
            e = l // 2
            proj = _linear(h, w['even_w_in_perm'][e]).reshape(bn, t, EVEN_COLS)
            cmp_kv = _compress_prompt(proj, *w['cmp'][e])
            o_a = _dsa_prompt(proj, w['band'][:A_HEADS])
            o_b = _nsa_prompt(proj, cmp_kv, w['band'][A_HEADS:], w['cband'][A_HEADS:])
            h = _heads_out_ln([o_a, o_b], w['even_w_out_heads'][e], h.reshape(bn, t, D_MODEL), g[1], bb[1])
            a_rows.append(proj[:, :, COL_A:COL_A + 3 * HEAD_DIM].reshape(bn, t, 3, HEAD_DIM))
            b_rows.append(proj[:, :, COL_B:COL_B + 4 * G_COLS].reshape(bn, t, 4, B_KV_HEADS, HEAD_DIM))
            b_states.append(proj[:, t - kb:, COL_WIN:COL_WIN + 2 * G_COLS].reshape(bn, kb, 2, B_KV_HEADS, HEAD_DIM))
        else:
            oi = l // 2
            proj = _linear(h, w['odd_w_in'][oi]).reshape(bn, t, C_MIX + 2 * G_COLS)
            o_c = _swa_prompt(proj, w['band'], w['sinks_shifted'][oi])
            h = _heads_out_ln([o_c], w['odd_w_out_heads'][oi], h.reshape(bn, t, D_MODEL), g[1], bb[1])
            c_states.append(proj[:, t - kc:, COL_CWIN:].reshape(bn, kc, 2, C_KV_HEADS, HEAD_DIM))
        h = h.reshape(n, D_MODEL)
        xq = _linear(h, w['x_wq'][l]).reshape(bn, t, X_WIDTH)
        xo = _cross_attn(xq, mem_kv[l].reshape(bn, -1, 2 * X_WIDTH)).reshape(n, X_WIDTH)
        h = _linear_res_ln(xo, w['x_wo'][l], h, g[2], bb[2])
        h = _ffn_ln(h, w['ffn_wg'][l, 1], w['ffn_wu'][l, 1], w['ffn_wd'][l, 1], g[3], bb[3])
    return (h.reshape(bn, t, D_MODEL), jnp.stack(a_rows), jnp.stack(b_rows), jnp.stack(b_states),
            jnp.stack(c_states))


def _shift_in(buf, new):
    return jnp.concatenate([buf, new], axis=1)[:, new.shape[1]:]


def _trunk_sample(x, mem_kv, a_pool, b_pool, b_win, c_win, page_table, w):
    ns, t, _ = x.shape
    assert t <= QS and b_win.shape[2] == B_WINDOW and c_win.shape[2] == C_WINDOW
    n = ns * QS
    n_pages = page_table.shape[1]
    a_pool = a_pool.reshape(a_pool.shape[:3] + (3 * HEAD_DIM,))
    b_pool = b_pool.reshape(b_pool.shape[:3] + (4 * G_COLS,))
    expand = _sample_expand(n_pages)
    a_rows, b_rows, b_states, c_states = [], [], [], []
    h = jnp.pad(x, ((0, 0), (0, QS - t), (0, 0))).reshape(n, D_MODEL)
    for l in range(DEPTH):
        g, bb = w['ln_g'][l], w['ln_b'][l]
        h = _ffn_ln(h, w['ffn_wg'][l, 0], w['ffn_wu'][l, 0], w['ffn_wd'][l, 0], g[0], bb[0])
        if l % 2 == 0:
            e = l // 2
            proj = _linear(h, w['even_w_in_perm'][e]).reshape(ns, QS, EVEN_COLS)
            win = b_win[e].reshape(ns, B_WINDOW, 2 * G_COLS)
            o_a = _dsa_sample(proj, a_pool, page_table, e, w['band'][:A_HEADS])
            o_b = _nsa_sample(proj, b_pool, win, page_table, e, w['cmp'][e], w['band'][A_HEADS:],
                              w['cband'][A_HEADS:], expand)
            h = _heads_out_ln([o_a, o_b], w['even_w_out_heads'][e], h.reshape(ns, QS, D_MODEL), g[1], bb[1])
            a_rows.append(proj[:, :t, COL_A:COL_A + 3 * HEAD_DIM].reshape(ns, t, 3, HEAD_DIM))
            b_rows.append(proj[:, :t, COL_B:COL_B + 4 * G_COLS].reshape(ns, t, 4, B_KV_HEADS, HEAD_DIM))
            b_states.append(_shift_in(win, proj[:, :t, COL_WIN:COL_WIN + 2 * G_COLS]).reshape(b_win.shape[1:]))
        else:
            oi = l // 2
            proj = _linear(h, w['odd_w_in'][oi]).reshape(ns, QS, C_MIX + 2 * G_COLS)
            buf = c_win[oi].reshape(ns, C_WINDOW, 2 * G_COLS)
            o_c = _swa_sample(proj, buf, w['band'], w['sinks_shifted'][oi])
            h = _heads_out_ln([o_c], w['odd_w_out_heads'][oi], h.reshape(ns, QS, D_MODEL), g[1], bb[1])
            c_states.append(_shift_in(buf, proj[:, :t, COL_CWIN:]).reshape(c_win.shape[1:]))
        h = h.reshape(n, D_MODEL)
        xq = _linear(h, w['x_wq'][l]).reshape(ns, QS, X_WIDTH)
        xo = _cross_attn(xq, mem_kv[l].reshape(ns, -1, 2 * X_WIDTH)).reshape(n, X_WIDTH)
        h = _linear_res_ln(xo, w['x_wo'][l], h, g[2], bb[2])
        h = _ffn_ln(h, w['ffn_wg'][l, 1], w['ffn_wu'][l, 1], w['ffn_wd'][l, 1], g[3], bb[3])
    return (h.reshape(ns, QS, D_MODEL)[:, :t], jnp.stack(a_rows), jnp.stack(b_rows), jnp.stack(b_states),
            jnp.stack(c_states))


def kernel(x_prompt, x_sample, cache_a_kv, cache_b_kv, cache_b_win, cache_c_win, cache_mem_kv, page_table,
           mem_prompt, ln_g, ln_b, ffn_wg, ffn_wu, ffn_wd, even_w_in, even_w_out, nsa_cmp_pos, nsa_cmp_w1,
           nsa_cmp_w2, odd_w_in, odd_w_out, c_sinks, x_wq, x_wk, x_wv, x_wo, rel_table):
    nb, n_mem = mem_prompt.shape[:2]
    dist_bias = _distance_bias(rel_table)
    w = dict(ln_g=ln_g, ln_b=ln_b, ffn_wg=ffn_wg.astype(BF16), ffn_wu=ffn_wu.astype(BF16),
             ffn_wd=ffn_wd.astype(BF16), odd_w_in=odd_w_in.astype(BF16),
             x_wq=x_wq.astype(BF16), x_wo=x_wo.astype(BF16),
             even_w_in_perm=_permute_even_w_in(even_w_in).astype(BF16),
             even_w_out_heads=even_w_out.astype(BF16).reshape(-1, A_HEADS + B_HEADS, HEAD_DIM, D_MODEL),
             odd_w_out_heads=odd_w_out.astype(BF16).reshape(-1, C_HEADS, HEAD_DIM, D_MODEL),
             band=_band(dist_bias), cband=_cmp_band(dist_bias),
             cmp=[_compress_weights(nsa_cmp_pos[e], nsa_cmp_w1[e], nsa_cmp_w2[e]) for e in range(even_w_in.shape[0])],
             sinks_shifted=(c_sinks - rel_table[N_BUCKETS - 1][None, :]).reshape(-1, C_HEADS, 1, 1))
    w_kv = jnp.concatenate([x_wk, x_wv], axis=-1).astype(BF16)
    mem2d = mem_prompt.reshape(nb * n_mem, D_MODEL)
    mem_kv_prompt = jnp.stack([_linear(mem2d, w_kv[l]) for l in range(DEPTH)])
    mem_kv_prompt = mem_kv_prompt.reshape(DEPTH, nb, n_mem, 2, X_HEADS, X_HEAD_DIM)
    y_p, a_p, b_p, bw_p, cw_p = _trunk_prompt(x_prompt, mem_kv_prompt, w)
    y_s, a_s, b_s, bw_s, cw_s = _trunk_sample(x_sample, cache_mem_kv, cache_a_kv, cache_b_kv, cache_b_win,
                                              cache_c_win, page_table, w)
    return (y_p, y_s, a_p, a_s, b_p, b_s, bw_p, bw_s, cw_p, cw_s, mem_kv_prompt)
```

```python
import functools
import math

import jax
import jax.numpy as jnp
import numpy as np
from jax import lax
from jax.experimental import pallas as pl
from jax.experimental.pallas import tpu as pltpu

D_MODEL = 1024
DEPTH = 4
PAGE_SIZE = 128
HEAD_DIM = 64
A_HEADS = 8
A_IDX_HEADS = 4
A_IDX_DIM = 64
A_TOPK = 256
B_HEADS = 8
B_KV_HEADS = 2
B_BLOCK = 64
B_TOPN = 16
B_WINDOW = 512
C_HEADS = 16
C_KV_HEADS = 2
C_WINDOW = 128
X_HEADS = 4
X_HEAD_DIM = 128
D_FF = 2816
N_BUCKETS = 32
BUCKET_MAX_DIST = 128
QBLOCK = 128
LN_EPS = 1e-5
ALPHA = (2 * DEPTH) ** 0.25
NEG = -1e30
FORCE_SCORE = 1e4
EVEN_SPLITS = [A_HEADS * HEAD_DIM, HEAD_DIM, HEAD_DIM, A_IDX_HEADS * A_IDX_DIM, A_IDX_DIM, A_IDX_HEADS,
               B_HEADS * HEAD_DIM] + [B_KV_HEADS * HEAD_DIM] * 6 + [B_HEADS * 3]
ODD_SPLITS = [C_HEADS * HEAD_DIM, C_KV_HEADS * HEAD_DIM, C_KV_HEADS * HEAD_DIM]
A_MIX = A_HEADS * HEAD_DIM
B_MIX = B_HEADS * HEAD_DIM
C_MIX = C_HEADS * HEAD_DIM
X_WIDTH = X_HEADS * X_HEAD_DIM

LANES = 128
VMEM_LIMIT = 56 << 20
BF16 = jnp.bfloat16
F32 = jnp.float32


def _params(*sem):
    return pltpu.CompilerParams(dimension_semantics=sem, vmem_limit_bytes=VMEM_LIMIT)


def _resident(shape):
    return pl.BlockSpec(shape, lambda *_: (0,) * len(shape), pipeline_mode=pl.Buffered(1))


def _per_batch(shape, col_block):
    return pl.BlockSpec(shape, lambda b, i: (b, 0, col_block), pipeline_mode=pl.Buffered(1))


def _row_tile(n):
    return min(n, 512)


def _layer_norm_rows(y, g, b):
    mu = jnp.mean(y, axis=-1, keepdims=True)
    d = y - mu
    var = jnp.mean(d * d, axis=-1, keepdims=True)
    return d * lax.rsqrt(var + LN_EPS) * g + b


FF_CHUNKS = 2
FF_CHUNK = D_FF // FF_CHUNKS


def _ffn_ln_kernel(x_ref, wg_ref, wu_ref, wd_ref, g_ref, b_ref, o_ref):
    x = x_ref[...]
    xb = x.astype(BF16)
    acc = jnp.zeros(x.shape, F32)
    for c in range(FF_CHUNKS):
        cols = slice(c * FF_CHUNK, (c + 1) * FF_CHUNK)
        gate = jnp.dot(xb, wg_ref[:, cols], preferred_element_type=F32)
        up = jnp.dot(xb, wu_ref[:, cols], preferred_element_type=F32)
        act = (gate * jax.nn.sigmoid(gate) * up).astype(BF16)
        acc = acc + jnp.dot(act, wd_ref[cols, :], preferred_element_type=F32)
    o_ref[...] = _layer_norm_rows(ALPHA * x + 0.5 * acc, g_ref[...], b_ref[...])


def _ffn_ln(x, wg, wu, wd, g, b):
    n = x.shape[0]
    tm = _row_tile(n)
    return pl.pallas_call(
        _ffn_ln_kernel,
        grid=(n // tm,),
        in_specs=[pl.BlockSpec((tm, D_MODEL), lambda i: (i, 0)),
                  _resident((D_MODEL, D_FF)), _resident((D_MODEL, D_FF)), _resident((D_FF, D_MODEL)),
                  _resident((1, D_MODEL)), _resident((1, D_MODEL))],
        out_specs=pl.BlockSpec((tm, D_MODEL), lambda i: (i, 0)),
        out_shape=jax.ShapeDtypeStruct((n, D_MODEL), F32),
        compiler_params=_params("parallel"),
        name="ffn_ln",
    )(x, wg, wu, wd, g.reshape(1, D_MODEL), b.reshape(1, D_MODEL))


def _linear_kernel(x_ref, w_ref, o_ref):
    o_ref[...] = jnp.dot(x_ref[...].astype(BF16), w_ref[...], preferred_element_type=F32)


def _linear(x, w):
    n, k = x.shape
    m = w.shape[1]
    tm = _row_tile(n)
    return pl.pallas_call(
        _linear_kernel,
        grid=(n // tm,),
        in_specs=[pl.BlockSpec((tm, k), lambda i: (i, 0)), _resident((k, m))],
        out_specs=pl.BlockSpec((tm, m), lambda i: (i, 0)),
        out_shape=jax.ShapeDtypeStruct((n, m), F32),
        compiler_params=_params("parallel"),
        name="linear",
    )(x, w)


def _linear_res_ln_kernel(x_ref, w_ref, h_ref, g_ref, b_ref, o_ref):
    y = jnp.dot(x_ref[...].astype(BF16), w_ref[...], preferred_element_type=F32)
    o_ref[...] = _layer_norm_rows(ALPHA * h_ref[...] + y, g_ref[...], b_ref[...])


def _linear_res_ln(x, w, h, g, b):
    n, k = x.shape
    tm = _row_tile(n)
    return pl.pallas_call(
        _linear_res_ln_kernel,
        grid=(n // tm,),
        in_specs=[pl.BlockSpec((tm, k), lambda i: (i, 0)), _resident((k, D_MODEL)),
                  pl.BlockSpec((tm, D_MODEL), lambda i: (i, 0)),
                  _resident((1, D_MODEL)), _resident((1, D_MODEL))],
        out_specs=pl.BlockSpec((tm, D_MODEL), lambda i: (i, 0)),
        out_shape=jax.ShapeDtypeStruct((n, D_MODEL), F32),
        compiler_params=_params("parallel"),
        name="linear_res_ln",
    )(x, w, h, g.reshape(1, D_MODEL), b.reshape(1, D_MODEL))


def t5_bucket(dist):
    exact = N_BUCKETS // 2
    d = jnp.maximum(dist, 0)
    rel = jnp.log(jnp.maximum(d, 1).astype(jnp.float32) / exact) / math.log(BUCKET_MAX_DIST / exact)
    large = jnp.minimum(exact + (rel * (N_BUCKETS - exact)).astype(jnp.int32), N_BUCKETS - 1)
    return jnp.where(d < exact, d, large)


TK = 128
INT_MIN = -2 ** 31
NT = (((1,), (1,)), ((), ()))


def _ordered_key(x):
    bits = pltpu.bitcast(x, jnp.int32)
    key = jnp.where(bits < 0, bits ^ jnp.int32(0x7FFFFFFF), bits)
    return jnp.where(x == 0.0, 0, key)


def _count(key_scr, nkt, rows, pred):
    def body(j, c):
        return c + jnp.where(pred(key_scr[j], j), 1.0, 0.0)
    c = lax.fori_loop(0, nkt, body, jnp.zeros((rows, TK), F32))
    return jnp.sum(c, axis=1, keepdims=True)


def _topk_threshold(key_scr, nkt, rows, k, idx_bits):
    lane = lax.broadcasted_iota(jnp.int32, (rows, TK), 1)

    def value_bit(it, ans):
        cand = ans + lax.shift_left(jnp.int32(1), 31 - it)
        cand_key = jnp.broadcast_to(cand, (rows, TK))
        cnt = _count(key_scr, nkt, rows, lambda t, j: t >= cand_key)
        return jnp.where(cnt >= k, cand, ans)
    tau = lax.fori_loop(0, 32, value_bit, jnp.full((rows, 1), INT_MIN, jnp.int32))
    n_ge = _count(key_scr, nkt, rows, lambda t, j: t >= tau)

    def cut_among_ties():
        need = k - _count(key_scr, nkt, rows, lambda t, j: t > tau)

        def index_bit(it, cut):
            cand = cut | lax.shift_left(jnp.int32(1), idx_bits - 1 - it)
            cnt = _count(key_scr, nkt, rows,
                         lambda t, j: jnp.where(t == tau, j * TK + lane, jnp.int32(2 ** 30)) < cand)
            return jnp.where(cnt < need, cand, cut)
        return lax.fori_loop(0, idx_bits, index_bit, jnp.zeros((rows, 1), jnp.int32))

    cut = lax.cond(jnp.max(n_ge) > k, cut_among_ties, lambda: jnp.full((rows, 1), 2 ** 30, jnp.int32))
    return tau, cut


def _in_topk(t, j, tau, cut):
    lane = lax.broadcasted_iota(jnp.int32, t.shape, 1)
    return (t > tau) | ((t == tau) & (j * TK + lane <= cut))


def _heads_to_rows(x, n_heads):
    return jnp.concatenate([x[:, h * HEAD_DIM:(h + 1) * HEAD_DIM] for h in range(n_heads)], axis=0)


FAR_TILES = 4


def _flash_init(m_scr, acc_scr):
    m_scr[...] = jnp.full(m_scr.shape, NEG, F32)
    acc_scr[...] = jnp.zeros(acc_scr.shape, F32)


def _with_ones(v):
    return jnp.concatenate([v, jnp.ones(v.shape, BF16)], axis=1)


def _flash_step(m_scr, acc_scr, heads, q_rows, k, v1, madd, bias):
    n_h = q_rows.shape[0] // TK
    lg = lax.dot_general(q_rows, k, NT, preferred_element_type=F32).reshape(n_h, TK, k.shape[0]) + madd[None]
    if bias is not None:
        lg = lg + bias
    lg = lg.reshape(n_h * TK, k.shape[0])
    m_old = m_scr[heads].reshape(n_h * TK, 1)
    m_new = jnp.maximum(m_old, jnp.max(lg, axis=1, keepdims=True))
    p = jnp.exp(lg - m_new).astype(BF16)
    acc = jnp.exp(m_old - m_new) * acc_scr[heads].reshape(n_h * TK, 2 * HEAD_DIM)
    acc_scr[heads] = (acc + jnp.dot(p, v1, preferred_element_type=F32)).reshape(n_h, TK, 2 * HEAD_DIM)
    m_scr[heads] = m_new.reshape(n_h, TK, 1)


def _flash_far_loop(n_far, attend):
    n_big = n_far // FAR_TILES

    def big(jb, carry):
        attend(jb * FAR_TILES, FAR_TILES)
        return carry
    lax.fori_loop(0, n_big, big, 0)

    def small(j, carry):
        attend(j, 1)
        return carry
    lax.fori_loop(n_big * FAR_TILES, n_far, small, 0)


def _flash_result(acc_scr, h):
    acc = acc_scr[h]
    return acc[:, 0:HEAD_DIM] / acc[:, HEAD_DIM:2 * HEAD_DIM]


def _window_attend(qg, n_r, qr, tiles, band_ref, head0, sink=None):
    row = lax.broadcasted_iota(jnp.int32, (qr, TK), 0)
    lane = lax.broadcasted_iota(jnp.int32, (qr, TK), 1)
    heads = slice(head0, head0 + n_r)
    pieces = []
    for d, (k, _, exists) in enumerate(tiles):
        lg = lax.dot_general(qg, k, NT, preferred_element_type=F32).reshape(n_r, qr, TK)
        ok = jnp.broadcast_to(exists, (qr, TK))
        if d == 0:
            ok = ok & (lane <= row)
            lg = lg + band_ref[heads, 0:qr, TK:2 * TK]
        if d == 1:
            lg = lg + band_ref[heads, 0:qr, 0:TK]
        if d == len(tiles) - 1:
            ok = ok & (lane > row)
        pieces.append(jnp.where(ok[None], lg, NEG).reshape(n_r * qr, TK))
    m = functools.reduce(jnp.maximum, [jnp.max(x, axis=1, keepdims=True) for x in pieces])
    if sink is not None:
        m = jnp.maximum(m, sink)
    ps = [jnp.exp(x - m) for x in pieces]
    den = functools.reduce(lambda a, b: a + b, [jnp.sum(p, axis=1, keepdims=True) for p in ps])
    if sink is not None:
        den = den + jnp.exp(sink - m)
    o = functools.reduce(lambda a, b: a + b,
                         [jnp.dot(p.astype(BF16), t[1], preferred_element_type=F32) for p, t in zip(ps, tiles)])
    return o / den


def _distance_bias(rel_table):
    tb = rel_table[t5_bucket(jnp.arange(2 * TK))]
    return tb - rel_table[N_BUCKETS - 1][None, :]


def _band(dist_bias):
    a = np.arange(TK)[:, None]
    c = np.arange(2 * TK)[None, :]
    idx = np.clip(a - c + TK, 0, 2 * TK - 1)
    return jnp.transpose(dist_bias[idx], (2, 0, 1))


def _dsa_prompt_kernel(q_ref, qi_ref, misc_ref, kv_ref, band_ref, o_ref, key_scr, m_scr, acc_scr):
    i = pl.program_id(1)
    nkt = i + 1
    row = lax.broadcasted_iota(jnp.int32, (TK, TK), 0)
    lane = lax.broadcasted_iota(jnp.int32, (TK, TK), 1)
    qi = qi_ref[0].astype(BF16)
    wi = [jnp.broadcast_to(misc_ref[0][:, MISC_WI + h:MISC_WI + h + 1], (TK, TK)) for h in range(A_IDX_HEADS)]

    def score_tiles(j0, n, causal=False):
        rows = pl.ds(pl.multiple_of(j0 * TK, TK), n * TK)
        kt = kv_ref[0, rows, 2 * HEAD_DIM:3 * HEAD_DIM].astype(BF16)
        sc = jnp.zeros((TK, n * TK), F32)
        for h in range(A_IDX_HEADS):
            s = lax.dot_general(qi[:, h * A_IDX_DIM:(h + 1) * A_IDX_DIM], kt, NT, preferred_element_type=F32)
            sc = sc + jnp.concatenate([wi[h]] * n, axis=1) * jnp.maximum(s * A_IDX_DIM ** -0.5, 0.0)
        sc = sc * A_IDX_HEADS ** -0.5
        if causal:
            sc = jnp.where(lane <= row, sc, NEG)
        for t in range(n):
            key_scr[j0 + t] = _ordered_key(sc[:, t * TK:(t + 1) * TK])
    _flash_far_loop(i, score_tiles)
    score_tiles(i, 1, True)
    tau, cut = _topk_threshold(key_scr, nkt, TK, float(A_TOPK), DSA_IDX_BITS)

    q = (q_ref[0] * HEAD_DIM ** -0.5).astype(BF16)
    qh = _heads_to_rows(q, A_HEADS)
    _flash_init(m_scr, acc_scr)

    def attend(j0, n, bias_cols=None, causal=False):
        rows = pl.ds(pl.multiple_of(j0 * TK, TK), n * TK)
        k = kv_ref[0, rows, 0:HEAD_DIM].astype(BF16)
        v1 = _with_ones(kv_ref[0, rows, HEAD_DIM:2 * HEAD_DIM].astype(BF16))
        sels = [_in_topk(key_scr[j0 + t], j0 + t, tau, cut) for t in range(n)]
        if causal:
            sels = [s & (lane <= row) for s in sels]
        madd = jnp.concatenate([jnp.where(s, 0.0, NEG) for s in sels], axis=1)
        bias = None if bias_cols is None else band_ref[:, :, bias_cols]
        _flash_step(m_scr, acc_scr, slice(0, A_HEADS), qh, k, v1, madd, bias)

    _flash_far_loop(jnp.maximum(i - 1, 0), attend)

    @pl.when(i >= 1)
    def _():
        attend(i - 1, 1, slice(0, TK))
    attend(i, 1, slice(TK, 2 * TK), True)
    for h in range(A_HEADS):
        o_ref[0, h] = _flash_result(acc_scr, h)


def _dsa_prompt(proj, band_a):
    bn, t, _ = proj.shape
    nqb = t // TK
    return pl.pallas_call(
        _dsa_prompt_kernel,
        grid=(bn, nqb),
        in_specs=[pl.BlockSpec((1, TK, A_MIX), lambda b, i: (b, i, COL_QA // A_MIX)),
                  pl.BlockSpec((1, TK, 256), lambda b, i: (b, i, COL_QI // 256)),
                  pl.BlockSpec((1, TK, 256), lambda b, i: (b, i, COL_A // 256)),
                  _per_batch((1, t, 256), COL_A // 256),
                  _resident((A_HEADS, TK, 2 * TK))],
        out_specs=pl.BlockSpec((1, A_HEADS, TK, HEAD_DIM), lambda b, i: (b, 0, i, 0)),
        out_shape=jax.ShapeDtypeStruct((bn, A_HEADS, t, HEAD_DIM), F32),
        scratch_shapes=[pltpu.VMEM((nqb, TK, TK), jnp.int32),
                        pltpu.VMEM((A_HEADS, TK, 1), F32), pltpu.VMEM((A_HEADS, TK, 2 * HEAD_DIM), F32)],
        compiler_params=_params("parallel", "arbitrary"),
        name="dsa_prompt",
    )(proj, proj, proj, proj, band_a)


COL_QA, COL_QB, COL_QI, COL_A = 0, 512, 1024, 1280
COL_B, COL_WIN, EVEN_COLS = 1536, 2048, 2304
MISC_WI = 3 * HEAD_DIM
MISC_GL = MISC_WI + A_IDX_HEADS
DSA_IDX_BITS = 14


def _even_in_perm():
    offs = np.concatenate([[0], np.cumsum(EVEN_SPLITS)])
    seg = lambda k: np.arange(offs[k], offs[k + 1])
    (q_a, k_a, v_a, qi, ki, wi, q_b, kc, vc, ksl, vsl, kw, vw, gl) = [seg(k) for k in range(14)]
    order = [q_a, q_b, qi, k_a, v_a, ki, wi, gl]
    pad = EVEN_COLS - sum(EVEN_SPLITS)
    return np.concatenate(order), pad, np.concatenate([kc, vc, ksl, vsl, kw, vw])


def _permute_even_w_in(w):
    head, pad, tail = _even_in_perm()
    zeros = jnp.zeros(w.shape[:-1] + (pad,), w.dtype)
    return jnp.concatenate([w[..., head], zeros, w[..., tail]], axis=-1)


G_COLS = B_KV_HEADS * HEAD_DIM


def _compress_rows(x, pos_w, w1, w2):
    nb = x.shape[0] // B_BLOCK
    pooled = jnp.sum(x.reshape(nb, B_BLOCK, G_COLS) * pos_w[None], axis=1)
    hid = jax.nn.gelu(jnp.dot(pooled.astype(BF16), w1, preferred_element_type=F32))
    return jnp.dot(hid.astype(BF16), w2, preferred_element_type=F32)


def _compress_kernel(kv_ref, pos_ref, w1_ref, w2_ref, o_ref):
    for s in range(2):
        x = kv_ref[0, :, s * G_COLS:(s + 1) * G_COLS]
        o_ref[0, :, s * G_COLS:(s + 1) * G_COLS] = _compress_rows(x, pos_ref[s], w1_ref[s], w2_ref[s])


def _compress_weights(cmp_pos, cmp_w1, cmp_w2):
    pos = jnp.concatenate([cmp_pos] * B_KV_HEADS, axis=-1)
    eye = jnp.eye(B_KV_HEADS, dtype=cmp_w1.dtype)
    bd = lambda w: jnp.einsum('gh,sde->sgdhe', eye, w).reshape(2, G_COLS, G_COLS).astype(BF16)
    return pos, bd(cmp_w1), bd(cmp_w2)


def _compress_prompt(proj, pos, w1, w2):
    bn, t, _ = proj.shape
    tr = min(t, 2048)
    return pl.pallas_call(
        _compress_kernel,
        grid=(bn, t // tr),
        in_specs=[pl.BlockSpec((1, tr, 2 * G_COLS), lambda b, i: (b, i, COL_B // (2 * G_COLS))),
                  _resident((2, B_BLOCK, G_COLS)), _resident((2, G_COLS, G_COLS)), _resident((2, G_COLS, G_COLS))],
        out_specs=pl.BlockSpec((1, tr // B_BLOCK, 2 * G_COLS), lambda b, i: (b, i, 0)),
        out_shape=jax.ShapeDtypeStruct((bn, t // B_BLOCK, 2 * G_COLS), F32),
        compiler_params=_params("parallel", "parallel"),
        name="nsa_compress",
    )(proj, pos, w1, w2)


B_R = B_HEADS // B_KV_HEADS
NSA_WIN_TILES = B_WINDOW // TK


def _softmax_pieces(pieces):
    m = functools.reduce(jnp.maximum, [jnp.max(x, axis=1, keepdims=True) for x in pieces])
    ps = [jnp.exp(x - m) for x in pieces]
    den = functools.reduce(lambda a, b: a + b, [jnp.sum(p, axis=1, keepdims=True) for p in ps])
    return ps, den, m


def _select_blocks(imp, n_sel, lanef=None):
    if lanef is None:
        lanef = lax.broadcasted_iota(jnp.int32, imp.shape, 1).astype(F32)
    sel = jnp.zeros(imp.shape, F32)
    for _ in range(n_sel):
        mx = jnp.max(imp, axis=1, keepdims=True)
        first = jnp.min(jnp.where(imp == mx, lanef, float(2 ** 24)), axis=1, keepdims=True)
        hit = lanef == first
        sel = jnp.where(hit, 1.0, sel)
        imp = jnp.where(hit, -jnp.inf, imp)
    return sel


def _nsa_prompt_kernel(q_ref, misc_ref, kv_ref, win_ref, cmp_ref, band_ref, cband_ref, o_ref,
                       selm_scr, m_scr, acc_scr):
    i = pl.program_id(1)
    n_blk = cmp_ref.shape[1]
    row = lax.broadcasted_iota(jnp.int32, (TK, TK), 0)
    lane = lax.broadcasted_iota(jnp.int32, (TK, TK), 1)
    tpos = i * TK + row
    q = (q_ref[0] * HEAD_DIM ** -0.5).astype(BF16)
    gates = jax.nn.sigmoid(misc_ref[0][:, MISC_GL:MISC_GL + 3 * B_HEADS])
    qg = [_heads_to_rows(q[:, g * B_R * HEAD_DIM:(g + 1) * B_R * HEAD_DIM], B_R) for g in range(B_KV_HEADS)]

    blk = lax.broadcasted_iota(jnp.int32, (TK, n_blk), 1)
    brow = lax.broadcasted_iota(jnp.int32, (TK, n_blk), 0)
    bpos = i * TK + brow
    valid_c = (blk + 1) * B_BLOCK - 1 <= bpos
    cur = bpos // B_BLOCK
    forced = jnp.where((blk == 0) | (blk == cur) | (blk == cur - 1), FORCE_SCORE, 0.0)
    o_cmp, imps = [], []
    for g in range(B_KV_HEADS):
        kc = cmp_ref[0, :, g * HEAD_DIM:(g + 1) * HEAD_DIM].astype(BF16)
        vc = cmp_ref[0, :, G_COLS + g * HEAD_DIM:G_COLS + (g + 1) * HEAD_DIM].astype(BF16)
        lc = lax.dot_general(qg[g], kc, NT, preferred_element_type=F32).reshape(B_R, TK, n_blk)
        bias = []
        for r in range(B_R):
            cb = cband_ref[g * B_R + r]
            b = jnp.zeros((TK, n_blk), F32)
            for u in range(4):
                b = jnp.where(blk == 2 * i + u - 2, cb[:, u:u + 1], b)
            bias.append(b)
        lc = jnp.where(valid_c[None], lc + jnp.stack(bias), NEG).reshape(B_R * TK, n_blk)
        (p,), den, _ = _softmax_pieces([lc])
        pc = ((p / den).reshape(B_R, TK, n_blk) * jnp.where(valid_c, 1.0, 0.0)[None])
        o_cmp.append(jnp.dot(pc.reshape(B_R * TK, n_blk).astype(BF16), vc, preferred_element_type=F32))
        imps.append(jnp.where(blk <= cur, jnp.sum(pc, axis=0) + forced, NEG))
    selm = _select_blocks(jnp.concatenate(imps, axis=0), min(B_TOPN, n_blk))
    for g in range(B_KV_HEADS):
        selm_scr[g] = selm[g * TK:(g + 1) * TK].astype(BF16)

    _flash_init(m_scr, acc_scr)

    def attend(j0, n, bias_cols=None, causal=False):
        rows = pl.ds(pl.multiple_of(j0 * TK, TK), n * TK)
        ebk = lax.broadcasted_iota(jnp.int32, (n_blk, n * TK), 0)
        etok = j0 * TK + lax.broadcasted_iota(jnp.int32, (n_blk, n * TK), 1)
        expand = jnp.where(ebk == etok // B_BLOCK, 1.0, 0.0).astype(BF16)
        for g in range(B_KV_HEADS):
            k = kv_ref[0, rows, g * HEAD_DIM:(g + 1) * HEAD_DIM].astype(BF16)
            v1 = _with_ones(kv_ref[0, rows, G_COLS + g * HEAD_DIM:G_COLS + (g + 1) * HEAD_DIM].astype(BF16))
            sel = jnp.dot(selm_scr[g], expand, preferred_element_type=F32) > 0.5
            if causal:
                sel = sel & (lane <= row)
            madd = jnp.where(sel, 0.0, NEG)
            heads = slice(g * B_R, (g + 1) * B_R)
            bias = None if bias_cols is None else band_ref[heads, :, bias_cols]
            _flash_step(m_scr, acc_scr, heads, qg[g], k, v1, madd, bias)

    _flash_far_loop(jnp.maximum(i - 1, 0), attend)

    @pl.when(i >= 1)
    def _():
        attend(i - 1, 1, slice(0, TK))
    attend(i, 1, slice(TK, 2 * TK), True)

    for g in range(B_KV_HEADS):
        tiles = []
        for d in range(NSA_WIN_TILES + 1):
            rows = pl.ds(pl.multiple_of(jnp.maximum(i - d, 0) * TK, TK), TK)
            tiles.append((win_ref[0, rows, g * HEAD_DIM:(g + 1) * HEAD_DIM].astype(BF16),
                          win_ref[0, rows, G_COLS + g * HEAD_DIM:G_COLS + (g + 1) * HEAD_DIM].astype(BF16),
                          i - d >= 0))
        o_win = _window_attend(qg[g], B_R, TK, tiles, band_ref, g * B_R)
        for r in range(B_R):
            h = g * B_R + r
            rr = slice(r * TK, (r + 1) * TK)
            o_ref[0, h] = (gates[:, 3 * h:3 * h + 1] * o_cmp[g][rr]
                           + gates[:, 3 * h + 1:3 * h + 2] * _flash_result(acc_scr, h)
                           + gates[:, 3 * h + 2:3 * h + 3] * o_win[rr])


def _cmp_band(dist_bias):
    a = np.arange(TK)[:, None]
    u = np.arange(4)[None, :] - 2
    idx = np.clip(a - B_BLOCK * u - (B_BLOCK - 1), 0, 2 * TK - 1)
    return jnp.transpose(dist_bias[idx], (2, 0, 1))


def _nsa_prompt(proj, cmp_kv, band_b, cband_b):
    bn, t, _ = proj.shape
    nqb = t // TK
    n_blk = cmp_kv.shape[1]
    return pl.pallas_call(
        _nsa_prompt_kernel,
        grid=(bn, nqb),
        in_specs=[pl.BlockSpec((1, TK, B_MIX), lambda b, i: (b, i, COL_QB // B_MIX)),
                  pl.BlockSpec((1, TK, 256), lambda b, i: (b, i, COL_A // 256)),
                  _per_batch((1, t, 2 * G_COLS), (COL_B + 2 * G_COLS) // (2 * G_COLS)),
                  _per_batch((1, t, 2 * G_COLS), COL_WIN // (2 * G_COLS)),
                  _per_batch((1, n_blk, 2 * G_COLS), 0),
                  _resident((B_HEADS, TK, 2 * TK)), _resident((B_HEADS, TK, 4))],
        out_specs=pl.BlockSpec((1, B_HEADS, TK, HEAD_DIM), lambda b, i: (b, 0, i, 0)),
        out_shape=jax.ShapeDtypeStruct((bn, B_HEADS, t, HEAD_DIM), F32),
        scratch_shapes=[pltpu.VMEM((B_KV_HEADS, TK, n_blk), BF16),
                        pltpu.VMEM((B_HEADS, TK, 1), F32), pltpu.VMEM((B_HEADS, TK, 2 * HEAD_DIM), F32)],
        compiler_params=_params("parallel", "arbitrary"),
        name="nsa_prompt",
    )(proj, proj, proj, proj, cmp_kv, band_b, cband_b)


C_R = C_HEADS // C_KV_HEADS
COL_QC, COL_CWIN = 0, C_MIX


def _swa_prompt_kernel(q_ref, kv_ref, band_ref, sink_ref, o_ref):
    i = pl.program_id(1)
    row = lax.broadcasted_iota(jnp.int32, (TK, TK), 0)
    lane = lax.broadcasted_iota(jnp.int32, (TK, TK), 1)
    q = (q_ref[0] * HEAD_DIM ** -0.5).astype(BF16)
    prev_rows = pl.ds(pl.multiple_of(jnp.maximum(i - 1, 0) * TK, TK), TK)
    own_rows = pl.ds(pl.multiple_of(i * TK, TK), TK)
    prev_ok = jnp.broadcast_to(i >= 1, (TK, TK)) & (lane > row)
    for g in range(C_KV_HEADS):
        qg = _heads_to_rows(q[:, g * C_R * HEAD_DIM:(g + 1) * C_R * HEAD_DIM], C_R)
        pieces, vals = [], []
        for rows, ok, cols in ((prev_rows, prev_ok, slice(0, TK)), (own_rows, lane <= row, slice(TK, 2 * TK))):
            k = kv_ref[0, rows, g * HEAD_DIM:(g + 1) * HEAD_DIM].astype(BF16)
            vals.append(kv_ref[0, rows, G_COLS + g * HEAD_DIM:G_COLS + (g + 1) * HEAD_DIM].astype(BF16))
            lg = lax.dot_general(qg, k, NT, preferred_element_type=F32).reshape(C_R, TK, TK)
            lg = lg + band_ref[g * C_R:(g + 1) * C_R, :, cols]
            pieces.append(jnp.where(ok[None], lg, NEG).reshape(C_R * TK, TK))
        sink = jnp.concatenate([jnp.broadcast_to(sink_ref[g * C_R + r], (TK, 1)) for r in range(C_R)], axis=0)
        m = jnp.maximum(jnp.maximum(jnp.max(pieces[0], axis=1, keepdims=True),
                                    jnp.max(pieces[1], axis=1, keepdims=True)), sink)
        ps = [jnp.exp(x - m) for x in pieces]
        den = jnp.sum(ps[0], axis=1, keepdims=True) + jnp.sum(ps[1], axis=1, keepdims=True) + jnp.exp(sink - m)
        o = (jnp.dot(ps[0].astype(BF16), vals[0], preferred_element_type=F32)
             + jnp.dot(ps[1].astype(BF16), vals[1], preferred_element_type=F32)) / den
        o_ref[0, g * C_R:(g + 1) * C_R] = o.reshape(C_R, TK, HEAD_DIM)


def _swa_prompt(proj, band_c, sinks):
    bn, t, _ = proj.shape
    return pl.pallas_call(
        _swa_prompt_kernel,
        grid=(bn, t // TK),
        in_specs=[pl.BlockSpec((1, TK, C_MIX), lambda b, i: (b, i, COL_QC // C_MIX)),
                  _per_batch((1, t, 2 * G_COLS), COL_CWIN // (2 * G_COLS)),
                  _resident((C_HEADS, TK, 2 * TK)), _resident((C_HEADS, 1, 1))],
        out_specs=pl.BlockSpec((1, C_HEADS, TK, HEAD_DIM), lambda b, i: (b, 0, i, 0)),
        out_shape=jax.ShapeDtypeStruct((bn, C_HEADS, t, HEAD_DIM), F32),
        compiler_params=_params("parallel", "arbitrary"),
        name="swa_prompt",
    )(proj, proj, band_c, sinks)


def _cross_attn_kernel(q_ref, mem_ref, o_ref):
    q = (q_ref[0] * X_HEAD_DIM ** -0.5).astype(BF16)
    for h in range(X_HEADS):
        cols = slice(h * X_HEAD_DIM, (h + 1) * X_HEAD_DIM)
        k = mem_ref[0, :, cols].astype(BF16)
        v = mem_ref[0, :, X_WIDTH + h * X_HEAD_DIM:X_WIDTH + (h + 1) * X_HEAD_DIM].astype(BF16)
        s = lax.dot_general(q[:, cols], k, NT, preferred_element_type=F32)
        p = jnp.exp(s - jnp.max(s, axis=1, keepdims=True))
        den = jnp.sum(p, axis=1, keepdims=True)
        o_ref[0, :, cols] = jnp.dot(p.astype(BF16), v, preferred_element_type=F32) / den


def _cross_attn(q, mem):
    bn, t, _ = q.shape
    n_mem = mem.shape[1]
    tq = min(t, 512)
    return pl.pallas_call(
        _cross_attn_kernel,
        grid=(bn, t // tq),
        in_specs=[pl.BlockSpec((1, tq, X_WIDTH), lambda b, i: (b, i, 0)),
                  pl.BlockSpec((1, n_mem, 2 * X_WIDTH), lambda b, i: (b, 0, 0))],
        out_specs=pl.BlockSpec((1, tq, X_WIDTH), lambda b, i: (b, i, 0)),
        out_shape=jax.ShapeDtypeStruct((bn, t, X_WIDTH), F32),
        compiler_params=_params("parallel", "arbitrary"),
        name="cross_attn",
    )(q, mem)


def _heads_out_ln_kernel(*refs):
    o_refs, (w_ref, h_ref, g_ref, b_ref, out_ref) = refs[:-5], refs[-5:]
    y = jnp.zeros(h_ref.shape[1:], F32)
    hh = 0
    for o_ref in o_refs:
        for h in range(o_ref.shape[1]):
            y = y + jnp.dot(o_ref[0, h].astype(BF16), w_ref[hh], preferred_element_type=F32)
            hh += 1
    out_ref[0] = _layer_norm_rows(ALPHA * h_ref[0] + y, g_ref[...], b_ref[...])


def _heads_out_ln(outs, w, h, g, b):
    bn, t, _ = h.shape
    tm = min(t, 512)
    return pl.pallas_call(
        _heads_out_ln_kernel,
        grid=(bn, t // tm),
        in_specs=[pl.BlockSpec((1, o.shape[1], tm, HEAD_DIM), lambda b, i: (b, 0, i, 0)) for o in outs]
        + [_resident(w.shape), pl.BlockSpec((1, tm, D_MODEL), lambda b, i: (b, i, 0)),
           _resident((1, D_MODEL)), _resident((1, D_MODEL))],
        out_specs=pl.BlockSpec((1, tm, D_MODEL), lambda b, i: (b, i, 0)),
        out_shape=jax.ShapeDtypeStruct((bn, t, D_MODEL), F32),
        compiler_params=_params("parallel", "parallel"),
        name="heads_out_ln",
    )(*outs, w, h, g.reshape(1, D_MODEL), b.reshape(1, D_MODEL))


QS = 8


def _own_tile(new_rows):
    return jnp.concatenate([new_rows, jnp.zeros((TK - new_rows.shape[0], new_rows.shape[1]), F32)], axis=0)


def _topk_threshold_dense(keys, pos, k, idx_bits):
    rows = keys.shape[1]
    count = lambda pred: jnp.sum(jnp.sum(jnp.where(pred, 1.0, 0.0), axis=0), axis=1, keepdims=True)

    def value_bits(it, ans):
        c1, c2, c3 = [ans + lax.shift_left(jnp.int32(m), 30 - 2 * it) for m in (1, 2, 3)]
        n1, n2, n3 = count(keys >= c1), count(keys >= c2), count(keys >= c3)
        return jnp.where(n3 >= k, c3, jnp.where(n2 >= k, c2, jnp.where(n1 >= k, c1, ans)))
    tau = lax.fori_loop(0, 16, value_bits, jnp.full((rows, 1), INT_MIN, jnp.int32))

    def cut_among_ties():
        need = k - count(keys > tau)
        tie_pos = jnp.where(keys == tau, pos, jnp.int32(2 ** 30))

        def index_bit(it, cut):
            cand = cut | lax.shift_left(jnp.int32(1), idx_bits - 1 - it)
            return jnp.where(count(tie_pos < cand) < need, cand, cut)
        return lax.fori_loop(0, idx_bits, index_bit, jnp.zeros((rows, 1), jnp.int32))

    cut = lax.cond(jnp.max(count(keys >= tau)) > k, cut_among_ties,
                   lambda: jnp.full((rows, 1), 2 ** 30, jnp.int32))
    return tau, cut


def _tile_attention(lg_ref, madd, band, n_h, vt_ref, own_v):
    nt = lg_ref.shape[0]
    lg = lg_ref[...].reshape(nt, n_h, QS, TK) + madd[:, None]
    far = lg[:nt - 2]
    near = lg[nt - 2:] + jnp.stack([band[:, :, 0:TK], band[:, :, TK:2 * TK]])
    m = jnp.maximum(jnp.max(jnp.max(far, axis=0), axis=-1, keepdims=True),
                    jnp.max(jnp.max(near, axis=0), axis=-1, keepdims=True))
    p_far = jnp.exp(far - m[None])
    p_near = jnp.exp(near - m[None])
    den = (jnp.sum(jnp.sum(p_far, axis=0), axis=-1, keepdims=True)
           + jnp.sum(jnp.sum(p_near, axis=0), axis=-1, keepdims=True))
    p = jnp.concatenate([p_far, p_near], axis=0).reshape(nt, n_h * QS, TK).astype(BF16)
    n_steps, _, width = vt_ref.shape
    pps = width // TK
    acc = jnp.dot(p[nt - 1], own_v, preferred_element_type=F32)
    for s in range(n_steps):
        chunk = jnp.concatenate([p[s * pps + c] for c in range(pps)], axis=1)
        acc = acc + lax.dot_general(chunk, vt_ref[s], NT, preferred_element_type=F32)
    return acc / den.reshape(n_h * QS, 1)


def _pages_per_step(n_pages):
    return next(c for c in (8, 4, 2, 1) if n_pages % c == 0)


def _page_specs(block, layer, n_pages, pps):
    tail = (0,) * (len(block) - 2)
    return [pl.BlockSpec(block, lambda b, p, tbl, k=k: (layer, tbl[b * n_pages + p * pps + k]) + tail)
            for k in range(pps)]


def _indexer_scores(qi, wi, s):
    s = s.reshape(A_IDX_HEADS, QS, s.shape[1])
    sc = jnp.zeros(s.shape[1:], F32)
    for h in range(A_IDX_HEADS):
        sc = sc + wi[:, h:h + 1] * jnp.maximum(s[h] * A_IDX_DIM ** -0.5, 0.0)
    return sc * A_IDX_HEADS ** -0.5


def _tile_positions(nt):
    tile = lax.broadcasted_iota(jnp.int32, (nt, QS, TK), 0)
    lane = lax.broadcasted_iota(jnp.int32, (nt, QS, TK), 2)
    row = lax.broadcasted_iota(jnp.int32, (nt, QS, TK), 1)
    pos = tile * TK + lane
    return pos, pos <= (nt - 1) * TK + row


def _dsa_sample_kernel(tbl_ref, q_ref, *refs):
    pps = len(refs) - 5
    page_refs, (band_ref, o_ref, key_scr, lg_scr, vt_scr) = refs[:pps], refs[pps:]
    p = pl.program_id(1)
    nt = key_scr.shape[0]
    x = q_ref[0]
    qi = _heads_to_rows(x[:, COL_QI:COL_QI + A_IDX_HEADS * A_IDX_DIM].astype(BF16), A_IDX_HEADS)
    wi = x[:, COL_A + MISC_WI:COL_A + MISC_WI + A_IDX_HEADS]
    qh = _heads_to_rows((x[:, COL_QA:COL_QA + A_MIX] * HEAD_DIM ** -0.5).astype(BF16), A_HEADS)
    for c, page_ref in enumerate(page_refs):
        lp = p * pps + c
        key_scr[lp] = _ordered_key(_indexer_scores(
            qi, wi, jnp.dot(qi, page_ref[0, 0, 2].astype(BF16), preferred_element_type=F32)))
        lg_scr[lp] = jnp.dot(qh, page_ref[0, 0, 0].astype(BF16), preferred_element_type=F32)
        vt_scr[p, :, c * TK:(c + 1) * TK] = page_ref[0, 0, 1].astype(BF16)

    @pl.when(p == pl.num_programs(1) - 1)
    def _():
        own = _own_tile(x[:, COL_A:COL_A + 3 * HEAD_DIM]).astype(BF16)
        pos, causal = _tile_positions(nt)
        sc = _indexer_scores(qi, wi, lax.dot_general(qi, own[:, 2 * HEAD_DIM:3 * HEAD_DIM], NT,
                                                     preferred_element_type=F32))
        key_scr[nt - 1] = _ordered_key(jnp.where(causal[nt - 1], sc, NEG))
        lg_scr[nt - 1] = lax.dot_general(qh, own[:, 0:HEAD_DIM], NT, preferred_element_type=F32)
        keys = key_scr[...]
        tau, cut = _topk_threshold_dense(keys, pos, float(A_TOPK), DSA_IDX_BITS)
        sel = ((keys > tau) | ((keys == tau) & (pos <= cut))) & causal
        o = _tile_attention(lg_scr, jnp.where(sel, 0.0, NEG), band_ref[:, 0:QS, :], A_HEADS, vt_scr,
                            own[:, HEAD_DIM:2 * HEAD_DIM])
        o_ref[0] = o.reshape(A_HEADS, QS, HEAD_DIM)


def _dsa_sample(proj, pool_t, page_table, layer, band_a):
    ns = proj.shape[0]
    n_pages = page_table.shape[1]
    nt = n_pages + 1
    pps = _pages_per_step(n_pages)
    grid_spec = pltpu.PrefetchScalarGridSpec(
        num_scalar_prefetch=1, grid=(ns, n_pages // pps),
        in_specs=[pl.BlockSpec((1, QS, EVEN_COLS), lambda b, p, tbl: (b, 0, 0))]
        + _page_specs((1, 1, 3, HEAD_DIM, PAGE_SIZE), layer, n_pages, pps)
        + [pl.BlockSpec((A_HEADS, TK, 2 * TK), lambda b, p, tbl: (0, 0, 0))],
        out_specs=pl.BlockSpec((1, A_HEADS, QS, HEAD_DIM), lambda b, p, tbl: (b, 0, 0, 0)),
        scratch_shapes=[pltpu.VMEM((nt, QS, TK), jnp.int32), pltpu.VMEM((nt, A_HEADS * QS, TK), F32),
                        pltpu.VMEM((n_pages // pps, HEAD_DIM, pps * TK), BF16)])
    return pl.pallas_call(
        _dsa_sample_kernel, grid_spec=grid_spec,
        out_shape=jax.ShapeDtypeStruct((ns, A_HEADS, QS, HEAD_DIM), F32),
        compiler_params=_params("parallel", "arbitrary"),
        name="dsa_sample",
    )(page_table.reshape(-1), proj, *([pool_t] * pps), band_a)


def _nsa_sample_kernel(tbl_ref, q_ref, *refs):
    pps = len(refs) - 11
    page_refs = refs[:pps]
    (win_ref, pos_ref, w1_ref, w2_ref, band_ref, cband_ref, expand_ref,
     o_ref, lg_scr, vt_scr, pool_scr) = refs[pps:]
    p = pl.program_id(1)
    nt = lg_scr.shape[1]
    nbl = pool_scr.shape[-1]
    x = q_ref[0]
    q = (x[:, COL_QB:COL_QB + B_MIX] * HEAD_DIM ** -0.5).astype(BF16)
    qg = [_heads_to_rows(q[:, g * B_R * HEAD_DIM:(g + 1) * B_R * HEAD_DIM], B_R) for g in range(B_KV_HEADS)]

    @pl.when(p == 0)
    def _():
        pool_scr[...] = jnp.zeros(pool_scr.shape, F32)

    ptok = lax.broadcasted_iota(jnp.int32, (TK, nbl), 0)
    pblk = lax.broadcasted_iota(jnp.int32, (TK, nbl), 1)
    for c, page_ref in enumerate(page_refs):
        lp = p * pps + c
        place = jnp.where(pblk == 2 * lp + ptok // B_BLOCK, 1.0, 0.0).astype(BF16)
        for g in range(B_KV_HEADS):
            for s in range(2):
                weighted = (page_ref[0, 0, s, g] * pos_ref[s]).astype(BF16)
                pool_scr[s, g] += jnp.dot(weighted, place, preferred_element_type=F32)
            lg_scr[g, lp] = jnp.dot(qg[g], page_ref[0, 0, 2, g].astype(BF16), preferred_element_type=F32)
            vt_scr[g, p, :, c * TK:(c + 1) * TK] = page_ref[0, 0, 3, g].astype(BF16)

    @pl.when(p == pl.num_programs(1) - 1)
    def _():
        past = (nt - 1) * TK
        n_blk = past // B_BLOCK
        nb_w = expand_ref.shape[0]
        own = _own_tile(x[:, COL_B + 2 * G_COLS:COL_B + 4 * G_COLS]).astype(BF16)
        gates = jax.nn.sigmoid(x[:, COL_A + MISC_GL:COL_A + MISC_GL + 3 * B_HEADS])
        blk = lax.broadcasted_iota(jnp.int32, (QS, nb_w), 1)
        tpos = past + lax.broadcasted_iota(jnp.int32, (QS, nb_w), 0)
        cur = tpos // B_BLOCK
        valid_c = ((blk + 1) * B_BLOCK - 1 <= tpos)[:, :nbl]
        forced = jnp.where((blk == 0) | (blk == cur) | (blk == cur - 1), FORCE_SCORE, 0.0)
        i_own = past // TK
        _, causal = _tile_positions(nt)
        own_win = _own_tile(x[:, COL_WIN:COL_WIN + 2 * G_COLS]).astype(BF16)
        o_cmps, imps = [], []
        for g in range(B_KV_HEADS):
            cmp_t = []
            for s in range(2):
                hid = jax.nn.gelu(jnp.dot(w1_ref[s], pool_scr[s, g].astype(BF16), preferred_element_type=F32))
                cmp_t.append(jnp.dot(w2_ref[s], hid.astype(BF16), preferred_element_type=F32).astype(BF16))
            lc = jnp.dot(qg[g], cmp_t[0], preferred_element_type=F32).reshape(B_R, QS, nbl)
            bias = []
            for r in range(B_R):
                cb = cband_ref[g * B_R + r][0:QS, :]
                b = jnp.zeros((QS, nbl), F32)
                for u in range(4):
                    b = jnp.where(blk[:, :nbl] == 2 * i_own + u - 2, cb[:, u:u + 1], b)
                bias.append(b)
            lc = jnp.where(valid_c[None], lc + jnp.stack(bias), NEG).reshape(B_R * QS, nbl)
            (pr,), den, _ = _softmax_pieces([lc])
            pc = (pr / den).reshape(B_R, QS, nbl) * jnp.where(valid_c, 1.0, 0.0)[None]
            o_cmps.append(lax.dot_general(pc.reshape(B_R * QS, nbl).astype(BF16), cmp_t[1], NT,
                                          preferred_element_type=F32))
            imp = jnp.concatenate([jnp.sum(pc, axis=0), jnp.zeros((QS, nb_w - nbl), F32)], axis=1)
            imps.append(jnp.where(blk <= cur, imp + forced, jnp.where(blk <= n_blk, NEG, -jnp.inf)))
        selm = _select_blocks(jnp.concatenate(imps, axis=0), B_TOPN)
        for g in range(B_KV_HEADS):
            gc = slice(g * HEAD_DIM, (g + 1) * HEAD_DIM)
            o_cmp = o_cmps[g]
            sel = jnp.dot(selm[g * QS:(g + 1) * QS].astype(BF16), expand_ref[...],
                          preferred_element_type=F32)
            madd = jnp.stack([jnp.where((sel[:, t * TK:(t + 1) * TK] > 0.5) & causal[t], 0.0, NEG)
                              for t in range(nt)])
            lg_scr[g, nt - 1] = lax.dot_general(qg[g], own[:, gc], NT, preferred_element_type=F32)
            o_sel = _tile_attention(lg_scr.at[g], madd, band_ref[g * B_R:(g + 1) * B_R, 0:QS, :], B_R, vt_scr.at[g],
                                    own[:, G_COLS + g * HEAD_DIM:G_COLS + (g + 1) * HEAD_DIM])
            tiles = [(own_win[:, gc], own_win[:, G_COLS + g * HEAD_DIM:G_COLS + (g + 1) * HEAD_DIM], True)]
            for d in range(1, NSA_WIN_TILES + 1):
                wr = slice((NSA_WIN_TILES - d) * TK, (NSA_WIN_TILES - d + 1) * TK)
                tiles.append((win_ref[0, wr, gc].astype(BF16),
                              win_ref[0, wr, G_COLS + g * HEAD_DIM:G_COLS + (g + 1) * HEAD_DIM].astype(BF16), True))
            o_win = _window_attend(qg[g], B_R, QS, tiles, band_ref, g * B_R)
            for r in range(B_R):
                h = g * B_R + r
                rr = slice(r * QS, (r + 1) * QS)
                o_ref[0, h] = (gates[:, 3 * h:3 * h + 1] * o_cmp[rr] + gates[:, 3 * h + 1:3 * h + 2] * o_sel[rr]
                               + gates[:, 3 * h + 2:3 * h + 3] * o_win[rr])


def _round_up(n, m):
    return -(-n // m) * m


def _sample_expand(n_pages):
    nb_w = _round_up(2 * n_pages + 1, LANES)
    tok_blk = np.arange((n_pages + 1) * TK) // B_BLOCK
    return jnp.asarray(np.arange(nb_w)[:, None] == tok_blk[None, :], BF16)


def _compress_weights_t(cmp_pos, cmp_w1, cmp_w2):
    pos_t = jnp.concatenate([jnp.swapaxes(cmp_pos, 1, 2)] * (PAGE_SIZE // B_BLOCK), axis=2)
    return pos_t, jnp.swapaxes(cmp_w1, 1, 2).astype(BF16), jnp.swapaxes(cmp_w2, 1, 2).astype(BF16)


def _nsa_sample(proj, pool_t, win, page_table, layer, cmp_w, band_b, cband_b, expand):
    ns = proj.shape[0]
    n_pages = page_table.shape[1]
    nt = n_pages + 1
    nbl = _round_up(2 * n_pages, LANES)
    const = lambda shape: pl.BlockSpec(shape, lambda b, p, tbl: (0,) * len(shape))
    pps = _pages_per_step(n_pages)
    grid_spec = pltpu.PrefetchScalarGridSpec(
        num_scalar_prefetch=1, grid=(ns, n_pages // pps),
        in_specs=[pl.BlockSpec((1, QS, EVEN_COLS), lambda b, p, tbl: (b, 0, 0))]
        + _page_specs((1, 1, 4, B_KV_HEADS, HEAD_DIM, PAGE_SIZE), layer, n_pages, pps)
        + [pl.BlockSpec((1, B_WINDOW, 2 * G_COLS), lambda b, p, tbl: (b, 0, 0)),
           const((2, HEAD_DIM, PAGE_SIZE)), const((2, HEAD_DIM, HEAD_DIM)), const((2, HEAD_DIM, HEAD_DIM)),
           const((B_HEADS, TK, 2 * TK)), const((B_HEADS, TK, 4)), const(expand.shape)],
        out_specs=pl.BlockSpec((1, B_HEADS, QS, HEAD_DIM), lambda b, p, tbl: (b, 0, 0, 0)),
        scratch_shapes=[pltpu.VMEM((B_KV_HEADS, nt, B_R * QS, TK), F32),
                        pltpu.VMEM((B_KV_HEADS, n_pages // pps, HEAD_DIM, pps * TK), BF16),
                        pltpu.VMEM((2, B_KV_HEADS, HEAD_DIM, nbl), F32)])
    return pl.pallas_call(
        _nsa_sample_kernel, grid_spec=grid_spec,
        out_shape=jax.ShapeDtypeStruct((ns, B_HEADS, QS, HEAD_DIM), F32),
        compiler_params=_params("parallel", "arbitrary"),
        name="nsa_sample",
    )(page_table.reshape(-1), proj, *([pool_t] * pps), win, *cmp_w, band_b, cband_b, expand)


def _swa_sample_kernel(q_ref, buf_ref, band_ref, sink_ref, o_ref):
    x = q_ref[0]
    q = (x[:, COL_QC:COL_QC + C_MIX] * HEAD_DIM ** -0.5).astype(BF16)
    own = _own_tile(x[:, COL_CWIN:COL_CWIN + 2 * G_COLS]).astype(BF16)
    for g in range(C_KV_HEADS):
        gc = slice(g * HEAD_DIM, (g + 1) * HEAD_DIM)
        vc = slice(G_COLS + g * HEAD_DIM, G_COLS + (g + 1) * HEAD_DIM)
        qg = _heads_to_rows(q[:, g * C_R * HEAD_DIM:(g + 1) * C_R * HEAD_DIM], C_R)
        tiles = [(own[:, gc], own[:, vc], True),
                 (buf_ref[0, :, gc].astype(BF16), buf_ref[0, :, vc].astype(BF16), True)]
        sink = jnp.concatenate([jnp.broadcast_to(sink_ref[g * C_R + r], (QS, 1)) for r in range(C_R)], axis=0)
        o = _window_attend(qg, C_R, QS, tiles, band_ref, g * C_R, sink)
        o_ref[0, g * C_R:(g + 1) * C_R] = o.reshape(C_R, QS, HEAD_DIM)


def _swa_sample(proj, buf, band_c, sinks):
    ns = proj.shape[0]
    return pl.pallas_call(
        _swa_sample_kernel,
        grid=(ns,),
        in_specs=[pl.BlockSpec((1, QS, C_MIX + 2 * G_COLS), lambda b: (b, 0, 0)),
                  pl.BlockSpec((1, C_WINDOW, 2 * G_COLS), lambda b: (b, 0, 0)),
                  _resident((C_HEADS, TK, 2 * TK)), _resident((C_HEADS, 1, 1))],
        out_specs=pl.BlockSpec((1, C_HEADS, QS, HEAD_DIM), lambda b: (b, 0, 0, 0)),
        out_shape=jax.ShapeDtypeStruct((ns, C_HEADS, QS, HEAD_DIM), F32),
        compiler_params=_params("parallel"),
        name="swa_sample",
    )(proj, buf, band_c, sinks)


def _trunk_prompt(x, mem_kv, w):
    bn, t, _ = x.shape
    assert t % TK == 0 and t >= 4 * A_TOPK
    n = bn * t
    a_rows, b_rows, b_states, c_states = [], [], [], []
    kb, kc = min(B_WINDOW, t), min(C_WINDOW, t)
    h = x.reshape(n, D_MODEL)
    for l in range(DEPTH):
        g, bb = w['ln_g'][l], w['ln_b'][l]
        h = _ffn_ln(h, w['ffn_wg'][l, 0], w['ffn_wu'][l, 0], w['ffn_wd'][l, 0], g[0], bb[0])
        if l % 2 == 0:
            e = l // 2
            proj = _linear(h, w['even_w_in_perm'][e]).reshape(bn, t, EVEN_COLS)
            cmp_kv = _compress_prompt(proj, *w['cmp'][e])
            o_a = _dsa_prompt(proj, w['band'][:A_HEADS])
            o_b = _nsa_prompt(proj, cmp_kv, w['band'][A_HEADS:], w['cband'][A_HEADS:])
            h = _heads_out_ln([o_a, o_b], w['even_w_out_heads'][e], h.reshape(bn, t, D_MODEL), g[1], bb[1])
            a_rows.append(proj[:, :, COL_A:COL_A + 3 * HEAD_DIM].reshape(bn, t, 3, HEAD_DIM))
            b_rows.append(proj[:, :, COL_B:COL_B + 4 * G_COLS].reshape(bn, t, 4, B_KV_HEADS, HEAD_DIM))
            b_states.append(proj[:, t - kb:, COL_WIN:COL_WIN + 2 * G_COLS].reshape(bn, kb, 2, B_KV_HEADS, HEAD_DIM))
        else:
            oi = l // 2
            proj = _linear(h, w['odd_w_in'][oi]).reshape(bn, t, C_MIX + 2 * G_COLS)
            o_c = _swa_prompt(proj, w['band'], w['sinks_shifted'][oi])
            h = _heads_out_ln([o_c], w['odd_w_out_heads'][oi], h.reshape(bn, t, D_MODEL), g[1], bb[1])
            c_states.append(proj[:, t - kc:, COL_CWIN:].reshape(bn, kc, 2, C_KV_HEADS, HEAD_DIM))
        h = h.reshape(n, D_MODEL)
        xq = _linear(h, w['x_wq'][l]).reshape(bn, t, X_WIDTH)
        xo = _cross_attn(xq, mem_kv[l].reshape(bn, -1, 2 * X_WIDTH)).reshape(n, X_WIDTH)
        h = _linear_res_ln(xo, w['x_wo'][l], h, g[2], bb[2])
        h = _ffn_ln(h, w['ffn_wg'][l, 1], w['ffn_wu'][l, 1], w['ffn_wd'][l, 1], g[3], bb[3])
    return (h.reshape(bn, t, D_MODEL), jnp.stack(a_rows), jnp.stack(b_rows), jnp.stack(b_states),
            jnp.stack(c_states))


def _shift_in(buf, new):
    return jnp.concatenate([buf, new], axis=1)[:, new.shape[1]:]


def _trunk_sample(x, mem_kv, a_pool, b_pool, b_win, c_win, page_table, w):
    ns, t, _ = x.shape
    assert t <= QS and b_win.shape[2] == B_WINDOW and c_win.shape[2] == C_WINDOW
    assert page_table.shape[1] * PAGE_SIZE >= max(4 * A_TOPK, B_TOPN * B_BLOCK, 3 * TK)
    n = ns * QS
    n_pages = page_table.shape[1]
    a_pool = jnp.transpose(a_pool, (0, 1, 3, 4, 2))
    b_pool = jnp.transpose(b_pool, (0, 1, 3, 4, 5, 2))
    expand = _sample_expand(n_pages)
    a_rows, b_rows, b_states, c_states = [], [], [], []
    h = jnp.pad(x, ((0, 0), (0, QS - t), (0, 0))).reshape(n, D_MODEL)
    for l in range(DEPTH):
        g, bb = w['ln_g'][l], w['ln_b'][l]
        h = _ffn_ln(h, w['ffn_wg'][l, 0], w['ffn_wu'][l, 0], w['ffn_wd'][l, 0], g[0], bb[0])
        if l % 2 == 0:
            e = l // 2
            proj = _linear(h, w['even_w_in_perm'][e]).reshape(ns, QS, EVEN_COLS)
            win = b_win[e].reshape(ns, B_WINDOW, 2 * G_COLS)
            o_a = _dsa_sample(proj, a_pool, page_table, e, w['band'][:A_HEADS])
            o_b = _nsa_sample(proj, b_pool, win, page_table, e, w['cmp_t'][e], w['band'][A_HEADS:],
                              w['cband'][A_HEADS:], expand)
            h = _heads_out_ln([o_a, o_b], w['even_w_out_heads'][e], h.reshape(ns, QS, D_MODEL), g[1], bb[1])
            a_rows.append(proj[:, :t, COL_A:COL_A + 3 * HEAD_DIM].reshape(ns, t, 3, HEAD_DIM))
            b_rows.append(proj[:, :t, COL_B:COL_B + 4 * G_COLS].reshape(ns, t, 4, B_KV_HEADS, HEAD_DIM))
            b_states.append(_shift_in(win, proj[:, :t, COL_WIN:COL_WIN + 2 * G_COLS]).reshape(b_win.shape[1:]))
        else:
            oi = l // 2
            proj = _linear(h, w['odd_w_in'][oi]).reshape(ns, QS, C_MIX + 2 * G_COLS)
            buf = c_win[oi].reshape(ns, C_WINDOW, 2 * G_COLS)
            o_c = _swa_sample(proj, buf, w['band'], w['sinks_shifted'][oi])
            h = _heads_out_ln([o_c], w['odd_w_out_heads'][oi], h.reshape(ns, QS, D_MODEL), g[1], bb[1])
            c_states.append(_shift_in(buf, proj[:, :t, COL_CWIN:]).reshape(c_win.shape[1:]))
        h = h.reshape(n, D_MODEL)
        xq = _linear(h, w['x_wq'][l]).reshape(ns, QS, X_WIDTH)
        xo = _cross_attn(xq, mem_kv[l].reshape(ns, -1, 2 * X_WIDTH)).reshape(n, X_WIDTH)
        h = _linear_res_ln(xo, w['x_wo'][l], h, g[2], bb[2])
        h = _ffn_ln(h, w['ffn_wg'][l, 1], w['ffn_wu'][l, 1], w['ffn_wd'][l, 1], g[3], bb[3])
    return (h.reshape(ns, QS, D_MODEL)[:, :t], jnp.stack(a_rows), jnp.stack(b_rows), jnp.stack(b_states),
            jnp.stack(c_states))


def kernel(x_prompt, x_sample, cache_a_kv, cache_b_kv, cache_b_win, cache_c_win, cache_mem_kv, page_table,
           mem_prompt, ln_g, ln_b, ffn_wg, ffn_wu, ffn_wd, even_w_in, even_w_out, nsa_cmp_pos, nsa_cmp_w1,
           nsa_cmp_w2, odd_w_in, odd_w_out, c_sinks, x_wq, x_wk, x_wv, x_wo, rel_table):
    nb, n_mem = mem_prompt.shape[:2]
    dist_bias = _distance_bias(rel_table)
    w = dict(ln_g=ln_g, ln_b=ln_b, ffn_wg=ffn_wg.astype(BF16), ffn_wu=ffn_wu.astype(BF16),
             ffn_wd=ffn_wd.astype(BF16), odd_w_in=odd_w_in.astype(BF16),
             x_wq=x_wq.astype(BF16), x_wo=x_wo.astype(BF16),
             even_w_in_perm=_permute_even_w_in(even_w_in).astype(BF16),
             even_w_out_heads=even_w_out.astype(BF16).reshape(-1, A_HEADS + B_HEADS, HEAD_DIM, D_MODEL),
             odd_w_out_heads=odd_w_out.astype(BF16).reshape(-1, C_HEADS, HEAD_DIM, D_MODEL),
             band=_band(dist_bias), cband=_cmp_band(dist_bias),
             cmp=[_compress_weights(nsa_cmp_pos[e], nsa_cmp_w1[e], nsa_cmp_w2[e]) for e in range(even_w_in.shape[0])],
             cmp_t=[_compress_weights_t(nsa_cmp_pos[e], nsa_cmp_w1[e], nsa_cmp_w2[e])
                    for e in range(even_w_in.shape[0])],
             sinks_shifted=(c_sinks - rel_table[N_BUCKETS - 1][None, :]).reshape(-1, C_HEADS, 1, 1))
    w_kv = jnp.concatenate([x_wk, x_wv], axis=-1).astype(BF16)
    mem2d = mem_prompt.reshape(nb * n_mem, D_MODEL)
    mem_kv_prompt = jnp.stack([_linear(mem2d, w_kv[l]) for l in range(DEPTH)])
    mem_kv_prompt = mem_kv_prompt.reshape(DEPTH, nb, n_mem, 2, X_HEADS, X_HEAD_DIM)
    y_p, a_p, b_p, bw_p, cw_p = _trunk_prompt(x_prompt, mem_kv_prompt, w)
    y_s, a_s, b_s, bw_s, cw_s = _trunk_sample(x_sample, cache_mem_kv, cache_a_kv, cache_b_kv, cache_b_win,
                                              cache_c_win, page_table, w)
    return (y_p, y_s, a_p, a_s, b_p, b_s, bw_p, bw_s, cw_p, cw_s, mem_kv_prompt)
```

```python
import functools
import math

import jax
import jax.numpy as jnp
import numpy as np
from jax import lax
from jax.experimental import pallas as pl
from jax.experimental.pallas import tpu as pltpu

D_MODEL = 1024
DEPTH = 4
PAGE_SIZE = 128
HEAD_DIM = 64
A_HEADS = 8
A_IDX_HEADS = 4
A_IDX_DIM = 64
A_TOPK = 256
B_HEADS = 8
B_KV_HEADS = 2
B_BLOCK = 64
B_TOPN = 16
B_WINDOW = 512
C_HEADS = 16
C_KV_HEADS = 2
C_WINDOW = 128
X_HEADS = 4
X_HEAD_DIM = 128
D_FF = 2816
N_BUCKETS = 32
BUCKET_MAX_DIST = 128
QBLOCK = 128
LN_EPS = 1e-5
ALPHA = (2 * DEPTH) ** 0.25
NEG = -1e30
FORCE_SCORE = 1e4
EVEN_SPLITS = [A_HEADS * HEAD_DIM, HEAD_DIM, HEAD_DIM, A_IDX_HEADS * A_IDX_DIM, A_IDX_DIM, A_IDX_HEADS,
               B_HEADS * HEAD_DIM] + [B_KV_HEADS * HEAD_DIM] * 6 + [B_HEADS * 3]
ODD_SPLITS = [C_HEADS * HEAD_DIM, C_KV_HEADS * HEAD_DIM, C_KV_HEADS * HEAD_DIM]
A_MIX = A_HEADS * HEAD_DIM
B_MIX = B_HEADS * HEAD_DIM
C_MIX = C_HEADS * HEAD_DIM
X_WIDTH = X_HEADS * X_HEAD_DIM

LANES = 128
VMEM_LIMIT = 56 << 20
BF16 = jnp.bfloat16
F32 = jnp.float32


def _params(*sem):
    return pltpu.CompilerParams(dimension_semantics=sem, vmem_limit_bytes=VMEM_LIMIT)


def _resident(shape):
    return pl.BlockSpec(shape, lambda *_: (0,) * len(shape), pipeline_mode=pl.Buffered(1))


def _per_batch(shape, col_block):
    return pl.BlockSpec(shape, lambda b, i: (b, 0, col_block), pipeline_mode=pl.Buffered(1))


def _row_tile(n):
    return min(n, 512)


def _layer_norm_rows(y, g, b):
    mu = jnp.mean(y, axis=-1, keepdims=True)
    d = y - mu
    var = jnp.mean(d * d, axis=-1, keepdims=True)
    return d * lax.rsqrt(var + LN_EPS) * g + b


FF_CHUNKS = 2
FF_CHUNK = D_FF // FF_CHUNKS


def _ffn_ln_kernel(x_ref, wg_ref, wu_ref, wd_ref, g_ref, b_ref, o_ref):
    x = x_ref[...]
    xb = x.astype(BF16)
    acc = jnp.zeros(x.shape, F32)
    for c in range(FF_CHUNKS):
        cols = slice(c * FF_CHUNK, (c + 1) * FF_CHUNK)
        gate = jnp.dot(xb, wg_ref[:, cols], preferred_element_type=F32)
        up = jnp.dot(xb, wu_ref[:, cols], preferred_element_type=F32)
        act = (gate * jax.nn.sigmoid(gate) * up).astype(BF16)
        acc = acc + jnp.dot(act, wd_ref[cols, :], preferred_element_type=F32)
    o_ref[...] = _layer_norm_rows(ALPHA * x + 0.5 * acc, g_ref[...], b_ref[...])


def _ffn_ln(x, wg, wu, wd, g, b):
    n = x.shape[0]
    tm = _row_tile(n)
    return pl.pallas_call(
        _ffn_ln_kernel,
        grid=(n // tm,),
        in_specs=[pl.BlockSpec((tm, D_MODEL), lambda i: (i, 0)),
                  _resident((D_MODEL, D_FF)), _resident((D_MODEL, D_FF)), _resident((D_FF, D_MODEL)),
                  _resident((1, D_MODEL)), _resident((1, D_MODEL))],
        out_specs=pl.BlockSpec((tm, D_MODEL), lambda i: (i, 0)),
        out_shape=jax.ShapeDtypeStruct((n, D_MODEL), F32),
        compiler_params=_params("parallel"),
        name="ffn_ln",
    )(x, wg, wu, wd, g.reshape(1, D_MODEL), b.reshape(1, D_MODEL))


def _linear_kernel(x_ref, w_ref, o_ref):
    o_ref[...] = jnp.dot(x_ref[...].astype(BF16), w_ref[...], preferred_element_type=F32)


def _linear(x, w):
    n, k = x.shape
    m = w.shape[1]
    tm = _row_tile(n)
    return pl.pallas_call(
        _linear_kernel,
        grid=(n // tm,),
        in_specs=[pl.BlockSpec((tm, k), lambda i: (i, 0)), _resident((k, m))],
        out_specs=pl.BlockSpec((tm, m), lambda i: (i, 0)),
        out_shape=jax.ShapeDtypeStruct((n, m), F32),
        compiler_params=_params("parallel"),
        name="linear",
    )(x, w)


def _linear_res_ln_kernel(x_ref, w_ref, h_ref, g_ref, b_ref, o_ref):
    y = jnp.dot(x_ref[...].astype(BF16), w_ref[...], preferred_element_type=F32)
    o_ref[...] = _layer_norm_rows(ALPHA * h_ref[...] + y, g_ref[...], b_ref[...])


def _linear_res_ln(x, w, h, g, b):
    n, k = x.shape
    tm = _row_tile(n)
    return pl.pallas_call(
        _linear_res_ln_kernel,
        grid=(n // tm,),
        in_specs=[pl.BlockSpec((tm, k), lambda i: (i, 0)), _resident((k, D_MODEL)),
                  pl.BlockSpec((tm, D_MODEL), lambda i: (i, 0)),
                  _resident((1, D_MODEL)), _resident((1, D_MODEL))],
        out_specs=pl.BlockSpec((tm, D_MODEL), lambda i: (i, 0)),
        out_shape=jax.ShapeDtypeStruct((n, D_MODEL), F32),
        compiler_params=_params("parallel"),
        name="linear_res_ln",
    )(x, w, h, g.reshape(1, D_MODEL), b.reshape(1, D_MODEL))


def t5_bucket(dist):
    exact = N_BUCKETS // 2
    d = jnp.maximum(dist, 0)
    rel = jnp.log(jnp.maximum(d, 1).astype(jnp.float32) / exact) / math.log(BUCKET_MAX_DIST / exact)
    large = jnp.minimum(exact + (rel * (N_BUCKETS - exact)).astype(jnp.int32), N_BUCKETS - 1)
    return jnp.where(d < exact, d, large)


TK = 128
INT_MIN = -2 ** 31
NT = (((1,), (1,)), ((), ()))


def _ordered_key(x):
    bits = pltpu.bitcast(x, jnp.int32)
    key = jnp.where(bits < 0, bits ^ jnp.int32(0x7FFFFFFF), bits)
    return jnp.where(x == 0.0, 0, key)


def _count(key_scr, nkt, rows, pred):
    hit = lambda j: jnp.where(pred(key_scr[j], j), 1.0, 0.0)

    def pair(j2, c):
        return c + hit(2 * j2) + hit(2 * j2 + 1)
    c = lax.fori_loop(0, nkt // 2, pair, jnp.zeros((rows, TK), F32))
    c = c + jnp.where(nkt % 2 == 1, 1.0, 0.0) * hit(nkt - 1)
    return jnp.sum(c, axis=1, keepdims=True)


def _topk_threshold(key_scr, nkt, rows, k, idx_bits):
    lane = lax.broadcasted_iota(jnp.int32, (rows, TK), 1)

    def value_bit(it, ans):
        cand = ans + lax.shift_left(jnp.int32(1), 31 - it)
        cand_key = jnp.broadcast_to(cand, (rows, TK))
        cnt = _count(key_scr, nkt, rows, lambda t, j: t >= cand_key)
        return jnp.where(cnt >= k, cand, ans)
    tau = lax.fori_loop(0, 32, value_bit, jnp.full((rows, 1), INT_MIN, jnp.int32))
    n_ge = _count(key_scr, nkt, rows, lambda t, j: t >= tau)

    def cut_among_ties():
        need = k - _count(key_scr, nkt, rows, lambda t, j: t > tau)

        def index_bit(it, cut):
            cand = cut | lax.shift_left(jnp.int32(1), idx_bits - 1 - it)
            cnt = _count(key_scr, nkt, rows,
                         lambda t, j: jnp.where(t == tau, j * TK + lane, jnp.int32(2 ** 30)) < cand)
            return jnp.where(cnt < need, cand, cut)
        return lax.fori_loop(0, idx_bits, index_bit, jnp.zeros((rows, 1), jnp.int32))

    cut = lax.cond(jnp.max(n_ge) > k, cut_among_ties, lambda: jnp.full((rows, 1), 2 ** 30, jnp.int32))
    return tau, cut


def _in_topk(t, j, tau, cut):
    lane = lax.broadcasted_iota(jnp.int32, t.shape, 1)
    return (t > tau) | ((t == tau) & (j * TK + lane <= cut))


def _heads_to_rows(x, n_heads):
    return jnp.concatenate([x[:, h * HEAD_DIM:(h + 1) * HEAD_DIM] for h in range(n_heads)], axis=0)


FAR_TILES = 4


def _flash_init(mx_scr, acc_scr):
    mx_scr[...] = jnp.full(mx_scr.shape, NEG, F32)
    acc_scr[...] = jnp.zeros(acc_scr.shape, F32)


def _with_ones(v):
    return jnp.concatenate([v, jnp.ones(v.shape, BF16)], axis=1)


def _masked_logits(q_rows, k, madd, bias):
    n_h = q_rows.shape[0] // TK
    lg = lax.dot_general(q_rows, k, NT, preferred_element_type=F32).reshape(n_h, TK, k.shape[0]) + madd[None]
    return lg if bias is None else lg + bias


def _max_step(mx_scr, heads, q_rows, k, madd, bias):
    lg = _masked_logits(q_rows, k, madd, bias)
    m = mx_scr[heads]
    for c in range(k.shape[0] // TK):
        m = jnp.maximum(m, lg[:, :, c * TK:(c + 1) * TK])
    mx_scr[heads] = m


def _max_finish(mx_scr):
    mx_scr[...] = jnp.broadcast_to(jnp.max(mx_scr[...], axis=-1, keepdims=True), mx_scr.shape)


def _acc_step(mx_scr, acc_scr, heads, q_rows, k, v1, madd, bias):
    lg = _masked_logits(q_rows, k, madd, bias)
    n_h, _, n = lg.shape
    p = jnp.exp(lg - jnp.concatenate([mx_scr[heads]] * (n // TK), axis=2)).astype(BF16)
    acc_scr[heads] += jnp.dot(p.reshape(n_h * TK, n), v1, preferred_element_type=F32).reshape(n_h, TK, 2 * HEAD_DIM)


def _flash_far_loop(n_far, attend):
    n_big = n_far // FAR_TILES

    def big(jb, carry):
        attend(jb * FAR_TILES, FAR_TILES)
        return carry
    lax.fori_loop(0, n_big, big, 0)

    def small(j, carry):
        attend(j, 1)
        return carry
    lax.fori_loop(n_big * FAR_TILES, n_far, small, 0)


def _two_pass_attention(i, mx_scr, attend):
    for accumulate in (False, True):
        _flash_far_loop(jnp.maximum(i - 1, 0), functools.partial(attend, accumulate))

        @pl.when(i >= 1)
        def _():
            attend(accumulate, i - 1, 1, slice(0, TK))
        attend(accumulate, i, 1, slice(TK, 2 * TK), True)
        if not accumulate:
            _max_finish(mx_scr)


def _flash_result(acc_scr, h):
    acc = acc_scr[h]
    return acc[:, 0:HEAD_DIM] / acc[:, HEAD_DIM:2 * HEAD_DIM]


def _window_attend(qg, n_r, qr, tiles, band_ref, head0, sink=None):
    row = lax.broadcasted_iota(jnp.int32, (qr, TK), 0)
    lane = lax.broadcasted_iota(jnp.int32, (qr, TK), 1)
    heads = slice(head0, head0 + n_r)
    pieces = []
    for d, (k, _, exists) in enumerate(tiles):
        lg = lax.dot_general(qg, k, NT, preferred_element_type=F32).reshape(n_r, qr, TK)
        ok = jnp.broadcast_to(exists, (qr, TK))
        if d == 0:
            ok = ok & (lane <= row)
            lg = lg + band_ref[heads, 0:qr, TK:2 * TK]
        if d == 1:
            lg = lg + band_ref[heads, 0:qr, 0:TK]
        if d == len(tiles) - 1:
            ok = ok & (lane > row)
        pieces.append(jnp.where(ok[None], lg, NEG).reshape(n_r * qr, TK))
    m = functools.reduce(jnp.maximum, [jnp.max(x, axis=1, keepdims=True) for x in pieces])
    if sink is not None:
        m = jnp.maximum(m, sink)
    ps = [jnp.exp(x - m) for x in pieces]
    den = functools.reduce(lambda a, b: a + b, [jnp.sum(p, axis=1, keepdims=True) for p in ps])
    if sink is not None:
        den = den + jnp.exp(sink - m)
    o = functools.reduce(lambda a, b: a + b,
                         [jnp.dot(p.astype(BF16), t[1], preferred_element_type=F32) for p, t in zip(ps, tiles)])
    return o / den


def _distance_bias(rel_table):
    tb = rel_table[t5_bucket(jnp.arange(2 * TK))]
    return tb - rel_table[N_BUCKETS - 1][None, :]


def _band(dist_bias):
    a = np.arange(TK)[:, None]
    c = np.arange(2 * TK)[None, :]
    idx = np.clip(a - c + TK, 0, 2 * TK - 1)
    return jnp.transpose(dist_bias[idx], (2, 0, 1))


def _dsa_prompt_kernel(q_ref, qi_ref, misc_ref, kv_ref, band_ref, o_ref, key_scr, m_scr, acc_scr):
    i = pl.program_id(1)
    nkt = i + 1
    row = lax.broadcasted_iota(jnp.int32, (TK, TK), 0)
    lane = lax.broadcasted_iota(jnp.int32, (TK, TK), 1)
    qi = qi_ref[0].astype(BF16)
    wi = [jnp.broadcast_to(misc_ref[0][:, MISC_WI + h:MISC_WI + h + 1], (TK, TK)) for h in range(A_IDX_HEADS)]

    def score_tiles(j0, n, causal=False):
        rows = pl.ds(pl.multiple_of(j0 * TK, TK), n * TK)
        kt = kv_ref[0, rows, 2 * HEAD_DIM:3 * HEAD_DIM].astype(BF16)
        sc = jnp.zeros((TK, n * TK), F32)
        for h in range(A_IDX_HEADS):
            s = lax.dot_general(qi[:, h * A_IDX_DIM:(h + 1) * A_IDX_DIM], kt, NT, preferred_element_type=F32)
            sc = sc + jnp.concatenate([wi[h]] * n, axis=1) * jnp.maximum(s * A_IDX_DIM ** -0.5, 0.0)
        sc = sc * A_IDX_HEADS ** -0.5
        if causal:
            sc = jnp.where(lane <= row, sc, NEG)
        for t in range(n):
            key_scr[j0 + t] = _ordered_key(sc[:, t * TK:(t + 1) * TK])
    _flash_far_loop(i, score_tiles)
    score_tiles(i, 1, True)
    tau, cut = _topk_threshold(key_scr, nkt, TK, float(A_TOPK), DSA_IDX_BITS)

    q = (q_ref[0] * HEAD_DIM ** -0.5).astype(BF16)
    qh = _heads_to_rows(q, A_HEADS)
    _flash_init(m_scr, acc_scr)

    def attend(accumulate, j0, n, bias_cols=None, causal=False):
        rows = pl.ds(pl.multiple_of(j0 * TK, TK), n * TK)
        k = kv_ref[0, rows, 0:HEAD_DIM].astype(BF16)
        sels = [_in_topk(key_scr[j0 + t], j0 + t, tau, cut) for t in range(n)]
        if causal:
            sels = [s & (lane <= row) for s in sels]
        madd = jnp.concatenate([jnp.where(s, 0.0, NEG) for s in sels], axis=1)
        bias = None if bias_cols is None else band_ref[:, :, bias_cols]
        if accumulate:
            v1 = _with_ones(kv_ref[0, rows, HEAD_DIM:2 * HEAD_DIM].astype(BF16))
            _acc_step(m_scr, acc_scr, slice(0, A_HEADS), qh, k, v1, madd, bias)
        else:
            _max_step(m_scr, slice(0, A_HEADS), qh, k, madd, bias)

    _two_pass_attention(i, m_scr, attend)
    for h in range(A_HEADS):
        o_ref[0, h] = _flash_result(acc_scr, h)


def _dsa_prompt(proj, band_a):
    bn, t, _ = proj.shape
    nqb = t // TK
    return pl.pallas_call(
        _dsa_prompt_kernel,
        grid=(bn, nqb),
        in_specs=[pl.BlockSpec((1, TK, A_MIX), lambda b, i: (b, i, COL_QA // A_MIX)),
                  pl.BlockSpec((1, TK, 256), lambda b, i: (b, i, COL_QI // 256)),
                  pl.BlockSpec((1, TK, 256), lambda b, i: (b, i, COL_A // 256)),
                  _per_batch((1, t, 256), COL_A // 256),
                  _resident((A_HEADS, TK, 2 * TK))],
        out_specs=pl.BlockSpec((1, A_HEADS, TK, HEAD_DIM), lambda b, i: (b, 0, i, 0)),
        out_shape=jax.ShapeDtypeStruct((bn, A_HEADS, t, HEAD_DIM), F32),
        scratch_shapes=[pltpu.VMEM((nqb, TK, TK), jnp.int32),
                        pltpu.VMEM((A_HEADS, TK, TK), F32), pltpu.VMEM((A_HEADS, TK, 2 * HEAD_DIM), F32)],
        compiler_params=_params("parallel", "arbitrary"),
        name="dsa_prompt",
    )(proj, proj, proj, proj, band_a)


COL_QA, COL_QB, COL_QI, COL_A = 0, 512, 1024, 1280
COL_B, COL_WIN, EVEN_COLS = 1536, 2048, 2304
MISC_WI = 3 * HEAD_DIM
MISC_GL = MISC_WI + A_IDX_HEADS
DSA_IDX_BITS = 14


def _even_in_perm():
    offs = np.concatenate([[0], np.cumsum(EVEN_SPLITS)])
    seg = lambda k: np.arange(offs[k], offs[k + 1])
    (q_a, k_a, v_a, qi, ki, wi, q_b, kc, vc, ksl, vsl, kw, vw, gl) = [seg(k) for k in range(14)]
    order = [q_a, q_b, qi, k_a, v_a, ki, wi, gl]
    pad = EVEN_COLS - sum(EVEN_SPLITS)
    return np.concatenate(order), pad, np.concatenate([kc, vc, ksl, vsl, kw, vw])


def _permute_even_w_in(w):
    head, pad, tail = _even_in_perm()
    zeros = jnp.zeros(w.shape[:-1] + (pad,), w.dtype)
    return jnp.concatenate([w[..., head], zeros, w[..., tail]], axis=-1)


G_COLS = B_KV_HEADS * HEAD_DIM


def _compress_rows(x, pos_w, w1, w2):
    nb = x.shape[0] // B_BLOCK
    pooled = jnp.sum(x.reshape(nb, B_BLOCK, G_COLS) * pos_w[None], axis=1)
    hid = jax.nn.gelu(jnp.dot(pooled.astype(BF16), w1, preferred_element_type=F32))
    return jnp.dot(hid.astype(BF16), w2, preferred_element_type=F32)


def _compress_kernel(kv_ref, pos_ref, w1_ref, w2_ref, o_ref):
    for s in range(2):
        x = kv_ref[0, :, s * G_COLS:(s + 1) * G_COLS]
        o_ref[0, :, s * G_COLS:(s + 1) * G_COLS] = _compress_rows(x, pos_ref[s], w1_ref[s], w2_ref[s])


def _compress_weights(cmp_pos, cmp_w1, cmp_w2):
    pos = jnp.concatenate([cmp_pos] * B_KV_HEADS, axis=-1)
    eye = jnp.eye(B_KV_HEADS, dtype=cmp_w1.dtype)
    bd = lambda w: jnp.einsum('gh,sde->sgdhe', eye, w).reshape(2, G_COLS, G_COLS).astype(BF16)
    return pos, bd(cmp_w1), bd(cmp_w2)


def _compress_prompt(proj, pos, w1, w2):
    bn, t, _ = proj.shape
    tr = min(t, 2048)
    return pl.pallas_call(
        _compress_kernel,
        grid=(bn, t // tr),
        in_specs=[pl.BlockSpec((1, tr, 2 * G_COLS), lambda b, i: (b, i, COL_B // (2 * G_COLS))),
                  _resident((2, B_BLOCK, G_COLS)), _resident((2, G_COLS, G_COLS)), _resident((2, G_COLS, G_COLS))],
        out_specs=pl.BlockSpec((1, tr // B_BLOCK, 2 * G_COLS), lambda b, i: (b, i, 0)),
        out_shape=jax.ShapeDtypeStruct((bn, t // B_BLOCK, 2 * G_COLS), F32),
        compiler_params=_params("parallel", "parallel"),
        name="nsa_compress",
    )(proj, pos, w1, w2)


B_R = B_HEADS // B_KV_HEADS
NSA_WIN_TILES = B_WINDOW // TK


def _softmax_pieces(pieces):
    m = functools.reduce(jnp.maximum, [jnp.max(x, axis=1, keepdims=True) for x in pieces])
    ps = [jnp.exp(x - m) for x in pieces]
    den = functools.reduce(lambda a, b: a + b, [jnp.sum(p, axis=1, keepdims=True) for p in ps])
    return ps, den, m


def _select_blocks(imp, n_sel, lanef=None):
    if lanef is None:
        lanef = lax.broadcasted_iota(jnp.int32, imp.shape, 1).astype(F32)
    sel = jnp.zeros(imp.shape, F32)
    for _ in range(n_sel):
        mx = jnp.max(imp, axis=1, keepdims=True)
        first = jnp.min(jnp.where(imp == mx, lanef, float(2 ** 24)), axis=1, keepdims=True)
        hit = lanef == first
        sel = jnp.where(hit, 1.0, sel)
        imp = jnp.where(hit, -jnp.inf, imp)
    return sel


def _nsa_prompt_kernel(q_ref, misc_ref, kv_ref, win_ref, cmp_ref, band_ref, cband_ref, o_ref,
                       selm_scr, m_scr, acc_scr):
    i = pl.program_id(1)
    n_blk = cmp_ref.shape[1]
    row = lax.broadcasted_iota(jnp.int32, (TK, TK), 0)
    lane = lax.broadcasted_iota(jnp.int32, (TK, TK), 1)
    tpos = i * TK + row
    q = (q_ref[0] * HEAD_DIM ** -0.5).astype(BF16)
    gates = jax.nn.sigmoid(misc_ref[0][:, MISC_GL:MISC_GL + 3 * B_HEADS])
    qg = [_heads_to_rows(q[:, g * B_R * HEAD_DIM:(g + 1) * B_R * HEAD_DIM], B_R) for g in range(B_KV_HEADS)]

    blk = lax.broadcasted_iota(jnp.int32, (TK, n_blk), 1)
    brow = lax.broadcasted_iota(jnp.int32, (TK, n_blk), 0)
    bpos = i * TK + brow
    valid_c = (blk + 1) * B_BLOCK - 1 <= bpos
    cur = bpos // B_BLOCK
    forced = jnp.where((blk == 0) | (blk == cur) | (blk == cur - 1), FORCE_SCORE, 0.0)
    o_cmp, imps = [], []
    for g in range(B_KV_HEADS):
        kc = cmp_ref[0, :, g * HEAD_DIM:(g + 1) * HEAD_DIM].astype(BF16)
        vc = cmp_ref[0, :, G_COLS + g * HEAD_DIM:G_COLS + (g + 1) * HEAD_DIM].astype(BF16)
        lc = lax.dot_general(qg[g], kc, NT, preferred_element_type=F32).reshape(B_R, TK, n_blk)
        bias = []
        for r in range(B_R):
            cb = cband_ref[g * B_R + r]
            b = jnp.zeros((TK, n_blk), F32)
            for u in range(4):
                b = jnp.where(blk == 2 * i + u - 2, cb[:, u:u + 1], b)
            bias.append(b)
        lc = jnp.where(valid_c[None], lc + jnp.stack(bias), NEG).reshape(B_R * TK, n_blk)
        (p,), den, _ = _softmax_pieces([lc])
        pc = ((p / den).reshape(B_R, TK, n_blk) * jnp.where(valid_c, 1.0, 0.0)[None])
        o_cmp.append(jnp.dot(pc.reshape(B_R * TK, n_blk).astype(BF16), vc, preferred_element_type=F32))
        imps.append(jnp.where(blk <= cur, jnp.sum(pc, axis=0) + forced, NEG))
    selm = _select_blocks(jnp.concatenate(imps, axis=0), min(B_TOPN, n_blk))
    for g in range(B_KV_HEADS):
        selm_scr[g] = selm[g * TK:(g + 1) * TK].astype(BF16)

    _flash_init(m_scr, acc_scr)

    def attend(accumulate, j0, n, bias_cols=None, causal=False):
        rows = pl.ds(pl.multiple_of(j0 * TK, TK), n * TK)
        ebk = lax.broadcasted_iota(jnp.int32, (n_blk, n * TK), 0)
        etok = j0 * TK + lax.broadcasted_iota(jnp.int32, (n_blk, n * TK), 1)
        expand = jnp.where(ebk == etok // B_BLOCK, 1.0, 0.0).astype(BF16)
        for g in range(B_KV_HEADS):
            k = kv_ref[0, rows, g * HEAD_DIM:(g + 1) * HEAD_DIM].astype(BF16)
            sel = jnp.dot(selm_scr[g], expand, preferred_element_type=F32) > 0.5
            if causal:
                sel = sel & (lane <= row)
            madd = jnp.where(sel, 0.0, NEG)
            heads = slice(g * B_R, (g + 1) * B_R)
            bias = None if bias_cols is None else band_ref[heads, :, bias_cols]
            if accumulate:
                v1 = _with_ones(kv_ref[0, rows, G_COLS + g * HEAD_DIM:G_COLS + (g + 1) * HEAD_DIM].astype(BF16))
                _acc_step(m_scr, acc_scr, heads, qg[g], k, v1, madd, bias)
            else:
                _max_step(m_scr, heads, qg[g], k, madd, bias)

    _two_pass_attention(i, m_scr, attend)

    for g in range(B_KV_HEADS):
        tiles = []
        for d in range(NSA_WIN_TILES + 1):
            rows = pl.ds(pl.multiple_of(jnp.maximum(i - d, 0) * TK, TK), TK)
            tiles.append((win_ref[0, rows, g * HEAD_DIM:(g + 1) * HEAD_DIM].astype(BF16),
                          win_ref[0, rows, G_COLS + g * HEAD_DIM:G_COLS + (g + 1) * HEAD_DIM].astype(BF16),
                          i - d >= 0))
        o_win = _window_attend(qg[g], B_R, TK, tiles, band_ref, g * B_R)
        for r in range(B_R):
            h = g * B_R + r
            rr = slice(r * TK, (r + 1) * TK)
            o_ref[0, h] = (gates[:, 3 * h:3 * h + 1] * o_cmp[g][rr]
                           + gates[:, 3 * h + 1:3 * h + 2] * _flash_result(acc_scr, h)
                           + gates[:, 3 * h + 2:3 * h + 3] * o_win[rr])


def _cmp_band(dist_bias):
    a = np.arange(TK)[:, None]
    u = np.arange(4)[None, :] - 2
    idx = np.clip(a - B_BLOCK * u - (B_BLOCK - 1), 0, 2 * TK - 1)
    return jnp.transpose(dist_bias[idx], (2, 0, 1))


def _nsa_prompt(proj, cmp_kv, band_b, cband_b):
    bn, t, _ = proj.shape
    nqb = t // TK
    n_blk = cmp_kv.shape[1]
    return pl.pallas_call(
        _nsa_prompt_kernel,
        grid=(bn, nqb),
        in_specs=[pl.BlockSpec((1, TK, B_MIX), lambda b, i: (b, i, COL_QB // B_MIX)),
                  pl.BlockSpec((1, TK, 256), lambda b, i: (b, i, COL_A // 256)),
                  _per_batch((1, t, 2 * G_COLS), (COL_B + 2 * G_COLS) // (2 * G_COLS)),
                  _per_batch((1, t, 2 * G_COLS), COL_WIN // (2 * G_COLS)),
                  _per_batch((1, n_blk, 2 * G_COLS), 0),
                  _resident((B_HEADS, TK, 2 * TK)), _resident((B_HEADS, TK, 4))],
        out_specs=pl.BlockSpec((1, B_HEADS, TK, HEAD_DIM), lambda b, i: (b, 0, i, 0)),
        out_shape=jax.ShapeDtypeStruct((bn, B_HEADS, t, HEAD_DIM), F32),
        scratch_shapes=[pltpu.VMEM((B_KV_HEADS, TK, n_blk), BF16),
                        pltpu.VMEM((B_HEADS, TK, TK), F32), pltpu.VMEM((B_HEADS, TK, 2 * HEAD_DIM), F32)],
        compiler_params=_params("parallel", "arbitrary"),
        name="nsa_prompt",
    )(proj, proj, proj, proj, cmp_kv, band_b, cband_b)


C_R = C_HEADS // C_KV_HEADS
COL_QC, COL_CWIN = 0, C_MIX


def _swa_prompt_kernel(q_ref, kv_ref, band_ref, sink_ref, o_ref):
    i = pl.program_id(1)
    row = lax.broadcasted_iota(jnp.int32, (TK, TK), 0)
    lane = lax.broadcasted_iota(jnp.int32, (TK, TK), 1)
    q = (q_ref[0] * HEAD_DIM ** -0.5).astype(BF16)
    prev_rows = pl.ds(pl.multiple_of(jnp.maximum(i - 1, 0) * TK, TK), TK)
    own_rows = pl.ds(pl.multiple_of(i * TK, TK), TK)
    prev_ok = jnp.broadcast_to(i >= 1, (TK, TK)) & (lane > row)
    for g in range(C_KV_HEADS):
        qg = _heads_to_rows(q[:, g * C_R * HEAD_DIM:(g + 1) * C_R * HEAD_DIM], C_R)
        pieces, vals = [], []
        for rows, ok, cols in ((prev_rows, prev_ok, slice(0, TK)), (own_rows, lane <= row, slice(TK, 2 * TK))):
            k = kv_ref[0, rows, g * HEAD_DIM:(g + 1) * HEAD_DIM].astype(BF16)
            vals.append(kv_ref[0, rows, G_COLS + g * HEAD_DIM:G_COLS + (g + 1) * HEAD_DIM].astype(BF16))
            lg = lax.dot_general(qg, k, NT, preferred_element_type=F32).reshape(C_R, TK, TK)
            lg = lg + band_ref[g * C_R:(g + 1) * C_R, :, cols]
            pieces.append(jnp.where(ok[None], lg, NEG).reshape(C_R * TK, TK))
        sink = jnp.concatenate([jnp.broadcast_to(sink_ref[g * C_R + r], (TK, 1)) for r in range(C_R)], axis=0)
        m = jnp.maximum(jnp.maximum(jnp.max(pieces[0], axis=1, keepdims=True),
                                    jnp.max(pieces[1], axis=1, keepdims=True)), sink)
        ps = [jnp.exp(x - m) for x in pieces]
        den = jnp.sum(ps[0], axis=1, keepdims=True) + jnp.sum(ps[1], axis=1, keepdims=True) + jnp.exp(sink - m)
        o = (jnp.dot(ps[0].astype(BF16), vals[0], preferred_element_type=F32)
             + jnp.dot(ps[1].astype(BF16), vals[1], preferred_element_type=F32)) / den
        o_ref[0, g * C_R:(g + 1) * C_R] = o.reshape(C_R, TK, HEAD_DIM)


def _swa_prompt(proj, band_c, sinks):
    bn, t, _ = proj.shape
    return pl.pallas_call(
        _swa_prompt_kernel,
        grid=(bn, t // TK),
        in_specs=[pl.BlockSpec((1, TK, C_MIX), lambda b, i: (b, i, COL_QC // C_MIX)),
                  _per_batch((1, t, 2 * G_COLS), COL_CWIN // (2 * G_COLS)),
                  _resident((C_HEADS, TK, 2 * TK)), _resident((C_HEADS, 1, 1))],
        out_specs=pl.BlockSpec((1, C_HEADS, TK, HEAD_DIM), lambda b, i: (b, 0, i, 0)),
        out_shape=jax.ShapeDtypeStruct((bn, C_HEADS, t, HEAD_DIM), F32),
        compiler_params=_params("parallel", "arbitrary"),
        name="swa_prompt",
    )(proj, proj, band_c, sinks)


def _cross_attn_kernel(q_ref, mem_ref, o_ref):
    q = (q_ref[0] * X_HEAD_DIM ** -0.5).astype(BF16)
    for h in range(X_HEADS):
        cols = slice(h * X_HEAD_DIM, (h + 1) * X_HEAD_DIM)
        k = mem_ref[0, :, cols].astype(BF16)
        v = mem_ref[0, :, X_WIDTH + h * X_HEAD_DIM:X_WIDTH + (h + 1) * X_HEAD_DIM].astype(BF16)
        s = lax.dot_general(q[:, cols], k, NT, preferred_element_type=F32)
        p = jnp.exp(s - jnp.max(s, axis=1, keepdims=True))
        den = jnp.sum(p, axis=1, keepdims=True)
        o_ref[0, :, cols] = jnp.dot(p.astype(BF16), v, preferred_element_type=F32) / den


def _cross_attn(q, mem):
    bn, t, _ = q.shape
    n_mem = mem.shape[1]
    tq = min(t, 512)
    return pl.pallas_call(
        _cross_attn_kernel,
        grid=(bn, t // tq),
        in_specs=[pl.BlockSpec((1, tq, X_WIDTH), lambda b, i: (b, i, 0)),
                  pl.BlockSpec((1, n_mem, 2 * X_WIDTH), lambda b, i: (b, 0, 0))],
        out_specs=pl.BlockSpec((1, tq, X_WIDTH), lambda b, i: (b, i, 0)),
        out_shape=jax.ShapeDtypeStruct((bn, t, X_WIDTH), F32),
        compiler_params=_params("parallel", "arbitrary"),
        name="cross_attn",
    )(q, mem)


def _heads_out_ln_kernel(*refs):
    o_refs, (w_ref, h_ref, g_ref, b_ref, out_ref) = refs[:-5], refs[-5:]
    y = jnp.zeros(h_ref.shape[1:], F32)
    hh = 0
    for o_ref in o_refs:
        for h in range(o_ref.shape[1]):
            y = y + jnp.dot(o_ref[0, h].astype(BF16), w_ref[hh], preferred_element_type=F32)
            hh += 1
    out_ref[0] = _layer_norm_rows(ALPHA * h_ref[0] + y, g_ref[...], b_ref[...])


def _heads_out_ln(outs, w, h, g, b):
    bn, t, _ = h.shape
    tm = min(t, 512)
    return pl.pallas_call(
        _heads_out_ln_kernel,
        grid=(bn, t // tm),
        in_specs=[pl.BlockSpec((1, o.shape[1], tm, HEAD_DIM), lambda b, i: (b, 0, i, 0)) for o in outs]
        + [_resident(w.shape), pl.BlockSpec((1, tm, D_MODEL), lambda b, i: (b, i, 0)),
           _resident((1, D_MODEL)), _resident((1, D_MODEL))],
        out_specs=pl.BlockSpec((1, tm, D_MODEL), lambda b, i: (b, i, 0)),
        out_shape=jax.ShapeDtypeStruct((bn, t, D_MODEL), F32),
        compiler_params=_params("parallel", "parallel"),
        name="heads_out_ln",
    )(*outs, w, h, g.reshape(1, D_MODEL), b.reshape(1, D_MODEL))


QS = 8


def _own_tile(new_rows):
    return jnp.concatenate([new_rows, jnp.zeros((TK - new_rows.shape[0], new_rows.shape[1]), F32)], axis=0)


def _topk_threshold_dense(keys, pos, k, idx_bits):
    rows = keys.shape[1]
    count = lambda pred: jnp.sum(jnp.sum(jnp.where(pred, 1.0, 0.0), axis=0), axis=1, keepdims=True)

    def value_bits(it, ans):
        c1, c2, c3 = [ans + lax.shift_left(jnp.int32(m), 30 - 2 * it) for m in (1, 2, 3)]
        n1, n2, n3 = count(keys >= c1), count(keys >= c2), count(keys >= c3)
        return jnp.where(n3 >= k, c3, jnp.where(n2 >= k, c2, jnp.where(n1 >= k, c1, ans)))
    tau = lax.fori_loop(0, 16, value_bits, jnp.full((rows, 1), INT_MIN, jnp.int32))

    def cut_among_ties():
        need = k - count(keys > tau)
        tie_pos = jnp.where(keys == tau, pos, jnp.int32(2 ** 30))

        def index_bit(it, cut):
            cand = cut | lax.shift_left(jnp.int32(1), idx_bits - 1 - it)
            return jnp.where(count(tie_pos < cand) < need, cand, cut)
        return lax.fori_loop(0, idx_bits, index_bit, jnp.zeros((rows, 1), jnp.int32))

    cut = lax.cond(jnp.max(count(keys >= tau)) > k, cut_among_ties,
                   lambda: jnp.full((rows, 1), 2 ** 30, jnp.int32))
    return tau, cut


def _tile_attention(lg_ref, madd, band, n_h, vt_ref, own_v):
    nt = lg_ref.shape[0]
    lg = lg_ref[...].reshape(nt, n_h, QS, TK) + madd[:, None]
    far = lg[:nt - 2]
    near = lg[nt - 2:] + jnp.stack([band[:, :, 0:TK], band[:, :, TK:2 * TK]])
    m = jnp.maximum(jnp.max(jnp.max(far, axis=0), axis=-1, keepdims=True),
                    jnp.max(jnp.max(near, axis=0), axis=-1, keepdims=True))
    p_far = jnp.exp(far - m[None])
    p_near = jnp.exp(near - m[None])
    den = (jnp.sum(jnp.sum(p_far, axis=0), axis=-1, keepdims=True)
           + jnp.sum(jnp.sum(p_near, axis=0), axis=-1, keepdims=True))
    p = jnp.concatenate([p_far, p_near], axis=0).reshape(nt, n_h * QS, TK).astype(BF16)
    n_steps, _, width = vt_ref.shape
    pps = width // TK
    acc = jnp.dot(p[nt - 1], own_v, preferred_element_type=F32)
    for s in range(n_steps):
        chunk = jnp.concatenate([p[s * pps + c] for c in range(pps)], axis=1)
        acc = acc + lax.dot_general(chunk, vt_ref[s], NT, preferred_element_type=F32)
    return acc / den.reshape(n_h * QS, 1)


def _pages_per_step(n_pages):
    return next(c for c in (8, 4, 2, 1) if n_pages % c == 0)


def _page_specs(block, layer, n_pages, pps):
    tail = (0,) * (len(block) - 2)
    return [pl.BlockSpec(block, lambda b, p, tbl, k=k: (layer, tbl[b * n_pages + p * pps + k]) + tail)
            for k in range(pps)]


def _indexer_scores(qi, wi, s):
    s = s.reshape(A_IDX_HEADS, QS, s.shape[1])
    sc = jnp.zeros(s.shape[1:], F32)
    for h in range(A_IDX_HEADS):
        sc = sc + wi[:, h:h + 1] * jnp.maximum(s[h] * A_IDX_DIM ** -0.5, 0.0)
    return sc * A_IDX_HEADS ** -0.5


def _tile_positions(nt):
    tile = lax.broadcasted_iota(jnp.int32, (nt, QS, TK), 0)
    lane = lax.broadcasted_iota(jnp.int32, (nt, QS, TK), 2)
    row = lax.broadcasted_iota(jnp.int32, (nt, QS, TK), 1)
    pos = tile * TK + lane
    return pos, pos <= (nt - 1) * TK + row


def _dsa_sample_kernel(tbl_ref, q_ref, *refs):
    pps = len(refs) - 5
    page_refs, (band_ref, o_ref, key_scr, lg_scr, vt_scr) = refs[:pps], refs[pps:]
    p = pl.program_id(1)
    nt = key_scr.shape[0]
    x = q_ref[0]
    qi = _heads_to_rows(x[:, COL_QI:COL_QI + A_IDX_HEADS * A_IDX_DIM].astype(BF16), A_IDX_HEADS)
    wi = x[:, COL_A + MISC_WI:COL_A + MISC_WI + A_IDX_HEADS]
    qh = _heads_to_rows((x[:, COL_QA:COL_QA + A_MIX] * HEAD_DIM ** -0.5).astype(BF16), A_HEADS)
    for c, page_ref in enumerate(page_refs):
        lp = p * pps + c
        key_scr[lp] = _ordered_key(_indexer_scores(
            qi, wi, jnp.dot(qi, page_ref[0, 0, 2].astype(BF16), preferred_element_type=F32)))
        lg_scr[lp] = jnp.dot(qh, page_ref[0, 0, 0].astype(BF16), preferred_element_type=F32)
        vt_scr[p, :, c * TK:(c + 1) * TK] = page_ref[0, 0, 1].astype(BF16)

    @pl.when(p == pl.num_programs(1) - 1)
    def _():
        own = _own_tile(x[:, COL_A:COL_A + 3 * HEAD_DIM]).astype(BF16)
        pos, causal = _tile_positions(nt)
        sc = _indexer_scores(qi, wi, lax.dot_general(qi, own[:, 2 * HEAD_DIM:3 * HEAD_DIM], NT,
                                                     preferred_element_type=F32))
        key_scr[nt - 1] = _ordered_key(jnp.where(causal[nt - 1], sc, NEG))
        lg_scr[nt - 1] = lax.dot_general(qh, own[:, 0:HEAD_DIM], NT, preferred_element_type=F32)
        keys = key_scr[...]
        tau, cut = _topk_threshold_dense(keys, pos, float(A_TOPK), DSA_IDX_BITS)
        sel = ((keys > tau) | ((keys == tau) & (pos <= cut))) & causal
        o = _tile_attention(lg_scr, jnp.where(sel, 0.0, NEG), band_ref[:, 0:QS, :], A_HEADS, vt_scr,
                            own[:, HEAD_DIM:2 * HEAD_DIM])
        o_ref[0] = o.reshape(A_HEADS, QS, HEAD_DIM)


def _dsa_sample(proj, pool_t, page_table, layer, band_a):
    ns = proj.shape[0]
    n_pages = page_table.shape[1]
    nt = n_pages + 1
    pps = _pages_per_step(n_pages)
    grid_spec = pltpu.PrefetchScalarGridSpec(
        num_scalar_prefetch=1, grid=(ns, n_pages // pps),
        in_specs=[pl.BlockSpec((1, QS, EVEN_COLS), lambda b, p, tbl: (b, 0, 0))]
        + _page_specs((1, 1, 3, HEAD_DIM, PAGE_SIZE), layer, n_pages, pps)
        + [pl.BlockSpec((A_HEADS, TK, 2 * TK), lambda b, p, tbl: (0, 0, 0))],
        out_specs=pl.BlockSpec((1, A_HEADS, QS, HEAD_DIM), lambda b, p, tbl: (b, 0, 0, 0)),
        scratch_shapes=[pltpu.VMEM((nt, QS, TK), jnp.int32), pltpu.VMEM((nt, A_HEADS * QS, TK), F32),
                        pltpu.VMEM((n_pages // pps, HEAD_DIM, pps * TK), BF16)])
    return pl.pallas_call(
        _dsa_sample_kernel, grid_spec=grid_spec,
        out_shape=jax.ShapeDtypeStruct((ns, A_HEADS, QS, HEAD_DIM), F32),
        compiler_params=_params("parallel", "arbitrary"),
        name="dsa_sample",
    )(page_table.reshape(-1), proj, *([pool_t] * pps), band_a)


def _nsa_sample_kernel(tbl_ref, q_ref, *refs):
    pps = len(refs) - 11
    page_refs = refs[:pps]
    (win_ref, pos_ref, w1_ref, w2_ref, band_ref, cband_ref, expand_ref,
     o_ref, lg_scr, vt_scr, pool_scr) = refs[pps:]
    p = pl.program_id(1)
    nt = lg_scr.shape[1]
    nbl = pool_scr.shape[-1]
    x = q_ref[0]
    q = (x[:, COL_QB:COL_QB + B_MIX] * HEAD_DIM ** -0.5).astype(BF16)
    qg = [_heads_to_rows(q[:, g * B_R * HEAD_DIM:(g + 1) * B_R * HEAD_DIM], B_R) for g in range(B_KV_HEADS)]

    @pl.when(p == 0)
    def _():
        pool_scr[...] = jnp.zeros(pool_scr.shape, F32)

    ptok = lax.broadcasted_iota(jnp.int32, (TK, nbl), 0)
    pblk = lax.broadcasted_iota(jnp.int32, (TK, nbl), 1)
    for c, page_ref in enumerate(page_refs):
        lp = p * pps + c
        place = jnp.where(pblk == 2 * lp + ptok // B_BLOCK, 1.0, 0.0).astype(BF16)
        for g in range(B_KV_HEADS):
            for s in range(2):
                weighted = (page_ref[0, 0, s, g] * pos_ref[s]).astype(BF16)
                pool_scr[s, g] += jnp.dot(weighted, place, preferred_element_type=F32)
            lg_scr[g, lp] = jnp.dot(qg[g], page_ref[0, 0, 2, g].astype(BF16), preferred_element_type=F32)
            vt_scr[g, p, :, c * TK:(c + 1) * TK] = page_ref[0, 0, 3, g].astype(BF16)

    @pl.when(p == pl.num_programs(1) - 1)
    def _():
        past = (nt - 1) * TK
        n_blk = past // B_BLOCK
        nb_w = expand_ref.shape[0]
        own = _own_tile(x[:, COL_B + 2 * G_COLS:COL_B + 4 * G_COLS]).astype(BF16)
        gates = jax.nn.sigmoid(x[:, COL_A + MISC_GL:COL_A + MISC_GL + 3 * B_HEADS])
        blk = lax.broadcasted_iota(jnp.int32, (QS, nb_w), 1)
        tpos = past + lax.broadcasted_iota(jnp.int32, (QS, nb_w), 0)
        cur = tpos // B_BLOCK
        valid_c = ((blk + 1) * B_BLOCK - 1 <= tpos)[:, :nbl]
        forced = jnp.where((blk == 0) | (blk == cur) | (blk == cur - 1), FORCE_SCORE, 0.0)
        i_own = past // TK
        _, causal = _tile_positions(nt)
        own_win = _own_tile(x[:, COL_WIN:COL_WIN + 2 * G_COLS]).astype(BF16)
        o_cmps, imps = [], []
        for g in range(B_KV_HEADS):
            cmp_t = []
            for s in range(2):
                hid = jax.nn.gelu(jnp.dot(w1_ref[s], pool_scr[s, g].astype(BF16), preferred_element_type=F32))
                cmp_t.append(jnp.dot(w2_ref[s], hid.astype(BF16), preferred_element_type=F32).astype(BF16))
            lc = jnp.dot(qg[g], cmp_t[0], preferred_element_type=F32).reshape(B_R, QS, nbl)
            bias = []
            for r in range(B_R):
                cb = cband_ref[g * B_R + r][0:QS, :]
                b = jnp.zeros((QS, nbl), F32)
                for u in range(4):
                    b = jnp.where(blk[:, :nbl] == 2 * i_own + u - 2, cb[:, u:u + 1], b)
                bias.append(b)
            lc = jnp.where(valid_c[None], lc + jnp.stack(bias), NEG).reshape(B_R * QS, nbl)
            (pr,), den, _ = _softmax_pieces([lc])
            pc = (pr / den).reshape(B_R, QS, nbl) * jnp.where(valid_c, 1.0, 0.0)[None]
            o_cmps.append(lax.dot_general(pc.reshape(B_R * QS, nbl).astype(BF16), cmp_t[1], NT,
                                          preferred_element_type=F32))
            imp = jnp.concatenate([jnp.sum(pc, axis=0), jnp.zeros((QS, nb_w - nbl), F32)], axis=1)
            imps.append(jnp.where(blk <= cur, imp + forced, jnp.where(blk <= n_blk, NEG, -jnp.inf)))
        selm = _select_blocks(jnp.concatenate(imps, axis=0), B_TOPN)
        for g in range(B_KV_HEADS):
            gc = slice(g * HEAD_DIM, (g + 1) * HEAD_DIM)
            o_cmp = o_cmps[g]
            sel = jnp.dot(selm[g * QS:(g + 1) * QS].astype(BF16), expand_ref[...],
                          preferred_element_type=F32)
            madd = jnp.stack([jnp.where((sel[:, t * TK:(t + 1) * TK] > 0.5) & causal[t], 0.0, NEG)
                              for t in range(nt)])
            lg_scr[g, nt - 1] = lax.dot_general(qg[g], own[:, gc], NT, preferred_element_type=F32)
            o_sel = _tile_attention(lg_scr.at[g], madd, band_ref[g * B_R:(g + 1) * B_R, 0:QS, :], B_R, vt_scr.at[g],
                                    own[:, G_COLS + g * HEAD_DIM:G_COLS + (g + 1) * HEAD_DIM])
            tiles = [(own_win[:, gc], own_win[:, G_COLS + g * HEAD_DIM:G_COLS + (g + 1) * HEAD_DIM], True)]
            for d in range(1, NSA_WIN_TILES + 1):
                wr = slice((NSA_WIN_TILES - d) * TK, (NSA_WIN_TILES - d + 1) * TK)
                tiles.append((win_ref[0, wr, gc].astype(BF16),
                              win_ref[0, wr, G_COLS + g * HEAD_DIM:G_COLS + (g + 1) * HEAD_DIM].astype(BF16), True))
            o_win = _window_attend(qg[g], B_R, QS, tiles, band_ref, g * B_R)
            for r in range(B_R):
                h = g * B_R + r
                rr = slice(r * QS, (r + 1) * QS)
                o_ref[0, h] = (gates[:, 3 * h:3 * h + 1] * o_cmp[rr] + gates[:, 3 * h + 1:3 * h + 2] * o_sel[rr]
                               + gates[:, 3 * h + 2:3 * h + 3] * o_win[rr])


def _round_up(n, m):
    return -(-n // m) * m


def _sample_expand(n_pages):
    nb_w = _round_up(2 * n_pages + 1, LANES)
    tok_blk = np.arange((n_pages + 1) * TK) // B_BLOCK
    return jnp.asarray(np.arange(nb_w)[:, None] == tok_blk[None, :], BF16)


def _compress_weights_t(cmp_pos, cmp_w1, cmp_w2):
    pos_t = jnp.concatenate([jnp.swapaxes(cmp_pos, 1, 2)] * (PAGE_SIZE // B_BLOCK), axis=2)
    return pos_t, jnp.swapaxes(cmp_w1, 1, 2).astype(BF16), jnp.swapaxes(cmp_w2, 1, 2).astype(BF16)


def _nsa_sample(proj, pool_t, win, page_table, layer, cmp_w, band_b, cband_b, expand):
    ns = proj.shape[0]
    n_pages = page_table.shape[1]
    nt = n_pages + 1
    nbl = _round_up(2 * n_pages, LANES)
    const = lambda shape: pl.BlockSpec(shape, lambda b, p, tbl: (0,) * len(shape))
    pps = _pages_per_step(n_pages)
    grid_spec = pltpu.PrefetchScalarGridSpec(
        num_scalar_prefetch=1, grid=(ns, n_pages // pps),
        in_specs=[pl.BlockSpec((1, QS, EVEN_COLS), lambda b, p, tbl: (b, 0, 0))]
        + _page_specs((1, 1, 4, B_KV_HEADS, HEAD_DIM, PAGE_SIZE), layer, n_pages, pps)
        + [pl.BlockSpec((1, B_WINDOW, 2 * G_COLS), lambda b, p, tbl: (b, 0, 0)),
           const((2, HEAD_DIM, PAGE_SIZE)), const((2, HEAD_DIM, HEAD_DIM)), const((2, HEAD_DIM, HEAD_DIM)),
           const((B_HEADS, TK, 2 * TK)), const((B_HEADS, TK, 4)), const(expand.shape)],
        out_specs=pl.BlockSpec((1, B_HEADS, QS, HEAD_DIM), lambda b, p, tbl: (b, 0, 0, 0)),
        scratch_shapes=[pltpu.VMEM((B_KV_HEADS, nt, B_R * QS, TK), F32),
                        pltpu.VMEM((B_KV_HEADS, n_pages // pps, HEAD_DIM, pps * TK), BF16),
                        pltpu.VMEM((2, B_KV_HEADS, HEAD_DIM, nbl), F32)])
    return pl.pallas_call(
        _nsa_sample_kernel, grid_spec=grid_spec,
        out_shape=jax.ShapeDtypeStruct((ns, B_HEADS, QS, HEAD_DIM), F32),
        compiler_params=_params("parallel", "arbitrary"),
        name="nsa_sample",
    )(page_table.reshape(-1), proj, *([pool_t] * pps), win, *cmp_w, band_b, cband_b, expand)


def _swa_sample_kernel(q_ref, buf_ref, band_ref, sink_ref, o_ref):
    x = q_ref[0]
    q = (x[:, COL_QC:COL_QC + C_MIX] * HEAD_DIM ** -0.5).astype(BF16)
    own = _own_tile(x[:, COL_CWIN:COL_CWIN + 2 * G_COLS]).astype(BF16)
    for g in range(C_KV_HEADS):
        gc = slice(g * HEAD_DIM, (g + 1) * HEAD_DIM)
        vc = slice(G_COLS + g * HEAD_DIM, G_COLS + (g + 1) * HEAD_DIM)
        qg = _heads_to_rows(q[:, g * C_R * HEAD_DIM:(g + 1) * C_R * HEAD_DIM], C_R)
        tiles = [(own[:, gc], own[:, vc], True),
                 (buf_ref[0, :, gc].astype(BF16), buf_ref[0, :, vc].astype(BF16), True)]
        sink = jnp.concatenate([jnp.broadcast_to(sink_ref[g * C_R + r], (QS, 1)) for r in range(C_R)], axis=0)
        o = _window_attend(qg, C_R, QS, tiles, band_ref, g * C_R, sink)
        o_ref[0, g * C_R:(g + 1) * C_R] = o.reshape(C_R, QS, HEAD_DIM)


def _swa_sample(proj, buf, band_c, sinks):
    ns = proj.shape[0]
    return pl.pallas_call(
        _swa_sample_kernel,
        grid=(ns,),
        in_specs=[pl.BlockSpec((1, QS, C_MIX + 2 * G_COLS), lambda b: (b, 0, 0)),
                  pl.BlockSpec((1, C_WINDOW, 2 * G_COLS), lambda b: (b, 0, 0)),
                  _resident((C_HEADS, TK, 2 * TK)), _resident((C_HEADS, 1, 1))],
        out_specs=pl.BlockSpec((1, C_HEADS, QS, HEAD_DIM), lambda b: (b, 0, 0, 0)),
        out_shape=jax.ShapeDtypeStruct((ns, C_HEADS, QS, HEAD_DIM), F32),
        compiler_params=_params("parallel"),
        name="swa_sample",
    )(proj, buf, band_c, sinks)


def _trunk_prompt(x, mem_kv, w):
    bn, t, _ = x.shape
    assert t % TK == 0 and t >= 4 * A_TOPK
    n = bn * t
    a_rows, b_rows, b_states, c_states = [], [], [], []
    kb, kc = min(B_WINDOW, t), min(C_WINDOW, t)
    h = x.reshape(n, D_MODEL)
    for l in range(DEPTH):
        g, bb = w['ln_g'][l], w['ln_b'][l]
        h = _ffn_ln(h, w['ffn_wg'][l, 0], w['ffn_wu'][l, 0], w['ffn_wd'][l, 0], g[0], bb[0])
        if l % 2 == 0:
            e = l // 2
            proj = _linear(h, w['even_w_in_perm'][e]).reshape(bn, t, EVEN_COLS)
            cmp_kv = _compress_prompt(proj, *w['cmp'][e])
            o_a = _dsa_prompt(proj, w['band'][:A_HEADS])
            o_b = _nsa_prompt(proj, cmp_kv, w['band'][A_HEADS:], w['cband'][A_HEADS:])
            h = _heads_out_ln([o_a, o_b], w['even_w_out_heads'][e], h.reshape(bn, t, D_MODEL), g[1], bb[1])
            a_rows.append(proj[:, :, COL_A:COL_A + 3 * HEAD_DIM].reshape(bn, t, 3, HEAD_DIM))
            b_rows.append(proj[:, :, COL_B:COL_B + 4 * G_COLS].reshape(bn, t, 4, B_KV_HEADS, HEAD_DIM))
            b_states.append(proj[:, t - kb:, COL_WIN:COL_WIN + 2 * G_COLS].reshape(bn, kb, 2, B_KV_HEADS, HEAD_DIM))
        else:
            oi = l // 2
            proj = _linear(h, w['odd_w_in'][oi]).reshape(bn, t, C_MIX + 2 * G_COLS)
            o_c = _swa_prompt(proj, w['band'], w['sinks_shifted'][oi])
            h = _heads_out_ln([o_c], w['odd_w_out_heads'][oi], h.reshape(bn, t, D_MODEL), g[1], bb[1])
            c_states.append(proj[:, t - kc:, COL_CWIN:].reshape(bn, kc, 2, C_KV_HEADS, HEAD_DIM))
        h = h.reshape(n, D_MODEL)
        xq = _linear(h, w['x_wq'][l]).reshape(bn, t, X_WIDTH)
        xo = _cross_attn(xq, mem_kv[l].reshape(bn, -1, 2 * X_WIDTH)).reshape(n, X_WIDTH)
        h = _linear_res_ln(xo, w['x_wo'][l], h, g[2], bb[2])
        h = _ffn_ln(h, w['ffn_wg'][l, 1], w['ffn_wu'][l, 1], w['ffn_wd'][l, 1], g[3], bb[3])
    return (h.reshape(bn, t, D_MODEL), jnp.stack(a_rows), jnp.stack(b_rows), jnp.stack(b_states),
            jnp.stack(c_states))


def _shift_in(buf, new):
    return jnp.concatenate([buf, new], axis=1)[:, new.shape[1]:]


def _trunk_sample(x, mem_kv, a_pool, b_pool, b_win, c_win, page_table, w):
    ns, t, _ = x.shape
    assert t <= QS and b_win.shape[2] == B_WINDOW and c_win.shape[2] == C_WINDOW
    assert page_table.shape[1] * PAGE_SIZE >= max(4 * A_TOPK, B_TOPN * B_BLOCK, 3 * TK)
    n = ns * QS
    n_pages = page_table.shape[1]
    a_pool = jnp.transpose(a_pool, (0, 1, 3, 4, 2))
    b_pool = jnp.transpose(b_pool, (0, 1, 3, 4, 5, 2))
    expand = _sample_expand(n_pages)
    a_rows, b_rows, b_states, c_states = [], [], [], []
    h = jnp.pad(x, ((0, 0), (0, QS - t), (0, 0))).reshape(n, D_MODEL)
    for l in range(DEPTH):
        g, bb = w['ln_g'][l], w['ln_b'][l]
        h = _ffn_ln(h, w['ffn_wg'][l, 0], w['ffn_wu'][l, 0], w['ffn_wd'][l, 0], g[0], bb[0])
        if l % 2 == 0:
            e = l // 2
            proj = _linear(h, w['even_w_in_perm'][e]).reshape(ns, QS, EVEN_COLS)
            win = b_win[e].reshape(ns, B_WINDOW, 2 * G_COLS)
            o_a = _dsa_sample(proj, a_pool, page_table, e, w['band'][:A_HEADS])
            o_b = _nsa_sample(proj, b_pool, win, page_table, e, w['cmp_t'][e], w['band'][A_HEADS:],
                              w['cband'][A_HEADS:], expand)
            h = _heads_out_ln([o_a, o_b], w['even_w_out_heads'][e], h.reshape(ns, QS, D_MODEL), g[1], bb[1])
            a_rows.append(proj[:, :t, COL_A:COL_A + 3 * HEAD_DIM].reshape(ns, t, 3, HEAD_DIM))
            b_rows.append(proj[:, :t, COL_B:COL_B + 4 * G_COLS].reshape(ns, t, 4, B_KV_HEADS, HEAD_DIM))
            b_states.append(_shift_in(win, proj[:, :t, COL_WIN:COL_WIN + 2 * G_COLS]).reshape(b_win.shape[1:]))
        else:
            oi = l // 2
            proj = _linear(h, w['odd_w_in'][oi]).reshape(ns, QS, C_MIX + 2 * G_COLS)
            buf = c_win[oi].reshape(ns, C_WINDOW, 2 * G_COLS)
            o_c = _swa_sample(proj, buf, w['band'], w['sinks_shifted'][oi])
            h = _heads_out_ln([o_c], w['odd_w_out_heads'][oi], h.reshape(ns, QS, D_MODEL), g[1], bb[1])
            c_states.append(_shift_in(buf, proj[:, :t, COL_CWIN:]).reshape(c_win.shape[1:]))
        h = h.reshape(n, D_MODEL)
        xq = _linear(h, w['x_wq'][l]).reshape(ns, QS, X_WIDTH)
        xo = _cross_attn(xq, mem_kv[l].reshape(ns, -1, 2 * X_WIDTH)).reshape(n, X_WIDTH)
        h = _linear_res_ln(xo, w['x_wo'][l], h, g[2], bb[2])
        h = _ffn_ln(h, w['ffn_wg'][l, 1], w['ffn_wu'][l, 1], w['ffn_wd'][l, 1], g[3], bb[3])
    return (h.reshape(ns, QS, D_MODEL)[:, :t], jnp.stack(a_rows), jnp.stack(b_rows), jnp.stack(b_states),
            jnp.stack(c_states))


def kernel(x_prompt, x_sample, cache_a_kv, cache_b_kv, cache_b_win, cache_c_win, cache_mem_kv, page_table,
           mem_prompt, ln_g, ln_b, ffn_wg, ffn_wu, ffn_wd, even_w_in, even_w_out, nsa_cmp_pos, nsa_cmp_w1,
           nsa_cmp_w2, odd_w_in, odd_w_out, c_sinks, x_wq, x_wk, x_wv, x_wo, rel_table):
    nb, n_mem = mem_prompt.shape[:2]
    dist_bias = _distance_bias(rel_table)
    w = dict(ln_g=ln_g, ln_b=ln_b, ffn_wg=ffn_wg.astype(BF16), ffn_wu=ffn_wu.astype(BF16),
             ffn_wd=ffn_wd.astype(BF16), odd_w_in=odd_w_in.astype(BF16),
             x_wq=x_wq.astype(BF16), x_wo=x_wo.astype(BF16),
             even_w_in_perm=_permute_even_w_in(even_w_in).astype(BF16),
             even_w_out_heads=even_w_out.astype(BF16).reshape(-1, A_HEADS + B_HEADS, HEAD_DIM, D_MODEL),
             odd_w_out_heads=odd_w_out.astype(BF16).reshape(-1, C_HEADS, HEAD_DIM, D_MODEL),
             band=_band(dist_bias), cband=_cmp_band(dist_bias),
             cmp=[_compress_weights(nsa_cmp_pos[e], nsa_cmp_w1[e], nsa_cmp_w2[e]) for e in range(even_w_in.shape[0])],
             cmp_t=[_compress_weights_t(nsa_cmp_pos[e], nsa_cmp_w1[e], nsa_cmp_w2[e])
                    for e in range(even_w_in.shape[0])],
             sinks_shifted=(c_sinks - rel_table[N_BUCKETS - 1][None, :]).reshape(-1, C_HEADS, 1, 1))
    w_kv = jnp.concatenate([x_wk, x_wv], axis=-1).astype(BF16)
    mem2d = mem_prompt.reshape(nb * n_mem, D_MODEL)
    mem_kv_prompt = jnp.stack([_linear(mem2d, w_kv[l]) for l in range(DEPTH)])
    mem_kv_prompt = mem_kv_prompt.reshape(DEPTH, nb, n_mem, 2, X_HEADS, X_HEAD_DIM)
    y_p, a_p, b_p, bw_p, cw_p = _trunk_prompt(x_prompt, mem_kv_prompt, w)
    y_s, a_s, b_s, bw_s, cw_s = _trunk_sample(x_sample, cache_mem_kv, cache_a_kv, cache_b_kv, cache_b_win,
                                              cache_c_win, page_table, w)
    return (y_p, y_s, a_p, a_s, b_p, b_s, bw_p, bw_s, cw_p, cw_s, mem_kv_prompt)
```

```python
import functools
import math

import jax
import jax.numpy as jnp
import numpy as np
from jax import lax
from jax.experimental import pallas as pl
from jax.experimental.pallas import tpu as pltpu

D_MODEL = 1024
DEPTH = 4
PAGE_SIZE = 128
HEAD_DIM = 64
A_HEADS = 8
A_IDX_HEADS = 4
A_IDX_DIM = 64
A_TOPK = 256
B_HEADS = 8
B_KV_HEADS = 2
B_BLOCK = 64
B_TOPN = 16
B_WINDOW = 512
C_HEADS = 16
C_KV_HEADS = 2
C_WINDOW = 128
X_HEADS = 4
X_HEAD_DIM = 128
D_FF = 2816
N_BUCKETS = 32
BUCKET_MAX_DIST = 128
QBLOCK = 128
LN_EPS = 1e-5
ALPHA = (2 * DEPTH) ** 0.25
NEG = -1e30
FORCE_SCORE = 1e4
EVEN_SPLITS = [A_HEADS * HEAD_DIM, HEAD_DIM, HEAD_DIM, A_IDX_HEADS * A_IDX_DIM, A_IDX_DIM, A_IDX_HEADS,
               B_HEADS * HEAD_DIM] + [B_KV_HEADS * HEAD_DIM] * 6 + [B_HEADS * 3]
ODD_SPLITS = [C_HEADS * HEAD_DIM, C_KV_HEADS * HEAD_DIM, C_KV_HEADS * HEAD_DIM]
A_MIX = A_HEADS * HEAD_DIM
B_MIX = B_HEADS * HEAD_DIM
C_MIX = C_HEADS * HEAD_DIM
X_WIDTH = X_HEADS * X_HEAD_DIM

LANES = 128
VMEM_LIMIT = 56 << 20
BF16 = jnp.bfloat16
F32 = jnp.float32


def _params(*sem):
    return pltpu.CompilerParams(dimension_semantics=sem, vmem_limit_bytes=VMEM_LIMIT)


def _resident(shape):
    return pl.BlockSpec(shape, lambda *_: (0,) * len(shape), pipeline_mode=pl.Buffered(1))


def _per_batch(shape, col_block):
    return pl.BlockSpec(shape, lambda b, i: (b, 0, col_block), pipeline_mode=pl.Buffered(1))


def _row_tile(n):
    return min(n, 512)


def _layer_norm_rows(y, g, b):
    mu = jnp.mean(y, axis=-1, keepdims=True)
    d = y - mu
    var = jnp.mean(d * d, axis=-1, keepdims=True)
    return d * lax.rsqrt(var + LN_EPS) * g + b


FF_CHUNKS = 2
FF_CHUNK = D_FF // FF_CHUNKS


def _ffn_ln_kernel(x_ref, wg_ref, wu_ref, wd_ref, g_ref, b_ref, o_ref):
    x = x_ref[...]
    xb = x.astype(BF16)
    acc = jnp.zeros(x.shape, F32)
    for c in range(FF_CHUNKS):
        cols = slice(c * FF_CHUNK, (c + 1) * FF_CHUNK)
        gate = jnp.dot(xb, wg_ref[:, cols], preferred_element_type=F32)
        up = jnp.dot(xb, wu_ref[:, cols], preferred_element_type=F32)
        act = (gate * jax.nn.sigmoid(gate) * up).astype(BF16)
        acc = acc + jnp.dot(act, wd_ref[cols, :], preferred_element_type=F32)
    o_ref[...] = _layer_norm_rows(ALPHA * x + 0.5 * acc, g_ref[...], b_ref[...])


def _ffn_ln(x, wg, wu, wd, g, b):
    n = x.shape[0]
    tm = _row_tile(n)
    return pl.pallas_call(
        _ffn_ln_kernel,
        grid=(n // tm,),
        in_specs=[pl.BlockSpec((tm, D_MODEL), lambda i: (i, 0)),
                  _resident((D_MODEL, D_FF)), _resident((D_MODEL, D_FF)), _resident((D_FF, D_MODEL)),
                  _resident((1, D_MODEL)), _resident((1, D_MODEL))],
        out_specs=pl.BlockSpec((tm, D_MODEL), lambda i: (i, 0)),
        out_shape=jax.ShapeDtypeStruct((n, D_MODEL), F32),
        compiler_params=_params("parallel"),
        name="ffn_ln",
    )(x, wg, wu, wd, g.reshape(1, D_MODEL), b.reshape(1, D_MODEL))


def _linear_kernel(x_ref, w_ref, o_ref):
    o_ref[...] = jnp.dot(x_ref[...].astype(BF16), w_ref[...], preferred_element_type=F32)


def _linear(x, w):
    n, k = x.shape
    m = w.shape[1]
    tm = _row_tile(n)
    return pl.pallas_call(
        _linear_kernel,
        grid=(n // tm,),
        in_specs=[pl.BlockSpec((tm, k), lambda i: (i, 0)), _resident((k, m))],
        out_specs=pl.BlockSpec((tm, m), lambda i: (i, 0)),
        out_shape=jax.ShapeDtypeStruct((n, m), F32),
        compiler_params=_params("parallel"),
        name="linear",
    )(x, w)


def _linear_res_ln_kernel(x_ref, w_ref, h_ref, g_ref, b_ref, o_ref):
    y = jnp.dot(x_ref[...].astype(BF16), w_ref[...], preferred_element_type=F32)
    o_ref[...] = _layer_norm_rows(ALPHA * h_ref[...] + y, g_ref[...], b_ref[...])


def _linear_res_ln(x, w, h, g, b):
    n, k = x.shape
    tm = _row_tile(n)
    return pl.pallas_call(
        _linear_res_ln_kernel,
        grid=(n // tm,),
        in_specs=[pl.BlockSpec((tm, k), lambda i: (i, 0)), _resident((k, D_MODEL)),
                  pl.BlockSpec((tm, D_MODEL), lambda i: (i, 0)),
                  _resident((1, D_MODEL)), _resident((1, D_MODEL))],
        out_specs=pl.BlockSpec((tm, D_MODEL), lambda i: (i, 0)),
        out_shape=jax.ShapeDtypeStruct((n, D_MODEL), F32),
        compiler_params=_params("parallel"),
        name="linear_res_ln",
    )(x, w, h, g.reshape(1, D_MODEL), b.reshape(1, D_MODEL))


def t5_bucket(dist):
    exact = N_BUCKETS // 2
    d = jnp.maximum(dist, 0)
    rel = jnp.log(jnp.maximum(d, 1).astype(jnp.float32) / exact) / math.log(BUCKET_MAX_DIST / exact)
    large = jnp.minimum(exact + (rel * (N_BUCKETS - exact)).astype(jnp.int32), N_BUCKETS - 1)
    return jnp.where(d < exact, d, large)


TK = 128
INT_MIN = -2 ** 31
NT = (((1,), (1,)), ((), ()))


def _ordered_key(x):
    bits = pltpu.bitcast(x, jnp.int32)
    key = jnp.where(bits < 0, bits ^ jnp.int32(0x7FFFFFFF), bits)
    return jnp.where(x == 0.0, 0, key)


def _count(key_scr, nkt, rows, pred):
    hit = lambda j: jnp.where(pred(key_scr[j], j), 1.0, 0.0)

    def pair(j2, c):
        return c + hit(2 * j2) + hit(2 * j2 + 1)
    c = lax.fori_loop(0, nkt // 2, pair, jnp.zeros((rows, TK), F32))
    c = c + jnp.where(nkt % 2 == 1, 1.0, 0.0) * hit(nkt - 1)
    return jnp.sum(c, axis=1, keepdims=True)


def _topk_threshold(key_scr, nkt, rows, k, idx_bits):
    lane = lax.broadcasted_iota(jnp.int32, (rows, TK), 1)

    def value_bit(it, ans):
        cand = ans + lax.shift_left(jnp.int32(1), 31 - it)
        cand_key = jnp.broadcast_to(cand, (rows, TK))
        cnt = _count(key_scr, nkt, rows, lambda t, j: t >= cand_key)
        return jnp.where(cnt >= k, cand, ans)
    tau = lax.fori_loop(0, 32, value_bit, jnp.full((rows, 1), INT_MIN, jnp.int32))
    n_ge = _count(key_scr, nkt, rows, lambda t, j: t >= tau)

    def cut_among_ties():
        need = k - _count(key_scr, nkt, rows, lambda t, j: t > tau)

        def index_bit(it, cut):
            cand = cut | lax.shift_left(jnp.int32(1), idx_bits - 1 - it)
            cnt = _count(key_scr, nkt, rows,
                         lambda t, j: jnp.where(t == tau, j * TK + lane, jnp.int32(2 ** 30)) < cand)
            return jnp.where(cnt < need, cand, cut)
        return lax.fori_loop(0, idx_bits, index_bit, jnp.zeros((rows, 1), jnp.int32))

    cut = lax.cond(jnp.max(n_ge) > k, cut_among_ties, lambda: jnp.full((rows, 1), 2 ** 30, jnp.int32))
    return tau, cut


def _in_topk(t, j, tau, cut):
    lane = lax.broadcasted_iota(jnp.int32, t.shape, 1)
    return (t > tau) | ((t == tau) & (j * TK + lane <= cut))


def _heads_to_rows(x, n_heads):
    return jnp.concatenate([x[:, h * HEAD_DIM:(h + 1) * HEAD_DIM] for h in range(n_heads)], axis=0)


FAR_TILES = 4


def _flash_init(mx_scr, acc_scr):
    mx_scr[...] = jnp.full(mx_scr.shape, NEG, F32)
    acc_scr[...] = jnp.zeros(acc_scr.shape, F32)


def _with_ones(v):
    return jnp.concatenate([v, jnp.ones(v.shape, BF16)], axis=1)


def _masked_logits(q_rows, k, madd, bias):
    n_h = q_rows.shape[0] // TK
    lg = lax.dot_general(q_rows, k, NT, preferred_element_type=F32).reshape(n_h, TK, k.shape[0]) + madd[None]
    return lg if bias is None else lg + bias


def _max_step(mx_scr, heads, q_rows, k, madd, bias):
    lg = _masked_logits(q_rows, k, madd, bias)
    m = mx_scr[heads]
    for c in range(k.shape[0] // TK):
        m = jnp.maximum(m, lg[:, :, c * TK:(c + 1) * TK])
    mx_scr[heads] = m


def _max_finish(mx_scr):
    mx_scr[...] = jnp.broadcast_to(jnp.max(mx_scr[...], axis=-1, keepdims=True), mx_scr.shape)


def _acc_step(mx_scr, acc_scr, heads, q_rows, k, v1, madd, bias):
    lg = _masked_logits(q_rows, k, madd, bias)
    n_h, _, n = lg.shape
    p = jnp.exp(lg - jnp.concatenate([mx_scr[heads]] * (n // TK), axis=2)).astype(BF16)
    acc_scr[heads] += jnp.dot(p.reshape(n_h * TK, n), v1, preferred_element_type=F32).reshape(n_h, TK, 2 * HEAD_DIM)


def _flash_far_loop(n_far, attend):
    n_big = n_far // FAR_TILES

    def big(jb, carry):
        attend(jb * FAR_TILES, FAR_TILES)
        return carry
    lax.fori_loop(0, n_big, big, 0)

    def small(j, carry):
        attend(j, 1)
        return carry
    lax.fori_loop(n_big * FAR_TILES, n_far, small, 0)


def _two_pass_attention(i, mx_scr, attend):
    for accumulate in (False, True):
        _flash_far_loop(jnp.maximum(i - 1, 0), functools.partial(attend, accumulate))

        @pl.when(i >= 1)
        def _():
            attend(accumulate, i - 1, 1, slice(0, TK))
        attend(accumulate, i, 1, slice(TK, 2 * TK), True)
        if not accumulate:
            _max_finish(mx_scr)


def _flash_result(acc_scr, h):
    acc = acc_scr[h]
    return acc[:, 0:HEAD_DIM] / acc[:, HEAD_DIM:2 * HEAD_DIM]


def _window_attend(qg, n_r, qr, tiles, band_ref, head0, sink=None):
    row = lax.broadcasted_iota(jnp.int32, (qr, TK), 0)
    lane = lax.broadcasted_iota(jnp.int32, (qr, TK), 1)
    heads = slice(head0, head0 + n_r)
    pieces = []
    for d, (k, _, exists) in enumerate(tiles):
        lg = lax.dot_general(qg, k, NT, preferred_element_type=F32).reshape(n_r, qr, TK)
        ok = jnp.broadcast_to(exists, (qr, TK))
        if d == 0:
            ok = ok & (lane <= row)
            lg = lg + band_ref[heads, 0:qr, TK:2 * TK]
        if d == 1:
            lg = lg + band_ref[heads, 0:qr, 0:TK]
        if d == len(tiles) - 1:
            ok = ok & (lane > row)
        pieces.append(jnp.where(ok[None], lg, NEG).reshape(n_r * qr, TK))
    m = jnp.max(functools.reduce(jnp.maximum, pieces), axis=1, keepdims=True)
    if sink is not None:
        m = jnp.maximum(m, sink)
    ps = [jnp.exp(x - m) for x in pieces]
    den = jnp.sum(functools.reduce(lambda a, b: a + b, ps), axis=1, keepdims=True)
    if sink is not None:
        den = den + jnp.exp(sink - m)
    o = functools.reduce(lambda a, b: a + b,
                         [jnp.dot(p.astype(BF16), t[1], preferred_element_type=F32) for p, t in zip(ps, tiles)])
    return o / den


def _distance_bias(rel_table):
    tb = rel_table[t5_bucket(jnp.arange(2 * TK))]
    return tb - rel_table[N_BUCKETS - 1][None, :]


def _band(dist_bias):
    a = np.arange(TK)[:, None]
    c = np.arange(2 * TK)[None, :]
    idx = np.clip(a - c + TK, 0, 2 * TK - 1)
    return jnp.transpose(dist_bias[idx], (2, 0, 1))


def _dsa_prompt_kernel(q_ref, qi_ref, misc_ref, kv_ref, band_ref, o_ref, key_scr, m_scr, acc_scr):
    i = pl.program_id(1)
    nkt = i + 1
    row = lax.broadcasted_iota(jnp.int32, (TK, TK), 0)
    lane = lax.broadcasted_iota(jnp.int32, (TK, TK), 1)
    qi = qi_ref[0].astype(BF16)
    wi = [jnp.broadcast_to(misc_ref[0][:, MISC_WI + h:MISC_WI + h + 1], (TK, TK)) for h in range(A_IDX_HEADS)]

    def score_tiles(j0, n, causal=False):
        rows = pl.ds(pl.multiple_of(j0 * TK, TK), n * TK)
        kt = kv_ref[0, rows, 2 * HEAD_DIM:3 * HEAD_DIM].astype(BF16)
        sc = jnp.zeros((TK, n * TK), F32)
        for h in range(A_IDX_HEADS):
            s = lax.dot_general(qi[:, h * A_IDX_DIM:(h + 1) * A_IDX_DIM], kt, NT, preferred_element_type=F32)
            sc = sc + jnp.concatenate([wi[h]] * n, axis=1) * jnp.maximum(s * A_IDX_DIM ** -0.5, 0.0)
        sc = sc * A_IDX_HEADS ** -0.5
        if causal:
            sc = jnp.where(lane <= row, sc, NEG)
        for t in range(n):
            key_scr[j0 + t] = _ordered_key(sc[:, t * TK:(t + 1) * TK])
    _flash_far_loop(i, score_tiles)
    score_tiles(i, 1, True)
    tau, cut = _topk_threshold(key_scr, nkt, TK, float(A_TOPK), DSA_IDX_BITS)

    q = (q_ref[0] * HEAD_DIM ** -0.5).astype(BF16)
    qh = _heads_to_rows(q, A_HEADS)
    _flash_init(m_scr, acc_scr)

    def attend(accumulate, j0, n, bias_cols=None, causal=False):
        rows = pl.ds(pl.multiple_of(j0 * TK, TK), n * TK)
        k = kv_ref[0, rows, 0:HEAD_DIM].astype(BF16)
        sels = [_in_topk(key_scr[j0 + t], j0 + t, tau, cut) for t in range(n)]
        if causal:
            sels = [s & (lane <= row) for s in sels]
        madd = jnp.concatenate([jnp.where(s, 0.0, NEG) for s in sels], axis=1)
        bias = None if bias_cols is None else band_ref[:, :, bias_cols]
        if accumulate:
            v1 = _with_ones(kv_ref[0, rows, HEAD_DIM:2 * HEAD_DIM].astype(BF16))
            _acc_step(m_scr, acc_scr, slice(0, A_HEADS), qh, k, v1, madd, bias)
        else:
            _max_step(m_scr, slice(0, A_HEADS), qh, k, madd, bias)

    _two_pass_attention(i, m_scr, attend)
    for h in range(A_HEADS):
        o_ref[0, h] = _flash_result(acc_scr, h)


def _dsa_prompt(proj, band_a):
    bn, t, _ = proj.shape
    nqb = t // TK
    return pl.pallas_call(
        _dsa_prompt_kernel,
        grid=(bn, nqb),
        in_specs=[pl.BlockSpec((1, TK, A_MIX), lambda b, i: (b, i, COL_QA // A_MIX)),
                  pl.BlockSpec((1, TK, 256), lambda b, i: (b, i, COL_QI // 256)),
                  pl.BlockSpec((1, TK, 256), lambda b, i: (b, i, COL_A // 256)),
                  _per_batch((1, t, 256), COL_A // 256),
                  _resident((A_HEADS, TK, 2 * TK))],
        out_specs=pl.BlockSpec((1, A_HEADS, TK, HEAD_DIM), lambda b, i: (b, 0, i, 0)),
        out_shape=jax.ShapeDtypeStruct((bn, A_HEADS, t, HEAD_DIM), F32),
        scratch_shapes=[pltpu.VMEM((nqb, TK, TK), jnp.int32),
                        pltpu.VMEM((A_HEADS, TK, TK), F32), pltpu.VMEM((A_HEADS, TK, 2 * HEAD_DIM), F32)],
        compiler_params=_params("parallel", "arbitrary"),
        name="dsa_prompt",
    )(proj, proj, proj, proj, band_a)


COL_QA, COL_QB, COL_QI, COL_A = 0, 512, 1024, 1280
COL_B, COL_WIN, EVEN_COLS = 1536, 2048, 2304
MISC_WI = 3 * HEAD_DIM
MISC_GL = MISC_WI + A_IDX_HEADS
DSA_IDX_BITS = 14


def _even_in_perm():
    offs = np.concatenate([[0], np.cumsum(EVEN_SPLITS)])
    seg = lambda k: np.arange(offs[k], offs[k + 1])
    (q_a, k_a, v_a, qi, ki, wi, q_b, kc, vc, ksl, vsl, kw, vw, gl) = [seg(k) for k in range(14)]
    order = [q_a, q_b, qi, k_a, v_a, ki, wi, gl]
    pad = EVEN_COLS - sum(EVEN_SPLITS)
    return np.concatenate(order), pad, np.concatenate([kc, vc, ksl, vsl, kw, vw])


def _permute_even_w_in(w):
    head, pad, tail = _even_in_perm()
    zeros = jnp.zeros(w.shape[:-1] + (pad,), w.dtype)
    return jnp.concatenate([w[..., head], zeros, w[..., tail]], axis=-1)


G_COLS = B_KV_HEADS * HEAD_DIM


def _compress_rows(x, pos_w, w1, w2):
    nb = x.shape[0] // B_BLOCK
    pooled = jnp.sum(x.reshape(nb, B_BLOCK, G_COLS) * pos_w[None], axis=1)
    hid = jax.nn.gelu(jnp.dot(pooled.astype(BF16), w1, preferred_element_type=F32))
    return jnp.dot(hid.astype(BF16), w2, preferred_element_type=F32)


def _compress_kernel(kv_ref, pos_ref, w1_ref, w2_ref, o_ref):
    for s in range(2):
        x = kv_ref[0, :, s * G_COLS:(s + 1) * G_COLS]
        o_ref[0, :, s * G_COLS:(s + 1) * G_COLS] = _compress_rows(x, pos_ref[s], w1_ref[s], w2_ref[s])


def _compress_weights(cmp_pos, cmp_w1, cmp_w2):
    pos = jnp.concatenate([cmp_pos] * B_KV_HEADS, axis=-1)
    eye = jnp.eye(B_KV_HEADS, dtype=cmp_w1.dtype)
    bd = lambda w: jnp.einsum('gh,sde->sgdhe', eye, w).reshape(2, G_COLS, G_COLS).astype(BF16)
    return pos, bd(cmp_w1), bd(cmp_w2)


def _compress_prompt(proj, pos, w1, w2):
    bn, t, _ = proj.shape
    tr = min(t, 2048)
    return pl.pallas_call(
        _compress_kernel,
        grid=(bn, t // tr),
        in_specs=[pl.BlockSpec((1, tr, 2 * G_COLS), lambda b, i: (b, i, COL_B // (2 * G_COLS))),
                  _resident((2, B_BLOCK, G_COLS)), _resident((2, G_COLS, G_COLS)), _resident((2, G_COLS, G_COLS))],
        out_specs=pl.BlockSpec((1, tr // B_BLOCK, 2 * G_COLS), lambda b, i: (b, i, 0)),
        out_shape=jax.ShapeDtypeStruct((bn, t // B_BLOCK, 2 * G_COLS), F32),
        compiler_params=_params("parallel", "parallel"),
        name="nsa_compress",
    )(proj, pos, w1, w2)


B_R = B_HEADS // B_KV_HEADS
NSA_WIN_TILES = B_WINDOW // TK


def _softmax_pieces(pieces):
    m = functools.reduce(jnp.maximum, [jnp.max(x, axis=1, keepdims=True) for x in pieces])
    ps = [jnp.exp(x - m) for x in pieces]
    den = functools.reduce(lambda a, b: a + b, [jnp.sum(p, axis=1, keepdims=True) for p in ps])
    return ps, den, m


def _select_blocks(imp, n_sel, lanef=None):
    if lanef is None:
        lanef = lax.broadcasted_iota(jnp.int32, imp.shape, 1).astype(F32)
    sel = jnp.zeros(imp.shape, F32)
    for _ in range(n_sel):
        mx = jnp.max(imp, axis=1, keepdims=True)
        first = jnp.min(jnp.where(imp == mx, lanef, float(2 ** 24)), axis=1, keepdims=True)
        hit = lanef == first
        sel = jnp.where(hit, 1.0, sel)
        imp = jnp.where(hit, -jnp.inf, imp)
    return sel


def _nsa_prompt_kernel(q_ref, misc_ref, kv_ref, win_ref, cmp_ref, band_ref, cband_ref, o_ref,
                       selm_scr, m_scr, acc_scr):
    i = pl.program_id(1)
    n_blk = cmp_ref.shape[1]
    row = lax.broadcasted_iota(jnp.int32, (TK, TK), 0)
    lane = lax.broadcasted_iota(jnp.int32, (TK, TK), 1)
    tpos = i * TK + row
    q = (q_ref[0] * HEAD_DIM ** -0.5).astype(BF16)
    gates = jax.nn.sigmoid(misc_ref[0][:, MISC_GL:MISC_GL + 3 * B_HEADS])
    qg = [_heads_to_rows(q[:, g * B_R * HEAD_DIM:(g + 1) * B_R * HEAD_DIM], B_R) for g in range(B_KV_HEADS)]

    blk = lax.broadcasted_iota(jnp.int32, (TK, n_blk), 1)
    brow = lax.broadcasted_iota(jnp.int32, (TK, n_blk), 0)
    bpos = i * TK + brow
    valid_c = (blk + 1) * B_BLOCK - 1 <= bpos
    cur = bpos // B_BLOCK
    forced = jnp.where((blk == 0) | (blk == cur) | (blk == cur - 1), FORCE_SCORE, 0.0)
    o_cmp, imps = [], []
    for g in range(B_KV_HEADS):
        kc = cmp_ref[0, :, g * HEAD_DIM:(g + 1) * HEAD_DIM].astype(BF16)
        vc = cmp_ref[0, :, G_COLS + g * HEAD_DIM:G_COLS + (g + 1) * HEAD_DIM].astype(BF16)
        lc = lax.dot_general(qg[g], kc, NT, preferred_element_type=F32).reshape(B_R, TK, n_blk)
        bias = []
        for r in range(B_R):
            cb = cband_ref[g * B_R + r]
            b = jnp.zeros((TK, n_blk), F32)
            for u in range(4):
                b = jnp.where(blk == 2 * i + u - 2, cb[:, u:u + 1], b)
            bias.append(b)
        lc = jnp.where(valid_c[None], lc + jnp.stack(bias), NEG).reshape(B_R * TK, n_blk)
        (p,), den, _ = _softmax_pieces([lc])
        pc = ((p / den).reshape(B_R, TK, n_blk) * jnp.where(valid_c, 1.0, 0.0)[None])
        o_cmp.append(jnp.dot(pc.reshape(B_R * TK, n_blk).astype(BF16), vc, preferred_element_type=F32))
        imps.append(jnp.where(blk <= cur, jnp.sum(pc, axis=0) + forced, NEG))
    selm = _select_blocks(jnp.concatenate(imps, axis=0), min(B_TOPN, n_blk))
    for g in range(B_KV_HEADS):
        selm_scr[g] = selm[g * TK:(g + 1) * TK].astype(BF16)

    _flash_init(m_scr, acc_scr)

    def attend(accumulate, j0, n, bias_cols=None, causal=False):
        rows = pl.ds(pl.multiple_of(j0 * TK, TK), n * TK)
        ebk = lax.broadcasted_iota(jnp.int32, (n_blk, n * TK), 0)
        etok = j0 * TK + lax.broadcasted_iota(jnp.int32, (n_blk, n * TK), 1)
        expand = jnp.where(ebk == etok // B_BLOCK, 1.0, 0.0).astype(BF16)
        for g in range(B_KV_HEADS):
            k = kv_ref[0, rows, g * HEAD_DIM:(g + 1) * HEAD_DIM].astype(BF16)
            sel = jnp.dot(selm_scr[g], expand, preferred_element_type=F32) > 0.5
            if causal:
                sel = sel & (lane <= row)
            madd = jnp.where(sel, 0.0, NEG)
            heads = slice(g * B_R, (g + 1) * B_R)
            bias = None if bias_cols is None else band_ref[heads, :, bias_cols]
            if accumulate:
                v1 = _with_ones(kv_ref[0, rows, G_COLS + g * HEAD_DIM:G_COLS + (g + 1) * HEAD_DIM].astype(BF16))
                _acc_step(m_scr, acc_scr, heads, qg[g], k, v1, madd, bias)
            else:
                _max_step(m_scr, heads, qg[g], k, madd, bias)

    _two_pass_attention(i, m_scr, attend)

    for g in range(B_KV_HEADS):
        tiles = []
        for d in range(NSA_WIN_TILES + 1):
            rows = pl.ds(pl.multiple_of(jnp.maximum(i - d, 0) * TK, TK), TK)
            tiles.append((win_ref[0, rows, g * HEAD_DIM:(g + 1) * HEAD_DIM].astype(BF16),
                          win_ref[0, rows, G_COLS + g * HEAD_DIM:G_COLS + (g + 1) * HEAD_DIM].astype(BF16),
                          i - d >= 0))
        o_win = _window_attend(qg[g], B_R, TK, tiles, band_ref, g * B_R)
        for r in range(B_R):
            h = g * B_R + r
            rr = slice(r * TK, (r + 1) * TK)
            o_ref[0, h] = (gates[:, 3 * h:3 * h + 1] * o_cmp[g][rr]
                           + gates[:, 3 * h + 1:3 * h + 2] * _flash_result(acc_scr, h)
                           + gates[:, 3 * h + 2:3 * h + 3] * o_win[rr])


def _cmp_band(dist_bias):
    a = np.arange(TK)[:, None]
    u = np.arange(4)[None, :] - 2
    idx = np.clip(a - B_BLOCK * u - (B_BLOCK - 1), 0, 2 * TK - 1)
    return jnp.transpose(dist_bias[idx], (2, 0, 1))


def _nsa_prompt(proj, cmp_kv, band_b, cband_b):
    bn, t, _ = proj.shape
    nqb = t // TK
    n_blk = cmp_kv.shape[1]
    return pl.pallas_call(
        _nsa_prompt_kernel,
        grid=(bn, nqb),
        in_specs=[pl.BlockSpec((1, TK, B_MIX), lambda b, i: (b, i, COL_QB // B_MIX)),
                  pl.BlockSpec((1, TK, 256), lambda b, i: (b, i, COL_A // 256)),
                  _per_batch((1, t, 2 * G_COLS), (COL_B + 2 * G_COLS) // (2 * G_COLS)),
                  _per_batch((1, t, 2 * G_COLS), COL_WIN // (2 * G_COLS)),
                  _per_batch((1, n_blk, 2 * G_COLS), 0),
                  _resident((B_HEADS, TK, 2 * TK)), _resident((B_HEADS, TK, 4))],
        out_specs=pl.BlockSpec((1, B_HEADS, TK, HEAD_DIM), lambda b, i: (b, 0, i, 0)),
        out_shape=jax.ShapeDtypeStruct((bn, B_HEADS, t, HEAD_DIM), F32),
        scratch_shapes=[pltpu.VMEM((B_KV_HEADS, TK, n_blk), BF16),
                        pltpu.VMEM((B_HEADS, TK, TK), F32), pltpu.VMEM((B_HEADS, TK, 2 * HEAD_DIM), F32)],
        compiler_params=_params("parallel", "arbitrary"),
        name="nsa_prompt",
    )(proj, proj, proj, proj, cmp_kv, band_b, cband_b)


C_R = C_HEADS // C_KV_HEADS
COL_QC, COL_CWIN = 0, C_MIX


def _swa_prompt_kernel(q_ref, kv_ref, band_ref, sink_ref, o_ref):
    i = pl.program_id(1)
    row = lax.broadcasted_iota(jnp.int32, (TK, TK), 0)
    lane = lax.broadcasted_iota(jnp.int32, (TK, TK), 1)
    q = (q_ref[0] * HEAD_DIM ** -0.5).astype(BF16)
    prev_rows = pl.ds(pl.multiple_of(jnp.maximum(i - 1, 0) * TK, TK), TK)
    own_rows = pl.ds(pl.multiple_of(i * TK, TK), TK)
    prev_ok = jnp.broadcast_to(i >= 1, (TK, TK)) & (lane > row)
    for g in range(C_KV_HEADS):
        qg = _heads_to_rows(q[:, g * C_R * HEAD_DIM:(g + 1) * C_R * HEAD_DIM], C_R)
        pieces, vals = [], []
        for rows, ok, cols in ((prev_rows, prev_ok, slice(0, TK)), (own_rows, lane <= row, slice(TK, 2 * TK))):
            k = kv_ref[0, rows, g * HEAD_DIM:(g + 1) * HEAD_DIM].astype(BF16)
            vals.append(kv_ref[0, rows, G_COLS + g * HEAD_DIM:G_COLS + (g + 1) * HEAD_DIM].astype(BF16))
            lg = lax.dot_general(qg, k, NT, preferred_element_type=F32).reshape(C_R, TK, TK)
            lg = lg + band_ref[g * C_R:(g + 1) * C_R, :, cols]
            pieces.append(jnp.where(ok[None], lg, NEG).reshape(C_R * TK, TK))
        sink = jnp.concatenate([jnp.broadcast_to(sink_ref[g * C_R + r], (TK, 1)) for r in range(C_R)], axis=0)
        m = jnp.maximum(jnp.maximum(jnp.max(pieces[0], axis=1, keepdims=True),
                                    jnp.max(pieces[1], axis=1, keepdims=True)), sink)
        ps = [jnp.exp(x - m) for x in pieces]
        den = jnp.sum(ps[0], axis=1, keepdims=True) + jnp.sum(ps[1], axis=1, keepdims=True) + jnp.exp(sink - m)
        o = (jnp.dot(ps[0].astype(BF16), vals[0], preferred_element_type=F32)
             + jnp.dot(ps[1].astype(BF16), vals[1], preferred_element_type=F32)) / den
        o_ref[0, g * C_R:(g + 1) * C_R] = o.reshape(C_R, TK, HEAD_DIM)


def _swa_prompt(proj, band_c, sinks):
    bn, t, _ = proj.shape
    return pl.pallas_call(
        _swa_prompt_kernel,
        grid=(bn, t // TK),
        in_specs=[pl.BlockSpec((1, TK, C_MIX), lambda b, i: (b, i, COL_QC // C_MIX)),
                  _per_batch((1, t, 2 * G_COLS), COL_CWIN // (2 * G_COLS)),
                  _resident((C_HEADS, TK, 2 * TK)), _resident((C_HEADS, 1, 1))],
        out_specs=pl.BlockSpec((1, C_HEADS, TK, HEAD_DIM), lambda b, i: (b, 0, i, 0)),
        out_shape=jax.ShapeDtypeStruct((bn, C_HEADS, t, HEAD_DIM), F32),
        compiler_params=_params("parallel", "arbitrary"),
        name="swa_prompt",
    )(proj, proj, band_c, sinks)


def _cross_attn_kernel(q_ref, mem_ref, o_ref):
    q = (q_ref[0] * X_HEAD_DIM ** -0.5).astype(BF16)
    for h in range(X_HEADS):
        cols = slice(h * X_HEAD_DIM, (h + 1) * X_HEAD_DIM)
        k = mem_ref[0, :, cols].astype(BF16)
        v = mem_ref[0, :, X_WIDTH + h * X_HEAD_DIM:X_WIDTH + (h + 1) * X_HEAD_DIM].astype(BF16)
        s = lax.dot_general(q[:, cols], k, NT, preferred_element_type=F32)
        p = jnp.exp(s - jnp.max(s, axis=1, keepdims=True))
        den = jnp.sum(p, axis=1, keepdims=True)
        o_ref[0, :, cols] = jnp.dot(p.astype(BF16), v, preferred_element_type=F32) / den


def _cross_attn(q, mem):
    bn, t, _ = q.shape
    n_mem = mem.shape[1]
    tq = min(t, 512)
    return pl.pallas_call(
        _cross_attn_kernel,
        grid=(bn, t // tq),
        in_specs=[pl.BlockSpec((1, tq, X_WIDTH), lambda b, i: (b, i, 0)),
                  pl.BlockSpec((1, n_mem, 2 * X_WIDTH), lambda b, i: (b, 0, 0))],
        out_specs=pl.BlockSpec((1, tq, X_WIDTH), lambda b, i: (b, i, 0)),
        out_shape=jax.ShapeDtypeStruct((bn, t, X_WIDTH), F32),
        compiler_params=_params("parallel", "arbitrary"),
        name="cross_attn",
    )(q, mem)


def _heads_out_ln_kernel(*refs):
    o_refs, (w_ref, h_ref, g_ref, b_ref, out_ref) = refs[:-5], refs[-5:]
    y = jnp.zeros(h_ref.shape[1:], F32)
    hh = 0
    for o_ref in o_refs:
        for h in range(o_ref.shape[1]):
            y = y + jnp.dot(o_ref[0, h].astype(BF16), w_ref[hh], preferred_element_type=F32)
            hh += 1
    out_ref[0] = _layer_norm_rows(ALPHA * h_ref[0] + y, g_ref[...], b_ref[...])


def _heads_out_ln(outs, w, h, g, b):
    bn, t, _ = h.shape
    tm = min(t, 512)
    return pl.pallas_call(
        _heads_out_ln_kernel,
        grid=(bn, t // tm),
        in_specs=[pl.BlockSpec((1, o.shape[1], tm, HEAD_DIM), lambda b, i: (b, 0, i, 0)) for o in outs]
        + [_resident(w.shape), pl.BlockSpec((1, tm, D_MODEL), lambda b, i: (b, i, 0)),
           _resident((1, D_MODEL)), _resident((1, D_MODEL))],
        out_specs=pl.BlockSpec((1, tm, D_MODEL), lambda b, i: (b, i, 0)),
        out_shape=jax.ShapeDtypeStruct((bn, t, D_MODEL), F32),
        compiler_params=_params("parallel", "parallel"),
        name="heads_out_ln",
    )(*outs, w, h, g.reshape(1, D_MODEL), b.reshape(1, D_MODEL))


QS = 8


def _own_tile(new_rows):
    return jnp.concatenate([new_rows, jnp.zeros((TK - new_rows.shape[0], new_rows.shape[1]), F32)], axis=0)


def _topk_threshold_dense(keys, pos, k, idx_bits):
    rows = keys.shape[1]
    count = lambda pred: jnp.sum(jnp.sum(jnp.where(pred, 1.0, 0.0), axis=0), axis=1, keepdims=True)

    def value_bits(it, ans):
        c1, c2, c3 = [ans + lax.shift_left(jnp.int32(m), 30 - 2 * it) for m in (1, 2, 3)]
        n1, n2, n3 = count(keys >= c1), count(keys >= c2), count(keys >= c3)
        return jnp.where(n3 >= k, c3, jnp.where(n2 >= k, c2, jnp.where(n1 >= k, c1, ans)))
    tau = lax.fori_loop(0, 16, value_bits, jnp.full((rows, 1), INT_MIN, jnp.int32))

    def cut_among_ties():
        need = k - count(keys > tau)
        tie_pos = jnp.where(keys == tau, pos, jnp.int32(2 ** 30))

        def index_bit(it, cut):
            cand = cut | lax.shift_left(jnp.int32(1), idx_bits - 1 - it)
            return jnp.where(count(tie_pos < cand) < need, cand, cut)
        return lax.fori_loop(0, idx_bits, index_bit, jnp.zeros((rows, 1), jnp.int32))

    cut = lax.cond(jnp.max(count(keys >= tau)) > k, cut_among_ties,
                   lambda: jnp.full((rows, 1), 2 ** 30, jnp.int32))
    return tau, cut


def _tile_attention(lg_ref, madd, band, n_h, vt_ref, own_v):
    nt = lg_ref.shape[0]
    lg = lg_ref[...].reshape(nt, n_h, QS, TK) + madd[:, None]
    far = lg[:nt - 2]
    near = lg[nt - 2:] + jnp.stack([band[:, :, 0:TK], band[:, :, TK:2 * TK]])
    m = jnp.maximum(jnp.max(jnp.max(far, axis=0), axis=-1, keepdims=True),
                    jnp.max(jnp.max(near, axis=0), axis=-1, keepdims=True))
    p_far = jnp.exp(far - m[None])
    p_near = jnp.exp(near - m[None])
    den = (jnp.sum(jnp.sum(p_far, axis=0), axis=-1, keepdims=True)
           + jnp.sum(jnp.sum(p_near, axis=0), axis=-1, keepdims=True))
    p = jnp.concatenate([p_far, p_near], axis=0).reshape(nt, n_h * QS, TK).astype(BF16)
    n_steps, _, width = vt_ref.shape
    pps = width // TK
    acc = jnp.dot(p[nt - 1], own_v, preferred_element_type=F32)
    for s in range(n_steps):
        chunk = jnp.concatenate([p[s * pps + c] for c in range(pps)], axis=1)
        acc = acc + lax.dot_general(chunk, vt_ref[s], NT, preferred_element_type=F32)
    return acc / den.reshape(n_h * QS, 1)


def _pages_per_step(n_pages):
    return next(c for c in (16, 8, 4, 2, 1) if n_pages % c == 0)


def _page_specs(block, layer, n_pages, pps):
    tail = (0,) * (len(block) - 2)
    return [pl.BlockSpec(block, lambda b, p, tbl, k=k: (layer, tbl[b * n_pages + p * pps + k]) + tail)
            for k in range(pps)]


def _indexer_scores(qi, wi, s):
    s = s.reshape(A_IDX_HEADS, QS, s.shape[1])
    sc = jnp.zeros(s.shape[1:], F32)
    for h in range(A_IDX_HEADS):
        sc = sc + wi[:, h:h + 1] * jnp.maximum(s[h] * A_IDX_DIM ** -0.5, 0.0)
    return sc * A_IDX_HEADS ** -0.5


def _tile_positions(nt):
    tile = lax.broadcasted_iota(jnp.int32, (nt, QS, TK), 0)
    lane = lax.broadcasted_iota(jnp.int32, (nt, QS, TK), 2)
    row = lax.broadcasted_iota(jnp.int32, (nt, QS, TK), 1)
    pos = tile * TK + lane
    return pos, pos <= (nt - 1) * TK + row


def _dsa_sample_kernel(tbl_ref, q_ref, *refs):
    pps = len(refs) - 5
    page_refs, (band_ref, o_ref, key_scr, lg_scr, vt_scr) = refs[:pps], refs[pps:]
    p = pl.program_id(1)
    nt = key_scr.shape[0]
    x = q_ref[0]
    qi = _heads_to_rows(x[:, COL_QI:COL_QI + A_IDX_HEADS * A_IDX_DIM].astype(BF16), A_IDX_HEADS)
    wi = x[:, COL_A + MISC_WI:COL_A + MISC_WI + A_IDX_HEADS]
    qh = _heads_to_rows((x[:, COL_QA:COL_QA + A_MIX] * HEAD_DIM ** -0.5).astype(BF16), A_HEADS)
    for c, page_ref in enumerate(page_refs):
        lp = p * pps + c
        key_scr[lp] = _ordered_key(_indexer_scores(
            qi, wi, jnp.dot(qi, page_ref[0, 0, 2].astype(BF16), preferred_element_type=F32)))
        lg_scr[lp] = jnp.dot(qh, page_ref[0, 0, 0].astype(BF16), preferred_element_type=F32)
        vt_scr[p, :, c * TK:(c + 1) * TK] = page_ref[0, 0, 1].astype(BF16)

    @pl.when(p == pl.num_programs(1) - 1)
    def _():
        own = _own_tile(x[:, COL_A:COL_A + 3 * HEAD_DIM]).astype(BF16)
        pos, causal = _tile_positions(nt)
        sc = _indexer_scores(qi, wi, lax.dot_general(qi, own[:, 2 * HEAD_DIM:3 * HEAD_DIM], NT,
                                                     preferred_element_type=F32))
        key_scr[nt - 1] = _ordered_key(jnp.where(causal[nt - 1], sc, NEG))
        lg_scr[nt - 1] = lax.dot_general(qh, own[:, 0:HEAD_DIM], NT, preferred_element_type=F32)
        keys = key_scr[...]
        tau, cut = _topk_threshold_dense(keys, pos, float(A_TOPK), DSA_IDX_BITS)
        sel = ((keys > tau) | ((keys == tau) & (pos <= cut))) & causal
        o = _tile_attention(lg_scr, jnp.where(sel, 0.0, NEG), band_ref[:, 0:QS, :], A_HEADS, vt_scr,
                            own[:, HEAD_DIM:2 * HEAD_DIM])
        o_ref[0] = o.reshape(A_HEADS, QS, HEAD_DIM)


def _dsa_sample(proj, pool_t, page_table, layer, band_a):
    ns = proj.shape[0]
    n_pages = page_table.shape[1]
    nt = n_pages + 1
    pps = _pages_per_step(n_pages)
    grid_spec = pltpu.PrefetchScalarGridSpec(
        num_scalar_prefetch=1, grid=(ns, n_pages // pps),
        in_specs=[pl.BlockSpec((1, QS, EVEN_COLS), lambda b, p, tbl: (b, 0, 0))]
        + _page_specs((1, 1, 3, HEAD_DIM, PAGE_SIZE), layer, n_pages, pps)
        + [pl.BlockSpec((A_HEADS, TK, 2 * TK), lambda b, p, tbl: (0, 0, 0))],
        out_specs=pl.BlockSpec((1, A_HEADS, QS, HEAD_DIM), lambda b, p, tbl: (b, 0, 0, 0)),
        scratch_shapes=[pltpu.VMEM((nt, QS, TK), jnp.int32), pltpu.VMEM((nt, A_HEADS * QS, TK), F32),
                        pltpu.VMEM((n_pages // pps, HEAD_DIM, pps * TK), BF16)])
    return pl.pallas_call(
        _dsa_sample_kernel, grid_spec=grid_spec,
        out_shape=jax.ShapeDtypeStruct((ns, A_HEADS, QS, HEAD_DIM), F32),
        compiler_params=_params("parallel", "arbitrary"),
        name="dsa_sample",
    )(page_table.reshape(-1), proj, *([pool_t] * pps), band_a)


def _nsa_sample_kernel(tbl_ref, q_ref, *refs):
    pps = len(refs) - 11
    page_refs = refs[:pps]
    (win_ref, pos_ref, w1_ref, w2_ref, band_ref, cband_ref, expand_ref,
     o_ref, lg_scr, vt_scr, pool_scr) = refs[pps:]
    p = pl.program_id(1)
    nt = lg_scr.shape[1]
    nbl = pool_scr.shape[-1]
    x = q_ref[0]
    q = (x[:, COL_QB:COL_QB + B_MIX] * HEAD_DIM ** -0.5).astype(BF16)
    qg = [_heads_to_rows(q[:, g * B_R * HEAD_DIM:(g + 1) * B_R * HEAD_DIM], B_R) for g in range(B_KV_HEADS)]

    @pl.when(p == 0)
    def _():
        pool_scr[...] = jnp.zeros(pool_scr.shape, F32)

    ptok = lax.broadcasted_iota(jnp.int32, (TK, nbl), 0)
    pblk = lax.broadcasted_iota(jnp.int32, (TK, nbl), 1)
    for c, page_ref in enumerate(page_refs):
        lp = p * pps + c
        place = jnp.where(pblk == 2 * lp + ptok // B_BLOCK, 1.0, 0.0).astype(BF16)
        for g in range(B_KV_HEADS):
            for s in range(2):
                weighted = (page_ref[0, 0, s, g] * pos_ref[s]).astype(BF16)
                pool_scr[s, g] += jnp.dot(weighted, place, preferred_element_type=F32)
            lg_scr[g, lp] = jnp.dot(qg[g], page_ref[0, 0, 2, g].astype(BF16), preferred_element_type=F32)
            vt_scr[g, p, :, c * TK:(c + 1) * TK] = page_ref[0, 0, 3, g].astype(BF16)

    @pl.when(p == pl.num_programs(1) - 1)
    def _():
        past = (nt - 1) * TK
        n_blk = past // B_BLOCK
        nb_w = expand_ref.shape[0]
        own = _own_tile(x[:, COL_B + 2 * G_COLS:COL_B + 4 * G_COLS]).astype(BF16)
        gates = jax.nn.sigmoid(x[:, COL_A + MISC_GL:COL_A + MISC_GL + 3 * B_HEADS])
        blk = lax.broadcasted_iota(jnp.int32, (QS, nb_w), 1)
        tpos = past + lax.broadcasted_iota(jnp.int32, (QS, nb_w), 0)
        cur = tpos // B_BLOCK
        valid_c = ((blk + 1) * B_BLOCK - 1 <= tpos)[:, :nbl]
        forced = jnp.where((blk == 0) | (blk == cur) | (blk == cur - 1), FORCE_SCORE, 0.0)
        i_own = past // TK
        _, causal = _tile_positions(nt)
        own_win = _own_tile(x[:, COL_WIN:COL_WIN + 2 * G_COLS]).astype(BF16)
        o_cmps, imps = [], []
        for g in range(B_KV_HEADS):
            cmp_t = []
            for s in range(2):
                hid = jax.nn.gelu(jnp.dot(w1_ref[s], pool_scr[s, g].astype(BF16), preferred_element_type=F32))
                cmp_t.append(jnp.dot(w2_ref[s], hid.astype(BF16), preferred_element_type=F32).astype(BF16))
            lc = jnp.dot(qg[g], cmp_t[0], preferred_element_type=F32).reshape(B_R, QS, nbl)
            bias = []
            for r in range(B_R):
                cb = cband_ref[g * B_R + r][0:QS, :]
                b = jnp.zeros((QS, nbl), F32)
                for u in range(4):
                    b = jnp.where(blk[:, :nbl] == 2 * i_own + u - 2, cb[:, u:u + 1], b)
                bias.append(b)
            lc = jnp.where(valid_c[None], lc + jnp.stack(bias), NEG).reshape(B_R * QS, nbl)
            (pr,), den, _ = _softmax_pieces([lc])
            pc = (pr / den).reshape(B_R, QS, nbl) * jnp.where(valid_c, 1.0, 0.0)[None]
            o_cmps.append(lax.dot_general(pc.reshape(B_R * QS, nbl).astype(BF16), cmp_t[1], NT,
                                          preferred_element_type=F32))
            imp = jnp.concatenate([jnp.sum(pc, axis=0), jnp.zeros((QS, nb_w - nbl), F32)], axis=1)
            imps.append(jnp.where(blk <= cur, imp + forced, jnp.where(blk <= n_blk, NEG, -jnp.inf)))
        selm = _select_blocks(jnp.concatenate(imps, axis=0), B_TOPN)
        for g in range(B_KV_HEADS):
            gc = slice(g * HEAD_DIM, (g + 1) * HEAD_DIM)
            o_cmp = o_cmps[g]
            sel = jnp.dot(selm[g * QS:(g + 1) * QS].astype(BF16), expand_ref[...],
                          preferred_element_type=F32)
            madd = jnp.stack([jnp.where((sel[:, t * TK:(t + 1) * TK] > 0.5) & causal[t], 0.0, NEG)
                              for t in range(nt)])
            lg_scr[g, nt - 1] = lax.dot_general(qg[g], own[:, gc], NT, preferred_element_type=F32)
            o_sel = _tile_attention(lg_scr.at[g], madd, band_ref[g * B_R:(g + 1) * B_R, 0:QS, :], B_R, vt_scr.at[g],
                                    own[:, G_COLS + g * HEAD_DIM:G_COLS + (g + 1) * HEAD_DIM])
            tiles = [(own_win[:, gc], own_win[:, G_COLS + g * HEAD_DIM:G_COLS + (g + 1) * HEAD_DIM], True)]
            for d in range(1, NSA_WIN_TILES + 1):
                wr = slice((NSA_WIN_TILES - d) * TK, (NSA_WIN_TILES - d + 1) * TK)
                tiles.append((win_ref[0, wr, gc].astype(BF16),
                              win_ref[0, wr, G_COLS + g * HEAD_DIM:G_COLS + (g + 1) * HEAD_DIM].astype(BF16), True))
            o_win = _window_attend(qg[g], B_R, QS, tiles, band_ref, g * B_R)
            for r in range(B_R):
                h = g * B_R + r
                rr = slice(r * QS, (r + 1) * QS)
                o_ref[0, h] = (gates[:, 3 * h:3 * h + 1] * o_cmp[rr] + gates[:, 3 * h + 1:3 * h + 2] * o_sel[rr]
                               + gates[:, 3 * h + 2:3 * h + 3] * o_win[rr])


def _round_up(n, m):
    return -(-n // m) * m


def _sample_expand(n_pages):
    nb_w = _round_up(2 * n_pages + 1, LANES)
    tok_blk = np.arange((n_pages + 1) * TK) // B_BLOCK
    return jnp.asarray(np.arange(nb_w)[:, None] == tok_blk[None, :], BF16)


def _compress_weights_t(cmp_pos, cmp_w1, cmp_w2):
    pos_t = jnp.concatenate([jnp.swapaxes(cmp_pos, 1, 2)] * (PAGE_SIZE // B_BLOCK), axis=2)
    return pos_t, jnp.swapaxes(cmp_w1, 1, 2).astype(BF16), jnp.swapaxes(cmp_w2, 1, 2).astype(BF16)


def _nsa_sample(proj, pool_t, win, page_table, layer, cmp_w, band_b, cband_b, expand):
    ns = proj.shape[0]
    n_pages = page_table.shape[1]
    nt = n_pages + 1
    nbl = _round_up(2 * n_pages, LANES)
    const = lambda shape: pl.BlockSpec(shape, lambda b, p, tbl: (0,) * len(shape))
    pps = _pages_per_step(n_pages)
    grid_spec = pltpu.PrefetchScalarGridSpec(
        num_scalar_prefetch=1, grid=(ns, n_pages // pps),
        in_specs=[pl.BlockSpec((1, QS, EVEN_COLS), lambda b, p, tbl: (b, 0, 0))]
        + _page_specs((1, 1, 4, B_KV_HEADS, HEAD_DIM, PAGE_SIZE), layer, n_pages, pps)
        + [pl.BlockSpec((1, B_WINDOW, 2 * G_COLS), lambda b, p, tbl: (b, 0, 0)),
           const((2, HEAD_DIM, PAGE_SIZE)), const((2, HEAD_DIM, HEAD_DIM)), const((2, HEAD_DIM, HEAD_DIM)),
           const((B_HEADS, TK, 2 * TK)), const((B_HEADS, TK, 4)), const(expand.shape)],
        out_specs=pl.BlockSpec((1, B_HEADS, QS, HEAD_DIM), lambda b, p, tbl: (b, 0, 0, 0)),
        scratch_shapes=[pltpu.VMEM((B_KV_HEADS, nt, B_R * QS, TK), F32),
                        pltpu.VMEM((B_KV_HEADS, n_pages // pps, HEAD_DIM, pps * TK), BF16),
                        pltpu.VMEM((2, B_KV_HEADS, HEAD_DIM, nbl), F32)])
    return pl.pallas_call(
        _nsa_sample_kernel, grid_spec=grid_spec,
        out_shape=jax.ShapeDtypeStruct((ns, B_HEADS, QS, HEAD_DIM), F32),
        compiler_params=_params("parallel", "arbitrary"),
        name="nsa_sample",
    )(page_table.reshape(-1), proj, *([pool_t] * pps), win, *cmp_w, band_b, cband_b, expand)


def _swa_sample_kernel(q_ref, buf_ref, band_ref, sink_ref, o_ref):
    x = q_ref[0]
    q = (x[:, COL_QC:COL_QC + C_MIX] * HEAD_DIM ** -0.5).astype(BF16)
    own = _own_tile(x[:, COL_CWIN:COL_CWIN + 2 * G_COLS]).astype(BF16)
    for g in range(C_KV_HEADS):
        gc = slice(g * HEAD_DIM, (g + 1) * HEAD_DIM)
        vc = slice(G_COLS + g * HEAD_DIM, G_COLS + (g + 1) * HEAD_DIM)
        qg = _heads_to_rows(q[:, g * C_R * HEAD_DIM:(g + 1) * C_R * HEAD_DIM], C_R)
        tiles = [(own[:, gc], own[:, vc], True),
                 (buf_ref[0, :, gc].astype(BF16), buf_ref[0, :, vc].astype(BF16), True)]
        sink = jnp.concatenate([jnp.broadcast_to(sink_ref[g * C_R + r], (QS, 1)) for r in range(C_R)], axis=0)
        o = _window_attend(qg, C_R, QS, tiles, band_ref, g * C_R, sink)
        o_ref[0, g * C_R:(g + 1) * C_R] = o.reshape(C_R, QS, HEAD_DIM)


def _swa_sample(proj, buf, band_c, sinks):
    ns = proj.shape[0]
    return pl.pallas_call(
        _swa_sample_kernel,
        grid=(ns,),
        in_specs=[pl.BlockSpec((1, QS, C_MIX + 2 * G_COLS), lambda b: (b, 0, 0)),
                  pl.BlockSpec((1, C_WINDOW, 2 * G_COLS), lambda b: (b, 0, 0)),
                  _resident((C_HEADS, TK, 2 * TK)), _resident((C_HEADS, 1, 1))],
        out_specs=pl.BlockSpec((1, C_HEADS, QS, HEAD_DIM), lambda b: (b, 0, 0, 0)),
        out_shape=jax.ShapeDtypeStruct((ns, C_HEADS, QS, HEAD_DIM), F32),
        compiler_params=_params("parallel"),
        name="swa_sample",
    )(proj, buf, band_c, sinks)


def _trunk_prompt(x, mem_kv, w):
    bn, t, _ = x.shape
    assert t % TK == 0 and t >= 4 * A_TOPK
    n = bn * t
    a_rows, b_rows, b_states, c_states = [], [], [], []
    kb, kc = min(B_WINDOW, t), min(C_WINDOW, t)
    h = x.reshape(n, D_MODEL)
    for l in range(DEPTH):
        g, bb = w['ln_g'][l], w['ln_b'][l]
        h = _ffn_ln(h, w['ffn_wg'][l, 0], w['ffn_wu'][l, 0], w['ffn_wd'][l, 0], g[0], bb[0])
        if l % 2 == 0:
            e = l // 2
            proj = _linear(h, w['even_w_in_perm'][e]).reshape(bn, t, EVEN_COLS)
            cmp_kv = _compress_prompt(proj, *w['cmp'][e])
            o_a = _dsa_prompt(proj, w['band'][:A_HEADS])
            o_b = _nsa_prompt(proj, cmp_kv, w['band'][A_HEADS:], w['cband'][A_HEADS:])
            h = _heads_out_ln([o_a, o_b], w['even_w_out_heads'][e], h.reshape(bn, t, D_MODEL), g[1], bb[1])
            a_rows.append(proj[:, :, COL_A:COL_A + 3 * HEAD_DIM].reshape(bn, t, 3, HEAD_DIM))
            b_rows.append(proj[:, :, COL_B:COL_B + 4 * G_COLS].reshape(bn, t, 4, B_KV_HEADS, HEAD_DIM))
            b_states.append(proj[:, t - kb:, COL_WIN:COL_WIN + 2 * G_COLS].reshape(bn, kb, 2, B_KV_HEADS, HEAD_DIM))
        else:
            oi = l // 2
            proj = _linear(h, w['odd_w_in'][oi]).reshape(bn, t, C_MIX + 2 * G_COLS)
            o_c = _swa_prompt(proj, w['band'], w['sinks_shifted'][oi])
            h = _heads_out_ln([o_c], w['odd_w_out_heads'][oi], h.reshape(bn, t, D_MODEL), g[1], bb[1])
            c_states.append(proj[:, t - kc:, COL_CWIN:].reshape(bn, kc, 2, C_KV_HEADS, HEAD_DIM))
        h = h.reshape(n, D_MODEL)
        xq = _linear(h, w['x_wq'][l]).reshape(bn, t, X_WIDTH)
        xo = _cross_attn(xq, mem_kv[l].reshape(bn, -1, 2 * X_WIDTH)).reshape(n, X_WIDTH)
        h = _linear_res_ln(xo, w['x_wo'][l], h, g[2], bb[2])
        h = _ffn_ln(h, w['ffn_wg'][l, 1], w['ffn_wu'][l, 1], w['ffn_wd'][l, 1], g[3], bb[3])
    return (h.reshape(bn, t, D_MODEL), jnp.stack(a_rows), jnp.stack(b_rows), jnp.stack(b_states),
            jnp.stack(c_states))


def _shift_in(buf, new):
    return jnp.concatenate([buf, new], axis=1)[:, new.shape[1]:]


def _trunk_sample(x, mem_kv, a_pool, b_pool, b_win, c_win, page_table, w):
    ns, t, _ = x.shape
    assert t <= QS and b_win.shape[2] == B_WINDOW and c_win.shape[2] == C_WINDOW
    assert page_table.shape[1] * PAGE_SIZE >= max(4 * A_TOPK, B_TOPN * B_BLOCK, 3 * TK)
    n = ns * QS
    n_pages = page_table.shape[1]
    a_pool = jnp.transpose(a_pool, (0, 1, 3, 4, 2))
    b_pool = jnp.transpose(b_pool, (0, 1, 3, 4, 5, 2))
    expand = _sample_expand(n_pages)
    a_rows, b_rows, b_states, c_states = [], [], [], []
    h = jnp.pad(x, ((0, 0), (0, QS - t), (0, 0))).reshape(n, D_MODEL)
    for l in range(DEPTH):
        g, bb = w['ln_g'][l], w['ln_b'][l]
        h = _ffn_ln(h, w['ffn_wg'][l, 0], w['ffn_wu'][l, 0], w['ffn_wd'][l, 0], g[0], bb[0])
        if l % 2 == 0:
            e = l // 2
            proj = _linear(h, w['even_w_in_perm'][e]).reshape(ns, QS, EVEN_COLS)
            win = b_win[e].reshape(ns, B_WINDOW, 2 * G_COLS)
            o_a = _dsa_sample(proj, a_pool, page_table, e, w['band'][:A_HEADS])
            o_b = _nsa_sample(proj, b_pool, win, page_table, e, w['cmp_t'][e], w['band'][A_HEADS:],
                              w['cband'][A_HEADS:], expand)
            h = _heads_out_ln([o_a, o_b], w['even_w_out_heads'][e], h.reshape(ns, QS, D_MODEL), g[1], bb[1])
            a_rows.append(proj[:, :t, COL_A:COL_A + 3 * HEAD_DIM].reshape(ns, t, 3, HEAD_DIM))
            b_rows.append(proj[:, :t, COL_B:COL_B + 4 * G_COLS].reshape(ns, t, 4, B_KV_HEADS, HEAD_DIM))
            b_states.append(_shift_in(win, proj[:, :t, COL_WIN:COL_WIN + 2 * G_COLS]).reshape(b_win.shape[1:]))
        else:
            oi = l // 2
            proj = _linear(h, w['odd_w_in'][oi]).reshape(ns, QS, C_MIX + 2 * G_COLS)
            buf = c_win[oi].reshape(ns, C_WINDOW, 2 * G_COLS)
            o_c = _swa_sample(proj, buf, w['band'], w['sinks_shifted'][oi])
            h = _heads_out_ln([o_c], w['odd_w_out_heads'][oi], h.reshape(ns, QS, D_MODEL), g[1], bb[1])
            c_states.append(_shift_in(buf, proj[:, :t, COL_CWIN:]).reshape(c_win.shape[1:]))
        h = h.reshape(n, D_MODEL)
        xq = _linear(h, w['x_wq'][l]).reshape(ns, QS, X_WIDTH)
        xo = _cross_attn(xq, mem_kv[l].reshape(ns, -1, 2 * X_WIDTH)).reshape(n, X_WIDTH)
        h = _linear_res_ln(xo, w['x_wo'][l], h, g[2], bb[2])
        h = _ffn_ln(h, w['ffn_wg'][l, 1], w['ffn_wu'][l, 1], w['ffn_wd'][l, 1], g[3], bb[3])
    return (h.reshape(ns, QS, D_MODEL)[:, :t], jnp.stack(a_rows), jnp.stack(b_rows), jnp.stack(b_states),
            jnp.stack(c_states))


def kernel(x_prompt, x_sample, cache_a_kv, cache_b_kv, cache_b_win, cache_c_win, cache_mem_kv, page_table,
           mem_prompt, ln_g, ln_b, ffn_wg, ffn_wu, ffn_wd, even_w_in, even_w_out, nsa_cmp_pos, nsa_cmp_w1,
           nsa_cmp_w2, odd_w_in, odd_w_out, c_sinks, x_wq, x_wk, x_wv, x_wo, rel_table):
    nb, n_mem = mem_prompt.shape[:2]
    dist_bias = _distance_bias(rel_table)
    w = dict(ln_g=ln_g, ln_b=ln_b, ffn_wg=ffn_wg.astype(BF16), ffn_wu=ffn_wu.astype(BF16),
             ffn_wd=ffn_wd.astype(BF16), odd_w_in=odd_w_in.astype(BF16),
             x_wq=x_wq.astype(BF16), x_wo=x_wo.astype(BF16),
             even_w_in_perm=_permute_even_w_in(even_w_in).astype(BF16),
             even_w_out_heads=even_w_out.astype(BF16).reshape(-1, A_HEADS + B_HEADS, HEAD_DIM, D_MODEL),
             odd_w_out_heads=odd_w_out.astype(BF16).reshape(-1, C_HEADS, HEAD_DIM, D_MODEL),
             band=_band(dist_bias), cband=_cmp_band(dist_bias),
             cmp=[_compress_weights(nsa_cmp_pos[e], nsa_cmp_w1[e], nsa_cmp_w2[e]) for e in range(even_w_in.shape[0])],
             cmp_t=[_compress_weights_t(nsa_cmp_pos[e], nsa_cmp_w1[e], nsa_cmp_w2[e])
                    for e in range(even_w_in.shape[0])],
             sinks_shifted=(c_sinks - rel_table[N_BUCKETS - 1][None, :]).reshape(-1, C_HEADS, 1, 1))
    w_kv = jnp.concatenate([x_wk, x_wv], axis=-1).astype(BF16)
    mem2d = mem_prompt.reshape(nb * n_mem, D_MODEL)
    mem_kv_prompt = jnp.stack([_linear(mem2d, w_kv[l]) for l in range(DEPTH)])
    mem_kv_prompt = mem_kv_prompt.reshape(DEPTH, nb, n_mem, 2, X_HEADS, X_HEAD_DIM)
    y_p, a_p, b_p, bw_p, cw_p = _trunk_prompt(x_prompt, mem_kv_prompt, w)
    y_s, a_s, b_s, bw_s, cw_s = _trunk_sample(x_sample, cache_mem_kv, cache_a_kv, cache_b_kv, cache_b_win,
                                              cache_c_win, page_table, w)
    return (y_p, y_s, a_p, a_s, b_p, b_s, bw_p, bw_s, cw_p, cw_s, mem_kv_prompt)
```

```python
import functools
import math

import jax
import jax.numpy as jnp
import numpy as np
from jax import lax
from jax.experimental import pallas as pl
from jax.experimental.pallas import tpu as pltpu

D_MODEL = 1024
DEPTH = 4
PAGE_SIZE = 128
HEAD_DIM = 64
A_HEADS = 8
A_IDX_HEADS = 4
A_IDX_DIM = 64
A_TOPK = 256
B_HEADS = 8
B_KV_HEADS = 2
B_BLOCK = 64
B_TOPN = 16
B_WINDOW = 512
C_HEADS = 16
C_KV_HEADS = 2
C_WINDOW = 128
X_HEADS = 4
X_HEAD_DIM = 128
D_FF = 2816
N_BUCKETS = 32
BUCKET_MAX_DIST = 128
QBLOCK = 128
LN_EPS = 1e-5
ALPHA = (2 * DEPTH) ** 0.25
NEG = -1e30
FORCE_SCORE = 1e4
EVEN_SPLITS = [A_HEADS * HEAD_DIM, HEAD_DIM, HEAD_DIM, A_IDX_HEADS * A_IDX_DIM, A_IDX_DIM, A_IDX_HEADS,
               B_HEADS * HEAD_DIM] + [B_KV_HEADS * HEAD_DIM] * 6 + [B_HEADS * 3]
ODD_SPLITS = [C_HEADS * HEAD_DIM, C_KV_HEADS * HEAD_DIM, C_KV_HEADS * HEAD_DIM]
A_MIX = A_HEADS * HEAD_DIM
B_MIX = B_HEADS * HEAD_DIM
C_MIX = C_HEADS * HEAD_DIM
X_WIDTH = X_HEADS * X_HEAD_DIM

LANES = 128
VMEM_LIMIT = 56 << 20
BF16 = jnp.bfloat16
F32 = jnp.float32


def _params(*sem):
    return pltpu.CompilerParams(dimension_semantics=sem, vmem_limit_bytes=VMEM_LIMIT)


def _resident(shape):
    return pl.BlockSpec(shape, lambda *_: (0,) * len(shape), pipeline_mode=pl.Buffered(1))


def _per_batch(shape, col_block):
    return pl.BlockSpec(shape, lambda b, i: (b, 0, col_block), pipeline_mode=pl.Buffered(1))


def _row_tile(n):
    return min(n, 512)


def _layer_norm_rows(y, g, b):
    mu = jnp.mean(y, axis=-1, keepdims=True)
    d = y - mu
    var = jnp.mean(d * d, axis=-1, keepdims=True)
    return d * lax.rsqrt(var + LN_EPS) * g + b


FF_CHUNKS = 2
FF_CHUNK = D_FF // FF_CHUNKS


def _ffn_ln_kernel(x_ref, wg_ref, wu_ref, wd_ref, g_ref, b_ref, o_ref):
    x = x_ref[...]
    xb = x.astype(BF16)
    acc = jnp.zeros(x.shape, F32)
    for c in range(FF_CHUNKS):
        cols = slice(c * FF_CHUNK, (c + 1) * FF_CHUNK)
        gate = jnp.dot(xb, wg_ref[:, cols], preferred_element_type=F32)
        up = jnp.dot(xb, wu_ref[:, cols], preferred_element_type=F32)
        act = (gate * jax.nn.sigmoid(gate) * up).astype(BF16)
        acc = acc + jnp.dot(act, wd_ref[cols, :], preferred_element_type=F32)
    o_ref[...] = _layer_norm_rows(ALPHA * x + 0.5 * acc, g_ref[...], b_ref[...])


def _ffn_ln(x, wg, wu, wd, g, b):
    n = x.shape[0]
    tm = _row_tile(n)
    return pl.pallas_call(
        _ffn_ln_kernel,
        grid=(n // tm,),
        in_specs=[pl.BlockSpec((tm, D_MODEL), lambda i: (i, 0)),
                  _resident((D_MODEL, D_FF)), _resident((D_MODEL, D_FF)), _resident((D_FF, D_MODEL)),
                  _resident((1, D_MODEL)), _resident((1, D_MODEL))],
        out_specs=pl.BlockSpec((tm, D_MODEL), lambda i: (i, 0)),
        out_shape=jax.ShapeDtypeStruct((n, D_MODEL), F32),
        compiler_params=_params("parallel"),
        name="ffn_ln",
    )(x, wg, wu, wd, g.reshape(1, D_MODEL), b.reshape(1, D_MODEL))


def _linear_kernel(x_ref, w_ref, o_ref):
    o_ref[...] = jnp.dot(x_ref[...].astype(BF16), w_ref[...], preferred_element_type=F32)


def _linear(x, w):
    n, k = x.shape
    m = w.shape[1]
    tm = _row_tile(n)
    return pl.pallas_call(
        _linear_kernel,
        grid=(n // tm,),
        in_specs=[pl.BlockSpec((tm, k), lambda i: (i, 0)), _resident((k, m))],
        out_specs=pl.BlockSpec((tm, m), lambda i: (i, 0)),
        out_shape=jax.ShapeDtypeStruct((n, m), F32),
        compiler_params=_params("parallel"),
        name="linear",
    )(x, w)


def _linear_res_ln_kernel(x_ref, w_ref, h_ref, g_ref, b_ref, o_ref):
    y = jnp.dot(x_ref[...].astype(BF16), w_ref[...], preferred_element_type=F32)
    o_ref[...] = _layer_norm_rows(ALPHA * h_ref[...] + y, g_ref[...], b_ref[...])


def _linear_res_ln(x, w, h, g, b):
    n, k = x.shape
    tm = _row_tile(n)
    return pl.pallas_call(
        _linear_res_ln_kernel,
        grid=(n // tm,),
        in_specs=[pl.BlockSpec((tm, k), lambda i: (i, 0)), _resident((k, D_MODEL)),
                  pl.BlockSpec((tm, D_MODEL), lambda i: (i, 0)),
                  _resident((1, D_MODEL)), _resident((1, D_MODEL))],
        out_specs=pl.BlockSpec((tm, D_MODEL), lambda i: (i, 0)),
        out_shape=jax.ShapeDtypeStruct((n, D_MODEL), F32),
        compiler_params=_params("parallel"),
        name="linear_res_ln",
    )(x, w, h, g.reshape(1, D_MODEL), b.reshape(1, D_MODEL))


def t5_bucket(dist):
    exact = N_BUCKETS // 2
    d = jnp.maximum(dist, 0)
    rel = jnp.log(jnp.maximum(d, 1).astype(jnp.float32) / exact) / math.log(BUCKET_MAX_DIST / exact)
    large = jnp.minimum(exact + (rel * (N_BUCKETS - exact)).astype(jnp.int32), N_BUCKETS - 1)
    return jnp.where(d < exact, d, large)


TK = 128
INT_MIN = -2 ** 31
NT = (((1,), (1,)), ((), ()))


def _ordered_key(x):
    bits = pltpu.bitcast(x, jnp.int32)
    key = jnp.where(bits < 0, bits ^ jnp.int32(0x7FFFFFFF), bits)
    return jnp.where(x == 0.0, 0, key)


def _count(key_scr, nkt, rows, pred):
    hit = lambda j: jnp.where(pred(key_scr[j], j), 1.0, 0.0)

    def pair(j2, c):
        return c + hit(2 * j2) + hit(2 * j2 + 1)
    c = lax.fori_loop(0, nkt // 2, pair, jnp.zeros((rows, TK), F32))
    c = c + jnp.where(nkt % 2 == 1, 1.0, 0.0) * hit(nkt - 1)
    return jnp.sum(c, axis=1, keepdims=True)


def _topk_threshold(key_scr, nkt, rows, k, idx_bits):
    lane = lax.broadcasted_iota(jnp.int32, (rows, TK), 1)

    def value_bit(state):
        it, ans, n_ans = state
        cand = ans + lax.shift_left(jnp.int32(1), 31 - it)
        cand_key = jnp.broadcast_to(cand, (rows, TK))
        cnt = _count(key_scr, nkt, rows, lambda t, j: t >= cand_key)
        keep = cnt >= k
        return it + 1, jnp.where(keep, cand, ans), jnp.where(keep, cnt, n_ans)

    def unsettled(state):
        it, _, n_ans = state
        return (it < 32) & (jnp.max(jnp.abs(n_ans - k)) > 0.0)
    _, tau, n_ge = lax.while_loop(unsettled, value_bit,
                                  (jnp.int32(0), jnp.full((rows, 1), INT_MIN, jnp.int32),
                                   jnp.full((rows, 1), 2.0 ** 30, F32)))

    def cut_among_ties():
        need = k - _count(key_scr, nkt, rows, lambda t, j: t > tau)

        def index_bit(it, cut):
            cand = cut | lax.shift_left(jnp.int32(1), idx_bits - 1 - it)
            cnt = _count(key_scr, nkt, rows,
                         lambda t, j: jnp.where(t == tau, j * TK + lane, jnp.int32(2 ** 30)) < cand)
            return jnp.where(cnt < need, cand, cut)
        return lax.fori_loop(0, idx_bits, index_bit, jnp.zeros((rows, 1), jnp.int32))

    cut = lax.cond(jnp.max(n_ge) > k, cut_among_ties, lambda: jnp.full((rows, 1), 2 ** 30, jnp.int32))
    return tau, cut


def _in_topk(t, j, tau, cut):
    lane = lax.broadcasted_iota(jnp.int32, t.shape, 1)
    return (t > tau) | ((t == tau) & (j * TK + lane <= cut))


def _heads_to_rows(x, n_heads):
    return jnp.concatenate([x[:, h * HEAD_DIM:(h + 1) * HEAD_DIM] for h in range(n_heads)], axis=0)


FAR_TILES = 4


def _flash_init(mx_scr, acc_scr):
    mx_scr[...] = jnp.full(mx_scr.shape, NEG, F32)
    acc_scr[...] = jnp.zeros(acc_scr.shape, F32)


def _with_ones(v):
    return jnp.concatenate([v, jnp.ones(v.shape, BF16)], axis=1)


def _masked_logits(q_rows, k, madd, bias):
    n_h = q_rows.shape[0] // TK
    lg = lax.dot_general(q_rows, k, NT, preferred_element_type=F32).reshape(n_h, TK, k.shape[0]) + madd[None]
    return lg if bias is None else lg + bias


def _max_step(mx_scr, heads, q_rows, k, madd, bias):
    lg = _masked_logits(q_rows, k, madd, bias)
    m = mx_scr[heads]
    for c in range(k.shape[0] // TK):
        m = jnp.maximum(m, lg[:, :, c * TK:(c + 1) * TK])
    mx_scr[heads] = m


def _max_finish(mx_scr):
    mx_scr[...] = jnp.broadcast_to(jnp.max(mx_scr[...], axis=-1, keepdims=True), mx_scr.shape)


def _acc_step(mx_scr, acc_scr, heads, q_rows, k, v1, madd, bias):
    lg = _masked_logits(q_rows, k, madd, bias)
    n_h, _, n = lg.shape
    p = jnp.exp(lg - jnp.concatenate([mx_scr[heads]] * (n // TK), axis=2)).astype(BF16)
    acc_scr[heads] += jnp.dot(p.reshape(n_h * TK, n), v1, preferred_element_type=F32).reshape(n_h, TK, 2 * HEAD_DIM)


def _flash_far_loop(n_far, attend):
    n_big = n_far // FAR_TILES

    def big(jb, carry):
        attend(jb * FAR_TILES, FAR_TILES)
        return carry
    lax.fori_loop(0, n_big, big, 0)

    def small(j, carry):
        attend(j, 1)
        return carry
    lax.fori_loop(n_big * FAR_TILES, n_far, small, 0)


def _two_pass_attention(i, mx_scr, attend):
    for accumulate in (False, True):
        _flash_far_loop(jnp.maximum(i - 1, 0), functools.partial(attend, accumulate))

        @pl.when(i >= 1)
        def _():
            attend(accumulate, i - 1, 1, slice(0, TK))
        attend(accumulate, i, 1, slice(TK, 2 * TK), True)
        if not accumulate:
            _max_finish(mx_scr)


def _flash_result(acc_scr, h):
    acc = acc_scr[h]
    return acc[:, 0:HEAD_DIM] / acc[:, HEAD_DIM:2 * HEAD_DIM]


def _window_attend(qg, n_r, qr, tiles, band_ref, head0, sink=None):
    row = lax.broadcasted_iota(jnp.int32, (qr, TK), 0)
    lane = lax.broadcasted_iota(jnp.int32, (qr, TK), 1)
    heads = slice(head0, head0 + n_r)
    pieces = []
    for d, (k, _, exists) in enumerate(tiles):
        lg = lax.dot_general(qg, k, NT, preferred_element_type=F32).reshape(n_r, qr, TK)
        ok = jnp.broadcast_to(exists, (qr, TK))
        if d == 0:
            ok = ok & (lane <= row)
            lg = lg + band_ref[heads, 0:qr, TK:2 * TK]
        if d == 1:
            lg = lg + band_ref[heads, 0:qr, 0:TK]
        if d == len(tiles) - 1:
            ok = ok & (lane > row)
        pieces.append(jnp.where(ok[None], lg, NEG).reshape(n_r * qr, TK))
    m = jnp.max(functools.reduce(jnp.maximum, pieces), axis=1, keepdims=True)
    if sink is not None:
        m = jnp.maximum(m, sink)
    ps = [jnp.exp(x - m) for x in pieces]
    den = jnp.sum(functools.reduce(lambda a, b: a + b, ps), axis=1, keepdims=True)
    if sink is not None:
        den = den + jnp.exp(sink - m)
    o = functools.reduce(lambda a, b: a + b,
                         [jnp.dot(p.astype(BF16), t[1], preferred_element_type=F32) for p, t in zip(ps, tiles)])
    return o / den


def _distance_bias(rel_table):
    tb = rel_table[t5_bucket(jnp.arange(2 * TK))]
    return tb - rel_table[N_BUCKETS - 1][None, :]


def _band(dist_bias):
    a = np.arange(TK)[:, None]
    c = np.arange(2 * TK)[None, :]
    idx = np.clip(a - c + TK, 0, 2 * TK - 1)
    return jnp.transpose(dist_bias[idx], (2, 0, 1))


def _dsa_prompt_kernel(q_ref, qi_ref, misc_ref, kv_ref, band_ref, o_ref, key_scr, m_scr, acc_scr):
    i = pl.program_id(1)
    nkt = i + 1
    row = lax.broadcasted_iota(jnp.int32, (TK, TK), 0)
    lane = lax.broadcasted_iota(jnp.int32, (TK, TK), 1)
    qi = qi_ref[0].astype(BF16)
    wi = [jnp.broadcast_to(misc_ref[0][:, MISC_WI + h:MISC_WI + h + 1], (TK, TK)) for h in range(A_IDX_HEADS)]

    def score_tiles(j0, n, causal=False):
        rows = pl.ds(pl.multiple_of(j0 * TK, TK), n * TK)
        kt = kv_ref[0, rows, 2 * HEAD_DIM:3 * HEAD_DIM].astype(BF16)
        sc = jnp.zeros((TK, n * TK), F32)
        for h in range(A_IDX_HEADS):
            s = lax.dot_general(qi[:, h * A_IDX_DIM:(h + 1) * A_IDX_DIM], kt, NT, preferred_element_type=F32)
            sc = sc + jnp.concatenate([wi[h]] * n, axis=1) * jnp.maximum(s * A_IDX_DIM ** -0.5, 0.0)
        sc = sc * A_IDX_HEADS ** -0.5
        if causal:
            sc = jnp.where(lane <= row, sc, NEG)
        for t in range(n):
            key_scr[j0 + t] = _ordered_key(sc[:, t * TK:(t + 1) * TK])
    _flash_far_loop(i, score_tiles)
    score_tiles(i, 1, True)
    tau, cut = _topk_threshold(key_scr, nkt, TK, float(A_TOPK), DSA_IDX_BITS)

    q = (q_ref[0] * HEAD_DIM ** -0.5).astype(BF16)
    qh = _heads_to_rows(q, A_HEADS)
    _flash_init(m_scr, acc_scr)

    def attend(accumulate, j0, n, bias_cols=None, causal=False):
        rows = pl.ds(pl.multiple_of(j0 * TK, TK), n * TK)
        k = kv_ref[0, rows, 0:HEAD_DIM].astype(BF16)
        sels = [_in_topk(key_scr[j0 + t], j0 + t, tau, cut) for t in range(n)]
        if causal:
            sels = [s & (lane <= row) for s in sels]
        madd = jnp.concatenate([jnp.where(s, 0.0, NEG) for s in sels], axis=1)
        bias = None if bias_cols is None else band_ref[:, :, bias_cols]
        if accumulate:
            v1 = _with_ones(kv_ref[0, rows, HEAD_DIM:2 * HEAD_DIM].astype(BF16))
            _acc_step(m_scr, acc_scr, slice(0, A_HEADS), qh, k, v1, madd, bias)
        else:
            _max_step(m_scr, slice(0, A_HEADS), qh, k, madd, bias)

    _two_pass_attention(i, m_scr, attend)
    for h in range(A_HEADS):
        o_ref[0, h] = _flash_result(acc_scr, h)


def _dsa_prompt(proj, band_a):
    bn, t, _ = proj.shape
    nqb = t // TK
    return pl.pallas_call(
        _dsa_prompt_kernel,
        grid=(bn, nqb),
        in_specs=[pl.BlockSpec((1, TK, A_MIX), lambda b, i: (b, i, COL_QA // A_MIX)),
                  pl.BlockSpec((1, TK, 256), lambda b, i: (b, i, COL_QI // 256)),
                  pl.BlockSpec((1, TK, 256), lambda b, i: (b, i, COL_A // 256)),
                  _per_batch((1, t, 256), COL_A // 256),
                  _resident((A_HEADS, TK, 2 * TK))],
        out_specs=pl.BlockSpec((1, A_HEADS, TK, HEAD_DIM), lambda b, i: (b, 0, i, 0)),
        out_shape=jax.ShapeDtypeStruct((bn, A_HEADS, t, HEAD_DIM), F32),
        scratch_shapes=[pltpu.VMEM((nqb, TK, TK), jnp.int32),
                        pltpu.VMEM((A_HEADS, TK, TK), F32), pltpu.VMEM((A_HEADS, TK, 2 * HEAD_DIM), F32)],
        compiler_params=_params("parallel", "arbitrary"),
        name="dsa_prompt",
    )(proj, proj, proj, proj, band_a)


COL_QA, COL_QB, COL_QI, COL_A = 0, 512, 1024, 1280
COL_B, COL_WIN, EVEN_COLS = 1536, 2048, 2304
MISC_WI = 3 * HEAD_DIM
MISC_GL = MISC_WI + A_IDX_HEADS
DSA_IDX_BITS = 14


def _even_in_perm():
    offs = np.concatenate([[0], np.cumsum(EVEN_SPLITS)])
    seg = lambda k: np.arange(offs[k], offs[k + 1])
    (q_a, k_a, v_a, qi, ki, wi, q_b, kc, vc, ksl, vsl, kw, vw, gl) = [seg(k) for k in range(14)]
    order = [q_a, q_b, qi, k_a, v_a, ki, wi, gl]
    pad = EVEN_COLS - sum(EVEN_SPLITS)
    return np.concatenate(order), pad, np.concatenate([kc, vc, ksl, vsl, kw, vw])


def _permute_even_w_in(w):
    head, pad, tail = _even_in_perm()
    zeros = jnp.zeros(w.shape[:-1] + (pad,), w.dtype)
    return jnp.concatenate([w[..., head], zeros, w[..., tail]], axis=-1)


G_COLS = B_KV_HEADS * HEAD_DIM


def _compress_rows(x, pos_w, w1, w2):
    nb = x.shape[0] // B_BLOCK
    pooled = jnp.sum(x.reshape(nb, B_BLOCK, G_COLS) * pos_w[None], axis=1)
    hid = jax.nn.gelu(jnp.dot(pooled.astype(BF16), w1, preferred_element_type=F32))
    return jnp.dot(hid.astype(BF16), w2, preferred_element_type=F32)


def _compress_kernel(kv_ref, pos_ref, w1_ref, w2_ref, o_ref):
    for s in range(2):
        x = kv_ref[0, :, s * G_COLS:(s + 1) * G_COLS]
        o_ref[0, :, s * G_COLS:(s + 1) * G_COLS] = _compress_rows(x, pos_ref[s], w1_ref[s], w2_ref[s])


def _compress_weights(cmp_pos, cmp_w1, cmp_w2):
    pos = jnp.concatenate([cmp_pos] * B_KV_HEADS, axis=-1)
    eye = jnp.eye(B_KV_HEADS, dtype=cmp_w1.dtype)
    bd = lambda w: jnp.einsum('gh,sde->sgdhe', eye, w).reshape(2, G_COLS, G_COLS).astype(BF16)
    return pos, bd(cmp_w1), bd(cmp_w2)


def _compress_prompt(proj, pos, w1, w2):
    bn, t, _ = proj.shape
    tr = min(t, 2048)
    return pl.pallas_call(
        _compress_kernel,
        grid=(bn, t // tr),
        in_specs=[pl.BlockSpec((1, tr, 2 * G_COLS), lambda b, i: (b, i, COL_B // (2 * G_COLS))),
                  _resident((2, B_BLOCK, G_COLS)), _resident((2, G_COLS, G_COLS)), _resident((2, G_COLS, G_COLS))],
        out_specs=pl.BlockSpec((1, tr // B_BLOCK, 2 * G_COLS), lambda b, i: (b, i, 0)),
        out_shape=jax.ShapeDtypeStruct((bn, t // B_BLOCK, 2 * G_COLS), F32),
        compiler_params=_params("parallel", "parallel"),
        name="nsa_compress",
    )(proj, pos, w1, w2)


B_R = B_HEADS // B_KV_HEADS
NSA_WIN_TILES = B_WINDOW // TK


def _softmax_pieces(pieces):
    m = functools.reduce(jnp.maximum, [jnp.max(x, axis=1, keepdims=True) for x in pieces])
    ps = [jnp.exp(x - m) for x in pieces]
    den = functools.reduce(lambda a, b: a + b, [jnp.sum(p, axis=1, keepdims=True) for p in ps])
    return ps, den, m


def _select_blocks(imp, n_sel, lanef=None):
    if lanef is None:
        lanef = lax.broadcasted_iota(jnp.int32, imp.shape, 1).astype(F32)
    sel = jnp.zeros(imp.shape, F32)
    for _ in range(n_sel):
        mx = jnp.max(imp, axis=1, keepdims=True)
        first = jnp.min(jnp.where(imp == mx, lanef, float(2 ** 24)), axis=1, keepdims=True)
        hit = lanef == first
        sel = jnp.where(hit, 1.0, sel)
        imp = jnp.where(hit, -jnp.inf, imp)
    return sel


def _nsa_prompt_kernel(q_ref, misc_ref, kv_ref, win_ref, cmp_ref, band_ref, cband_ref, o_ref,
                       selm_scr, m_scr, acc_scr):
    i = pl.program_id(1)
    n_blk = cmp_ref.shape[1]
    row = lax.broadcasted_iota(jnp.int32, (TK, TK), 0)
    lane = lax.broadcasted_iota(jnp.int32, (TK, TK), 1)
    tpos = i * TK + row
    q = (q_ref[0] * HEAD_DIM ** -0.5).astype(BF16)
    gates = jax.nn.sigmoid(misc_ref[0][:, MISC_GL:MISC_GL + 3 * B_HEADS])
    qg = [_heads_to_rows(q[:, g * B_R * HEAD_DIM:(g + 1) * B_R * HEAD_DIM], B_R) for g in range(B_KV_HEADS)]

    blk = lax.broadcasted_iota(jnp.int32, (TK, n_blk), 1)
    brow = lax.broadcasted_iota(jnp.int32, (TK, n_blk), 0)
    bpos = i * TK + brow
    valid_c = (blk + 1) * B_BLOCK - 1 <= bpos
    cur = bpos // B_BLOCK
    forced = jnp.where((blk == 0) | (blk == cur) | (blk == cur - 1), FORCE_SCORE, 0.0)
    o_cmp, imps = [], []
    for g in range(B_KV_HEADS):
        kc = cmp_ref[0, :, g * HEAD_DIM:(g + 1) * HEAD_DIM].astype(BF16)
        vc = cmp_ref[0, :, G_COLS + g * HEAD_DIM:G_COLS + (g + 1) * HEAD_DIM].astype(BF16)
        lc = lax.dot_general(qg[g], kc, NT, preferred_element_type=F32).reshape(B_R, TK, n_blk)
        bias = []
        for r in range(B_R):
            cb = cband_ref[g * B_R + r]
            b = jnp.zeros((TK, n_blk), F32)
            for u in range(4):
                b = jnp.where(blk == 2 * i + u - 2, cb[:, u:u + 1], b)
            bias.append(b)
        lc = jnp.where(valid_c[None], lc + jnp.stack(bias), NEG).reshape(B_R * TK, n_blk)
        (p,), den, _ = _softmax_pieces([lc])
        pc = ((p / den).reshape(B_R, TK, n_blk) * jnp.where(valid_c, 1.0, 0.0)[None])
        o_cmp.append(jnp.dot(pc.reshape(B_R * TK, n_blk).astype(BF16), vc, preferred_element_type=F32))
        imps.append(jnp.where(blk <= cur, jnp.sum(pc, axis=0) + forced, NEG))
    selm = _select_blocks(jnp.concatenate(imps, axis=0), min(B_TOPN, n_blk))
    for g in range(B_KV_HEADS):
        selm_scr[g] = selm[g * TK:(g + 1) * TK].astype(BF16)

    _flash_init(m_scr, acc_scr)

    def attend(accumulate, j0, n, bias_cols=None, causal=False):
        rows = pl.ds(pl.multiple_of(j0 * TK, TK), n * TK)
        ebk = lax.broadcasted_iota(jnp.int32, (n_blk, n * TK), 0)
        etok = j0 * TK + lax.broadcasted_iota(jnp.int32, (n_blk, n * TK), 1)
        expand = jnp.where(ebk == etok // B_BLOCK, 1.0, 0.0).astype(BF16)
        for g in range(B_KV_HEADS):
            k = kv_ref[0, rows, g * HEAD_DIM:(g + 1) * HEAD_DIM].astype(BF16)
            sel = jnp.dot(selm_scr[g], expand, preferred_element_type=F32) > 0.5
            if causal:
                sel = sel & (lane <= row)
            madd = jnp.where(sel, 0.0, NEG)
            heads = slice(g * B_R, (g + 1) * B_R)
            bias = None if bias_cols is None else band_ref[heads, :, bias_cols]
            if accumulate:
                v1 = _with_ones(kv_ref[0, rows, G_COLS + g * HEAD_DIM:G_COLS + (g + 1) * HEAD_DIM].astype(BF16))
                _acc_step(m_scr, acc_scr, heads, qg[g], k, v1, madd, bias)
            else:
                _max_step(m_scr, heads, qg[g], k, madd, bias)

    _two_pass_attention(i, m_scr, attend)

    for g in range(B_KV_HEADS):
        tiles = []
        for d in range(NSA_WIN_TILES + 1):
            rows = pl.ds(pl.multiple_of(jnp.maximum(i - d, 0) * TK, TK), TK)
            tiles.append((win_ref[0, rows, g * HEAD_DIM:(g + 1) * HEAD_DIM].astype(BF16),
                          win_ref[0, rows, G_COLS + g * HEAD_DIM:G_COLS + (g + 1) * HEAD_DIM].astype(BF16),
                          i - d >= 0))
        o_win = _window_attend(qg[g], B_R, TK, tiles, band_ref, g * B_R)
        for r in range(B_R):
            h = g * B_R + r
            rr = slice(r * TK, (r + 1) * TK)
            o_ref[0, h] = (gates[:, 3 * h:3 * h + 1] * o_cmp[g][rr]
                           + gates[:, 3 * h + 1:3 * h + 2] * _flash_result(acc_scr, h)
                           + gates[:, 3 * h + 2:3 * h + 3] * o_win[rr])


def _cmp_band(dist_bias):
    a = np.arange(TK)[:, None]
    u = np.arange(4)[None, :] - 2
    idx = np.clip(a - B_BLOCK * u - (B_BLOCK - 1), 0, 2 * TK - 1)
    return jnp.transpose(dist_bias[idx], (2, 0, 1))


def _nsa_prompt(proj, cmp_kv, band_b, cband_b):
    bn, t, _ = proj.shape
    nqb = t // TK
    n_blk = cmp_kv.shape[1]
    return pl.pallas_call(
        _nsa_prompt_kernel,
        grid=(bn, nqb),
        in_specs=[pl.BlockSpec((1, TK, B_MIX), lambda b, i: (b, i, COL_QB // B_MIX)),
                  pl.BlockSpec((1, TK, 256), lambda b, i: (b, i, COL_A // 256)),
                  _per_batch((1, t, 2 * G_COLS), (COL_B + 2 * G_COLS) // (2 * G_COLS)),
                  _per_batch((1, t, 2 * G_COLS), COL_WIN // (2 * G_COLS)),
                  _per_batch((1, n_blk, 2 * G_COLS), 0),
                  _resident((B_HEADS, TK, 2 * TK)), _resident((B_HEADS, TK, 4))],
        out_specs=pl.BlockSpec((1, B_HEADS, TK, HEAD_DIM), lambda b, i: (b, 0, i, 0)),
        out_shape=jax.ShapeDtypeStruct((bn, B_HEADS, t, HEAD_DIM), F32),
        scratch_shapes=[pltpu.VMEM((B_KV_HEADS, TK, n_blk), BF16),
                        pltpu.VMEM((B_HEADS, TK, TK), F32), pltpu.VMEM((B_HEADS, TK, 2 * HEAD_DIM), F32)],
        compiler_params=_params("parallel", "arbitrary"),
        name="nsa_prompt",
    )(proj, proj, proj, proj, cmp_kv, band_b, cband_b)


C_R = C_HEADS // C_KV_HEADS
COL_QC, COL_CWIN = 0, C_MIX


def _swa_prompt_kernel(q_ref, kv_ref, band_ref, sink_ref, o_ref):
    i = pl.program_id(1)
    row = lax.broadcasted_iota(jnp.int32, (TK, TK), 0)
    lane = lax.broadcasted_iota(jnp.int32, (TK, TK), 1)
    q = (q_ref[0] * HEAD_DIM ** -0.5).astype(BF16)
    prev_rows = pl.ds(pl.multiple_of(jnp.maximum(i - 1, 0) * TK, TK), TK)
    own_rows = pl.ds(pl.multiple_of(i * TK, TK), TK)
    prev_ok = jnp.broadcast_to(i >= 1, (TK, TK)) & (lane > row)
    for g in range(C_KV_HEADS):
        qg = _heads_to_rows(q[:, g * C_R * HEAD_DIM:(g + 1) * C_R * HEAD_DIM], C_R)
        pieces, vals = [], []
        for rows, ok, cols in ((prev_rows, prev_ok, slice(0, TK)), (own_rows, lane <= row, slice(TK, 2 * TK))):
            k = kv_ref[0, rows, g * HEAD_DIM:(g + 1) * HEAD_DIM].astype(BF16)
            vals.append(kv_ref[0, rows, G_COLS + g * HEAD_DIM:G_COLS + (g + 1) * HEAD_DIM].astype(BF16))
            lg = lax.dot_general(qg, k, NT, preferred_element_type=F32).reshape(C_R, TK, TK)
            lg = lg + band_ref[g * C_R:(g + 1) * C_R, :, cols]
            pieces.append(jnp.where(ok[None], lg, NEG).reshape(C_R * TK, TK))
        sink = jnp.concatenate([jnp.broadcast_to(sink_ref[g * C_R + r], (TK, 1)) for r in range(C_R)], axis=0)
        m = jnp.maximum(jnp.maximum(jnp.max(pieces[0], axis=1, keepdims=True),
                                    jnp.max(pieces[1], axis=1, keepdims=True)), sink)
        ps = [jnp.exp(x - m) for x in pieces]
        den = jnp.sum(ps[0], axis=1, keepdims=True) + jnp.sum(ps[1], axis=1, keepdims=True) + jnp.exp(sink - m)
        o = (jnp.dot(ps[0].astype(BF16), vals[0], preferred_element_type=F32)
             + jnp.dot(ps[1].astype(BF16), vals[1], preferred_element_type=F32)) / den
        o_ref[0, g * C_R:(g + 1) * C_R] = o.reshape(C_R, TK, HEAD_DIM)


def _swa_prompt(proj, band_c, sinks):
    bn, t, _ = proj.shape
    return pl.pallas_call(
        _swa_prompt_kernel,
        grid=(bn, t // TK),
        in_specs=[pl.BlockSpec((1, TK, C_MIX), lambda b, i: (b, i, COL_QC // C_MIX)),
                  _per_batch((1, t, 2 * G_COLS), COL_CWIN // (2 * G_COLS)),
                  _resident((C_HEADS, TK, 2 * TK)), _resident((C_HEADS, 1, 1))],
        out_specs=pl.BlockSpec((1, C_HEADS, TK, HEAD_DIM), lambda b, i: (b, 0, i, 0)),
        out_shape=jax.ShapeDtypeStruct((bn, C_HEADS, t, HEAD_DIM), F32),
        compiler_params=_params("parallel", "arbitrary"),
        name="swa_prompt",
    )(proj, proj, band_c, sinks)


def _cross_attn_kernel(q_ref, mem_ref, o_ref):
    q = (q_ref[0] * X_HEAD_DIM ** -0.5).astype(BF16)
    for h in range(X_HEADS):
        cols = slice(h * X_HEAD_DIM, (h + 1) * X_HEAD_DIM)
        k = mem_ref[0, :, cols].astype(BF16)
        v = mem_ref[0, :, X_WIDTH + h * X_HEAD_DIM:X_WIDTH + (h + 1) * X_HEAD_DIM].astype(BF16)
        s = lax.dot_general(q[:, cols], k, NT, preferred_element_type=F32)
        p = jnp.exp(s - jnp.max(s, axis=1, keepdims=True))
        den = jnp.sum(p, axis=1, keepdims=True)
        o_ref[0, :, cols] = jnp.dot(p.astype(BF16), v, preferred_element_type=F32) / den


def _cross_attn(q, mem):
    bn, t, _ = q.shape
    n_mem = mem.shape[1]
    tq = min(t, 512)
    return pl.pallas_call(
        _cross_attn_kernel,
        grid=(bn, t // tq),
        in_specs=[pl.BlockSpec((1, tq, X_WIDTH), lambda b, i: (b, i, 0)),
                  pl.BlockSpec((1, n_mem, 2 * X_WIDTH), lambda b, i: (b, 0, 0))],
        out_specs=pl.BlockSpec((1, tq, X_WIDTH), lambda b, i: (b, i, 0)),
        out_shape=jax.ShapeDtypeStruct((bn, t, X_WIDTH), F32),
        compiler_params=_params("parallel", "arbitrary"),
        name="cross_attn",
    )(q, mem)


def _heads_out_ln_kernel(*refs):
    o_refs, (w_ref, h_ref, g_ref, b_ref, out_ref) = refs[:-5], refs[-5:]
    y = jnp.zeros(h_ref.shape[1:], F32)
    hh = 0
    for o_ref in o_refs:
        for h in range(o_ref.shape[1]):
            y = y + jnp.dot(o_ref[0, h].astype(BF16), w_ref[hh], preferred_element_type=F32)
            hh += 1
    out_ref[0] = _layer_norm_rows(ALPHA * h_ref[0] + y, g_ref[...], b_ref[...])


def _heads_out_ln(outs, w, h, g, b):
    bn, t, _ = h.shape
    tm = min(t, 512)
    return pl.pallas_call(
        _heads_out_ln_kernel,
        grid=(bn, t // tm),
        in_specs=[pl.BlockSpec((1, o.shape[1], tm, HEAD_DIM), lambda b, i: (b, 0, i, 0)) for o in outs]
        + [_resident(w.shape), pl.BlockSpec((1, tm, D_MODEL), lambda b, i: (b, i, 0)),
           _resident((1, D_MODEL)), _resident((1, D_MODEL))],
        out_specs=pl.BlockSpec((1, tm, D_MODEL), lambda b, i: (b, i, 0)),
        out_shape=jax.ShapeDtypeStruct((bn, t, D_MODEL), F32),
        compiler_params=_params("parallel", "parallel"),
        name="heads_out_ln",
    )(*outs, w, h, g.reshape(1, D_MODEL), b.reshape(1, D_MODEL))


QS = 8


def _own_tile(new_rows):
    return jnp.concatenate([new_rows, jnp.zeros((TK - new_rows.shape[0], new_rows.shape[1]), F32)], axis=0)


def _topk_threshold_dense(keys, pos, k, idx_bits):
    rows = keys.shape[1]
    count = lambda pred: jnp.sum(jnp.sum(jnp.where(pred, 1.0, 0.0), axis=0), axis=1, keepdims=True)

    def value_bits(it, ans):
        c1, c2, c3 = [ans + lax.shift_left(jnp.int32(m), 30 - 2 * it) for m in (1, 2, 3)]
        n1, n2, n3 = count(keys >= c1), count(keys >= c2), count(keys >= c3)
        return jnp.where(n3 >= k, c3, jnp.where(n2 >= k, c2, jnp.where(n1 >= k, c1, ans)))
    tau = lax.fori_loop(0, 16, value_bits, jnp.full((rows, 1), INT_MIN, jnp.int32))

    def cut_among_ties():
        need = k - count(keys > tau)
        tie_pos = jnp.where(keys == tau, pos, jnp.int32(2 ** 30))

        def index_bit(it, cut):
            cand = cut | lax.shift_left(jnp.int32(1), idx_bits - 1 - it)
            return jnp.where(count(tie_pos < cand) < need, cand, cut)
        return lax.fori_loop(0, idx_bits, index_bit, jnp.zeros((rows, 1), jnp.int32))

    cut = lax.cond(jnp.max(count(keys >= tau)) > k, cut_among_ties,
                   lambda: jnp.full((rows, 1), 2 ** 30, jnp.int32))
    return tau, cut


def _tile_attention(lg_ref, madd, band, n_h, vt_ref, own_v):
    nt = lg_ref.shape[0]
    lg = lg_ref[...].reshape(nt, n_h, QS, TK) + madd[:, None]
    far = lg[:nt - 2]
    near = lg[nt - 2:] + jnp.stack([band[:, :, 0:TK], band[:, :, TK:2 * TK]])
    m = jnp.maximum(jnp.max(jnp.max(far, axis=0), axis=-1, keepdims=True),
                    jnp.max(jnp.max(near, axis=0), axis=-1, keepdims=True))
    p_far = jnp.exp(far - m[None])
    p_near = jnp.exp(near - m[None])
    den = (jnp.sum(jnp.sum(p_far, axis=0), axis=-1, keepdims=True)
           + jnp.sum(jnp.sum(p_near, axis=0), axis=-1, keepdims=True))
    p = jnp.concatenate([p_far, p_near], axis=0).reshape(nt, n_h * QS, TK).astype(BF16)
    n_steps, _, width = vt_ref.shape
    pps = width // TK
    acc = jnp.dot(p[nt - 1], own_v, preferred_element_type=F32)
    for s in range(n_steps):
        chunk = jnp.concatenate([p[s * pps + c] for c in range(pps)], axis=1)
        acc = acc + lax.dot_general(chunk, vt_ref[s], NT, preferred_element_type=F32)
    return acc / den.reshape(n_h * QS, 1)


def _pages_per_step(n_pages):
    return next(c for c in (16, 8, 4, 2, 1) if n_pages % c == 0)


def _page_specs(block, layer, n_pages, pps):
    tail = (0,) * (len(block) - 2)
    return [pl.BlockSpec(block, lambda b, p, tbl, k=k: (layer, tbl[b * n_pages + p * pps + k]) + tail)
            for k in range(pps)]


def _indexer_scores(qi, wi, s):
    s = s.reshape(A_IDX_HEADS, QS, s.shape[1])
    sc = jnp.zeros(s.shape[1:], F32)
    for h in range(A_IDX_HEADS):
        sc = sc + wi[:, h:h + 1] * jnp.maximum(s[h] * A_IDX_DIM ** -0.5, 0.0)
    return sc * A_IDX_HEADS ** -0.5


def _tile_positions(nt):
    tile = lax.broadcasted_iota(jnp.int32, (nt, QS, TK), 0)
    lane = lax.broadcasted_iota(jnp.int32, (nt, QS, TK), 2)
    row = lax.broadcasted_iota(jnp.int32, (nt, QS, TK), 1)
    pos = tile * TK + lane
    return pos, pos <= (nt - 1) * TK + row


def _dsa_sample_kernel(tbl_ref, q_ref, *refs):
    pps = len(refs) - 5
    page_refs, (band_ref, o_ref, key_scr, lg_scr, vt_scr) = refs[:pps], refs[pps:]
    p = pl.program_id(1)
    nt = key_scr.shape[0]
    x = q_ref[0]
    qi = _heads_to_rows(x[:, COL_QI:COL_QI + A_IDX_HEADS * A_IDX_DIM].astype(BF16), A_IDX_HEADS)
    wi = x[:, COL_A + MISC_WI:COL_A + MISC_WI + A_IDX_HEADS]
    qh = _heads_to_rows((x[:, COL_QA:COL_QA + A_MIX] * HEAD_DIM ** -0.5).astype(BF16), A_HEADS)
    for c, page_ref in enumerate(page_refs):
        lp = p * pps + c
        key_scr[lp] = _ordered_key(_indexer_scores(
            qi, wi, jnp.dot(qi, page_ref[0, 0, 2].astype(BF16), preferred_element_type=F32)))
        lg_scr[lp] = jnp.dot(qh, page_ref[0, 0, 0].astype(BF16), preferred_element_type=F32)
        vt_scr[p, :, c * TK:(c + 1) * TK] = page_ref[0, 0, 1].astype(BF16)

    @pl.when(p == pl.num_programs(1) - 1)
    def _():
        own = _own_tile(x[:, COL_A:COL_A + 3 * HEAD_DIM]).astype(BF16)
        pos, causal = _tile_positions(nt)
        sc = _indexer_scores(qi, wi, lax.dot_general(qi, own[:, 2 * HEAD_DIM:3 * HEAD_DIM], NT,
                                                     preferred_element_type=F32))
        key_scr[nt - 1] = _ordered_key(jnp.where(causal[nt - 1], sc, NEG))
        lg_scr[nt - 1] = lax.dot_general(qh, own[:, 0:HEAD_DIM], NT, preferred_element_type=F32)
        keys = key_scr[...]
        tau, cut = _topk_threshold_dense(keys, pos, float(A_TOPK), DSA_IDX_BITS)
        sel = ((keys > tau) | ((keys == tau) & (pos <= cut))) & causal
        o = _tile_attention(lg_scr, jnp.where(sel, 0.0, NEG), band_ref[:, 0:QS, :], A_HEADS, vt_scr,
                            own[:, HEAD_DIM:2 * HEAD_DIM])
        o_ref[0] = o.reshape(A_HEADS, QS, HEAD_DIM)


def _dsa_sample(proj, pool_t, page_table, layer, band_a):
    ns = proj.shape[0]
    n_pages = page_table.shape[1]
    nt = n_pages + 1
    pps = _pages_per_step(n_pages)
    grid_spec = pltpu.PrefetchScalarGridSpec(
        num_scalar_prefetch=1, grid=(ns, n_pages // pps),
        in_specs=[pl.BlockSpec((1, QS, EVEN_COLS), lambda b, p, tbl: (b, 0, 0))]
        + _page_specs((1, 1, 3, HEAD_DIM, PAGE_SIZE), layer, n_pages, pps)
        + [pl.BlockSpec((A_HEADS, TK, 2 * TK), lambda b, p, tbl: (0, 0, 0))],
        out_specs=pl.BlockSpec((1, A_HEADS, QS, HEAD_DIM), lambda b, p, tbl: (b, 0, 0, 0)),
        scratch_shapes=[pltpu.VMEM((nt, QS, TK), jnp.int32), pltpu.VMEM((nt, A_HEADS * QS, TK), F32),
                        pltpu.VMEM((n_pages // pps, HEAD_DIM, pps * TK), BF16)])
    return pl.pallas_call(
        _dsa_sample_kernel, grid_spec=grid_spec,
        out_shape=jax.ShapeDtypeStruct((ns, A_HEADS, QS, HEAD_DIM), F32),
        compiler_params=_params("parallel", "arbitrary"),
        name="dsa_sample",
    )(page_table.reshape(-1), proj, *([pool_t] * pps), band_a)


def _nsa_sample_kernel(tbl_ref, q_ref, *refs):
    pps = len(refs) - 11
    page_refs = refs[:pps]
    (win_ref, pos_ref, w1_ref, w2_ref, band_ref, cband_ref, expand_ref,
     o_ref, lg_scr, vt_scr, pool_scr) = refs[pps:]
    p = pl.program_id(1)
    nt = lg_scr.shape[1]
    nbl = pool_scr.shape[-1]
    x = q_ref[0]
    q = (x[:, COL_QB:COL_QB + B_MIX] * HEAD_DIM ** -0.5).astype(BF16)
    qg = [_heads_to_rows(q[:, g * B_R * HEAD_DIM:(g + 1) * B_R * HEAD_DIM], B_R) for g in range(B_KV_HEADS)]

    @pl.when(p == 0)
    def _():
        pool_scr[...] = jnp.zeros(pool_scr.shape, F32)

    ptok = lax.broadcasted_iota(jnp.int32, (TK, nbl), 0)
    pblk = lax.broadcasted_iota(jnp.int32, (TK, nbl), 1)
    for c, page_ref in enumerate(page_refs):
        lp = p * pps + c
        place = jnp.where(pblk == 2 * lp + ptok // B_BLOCK, 1.0, 0.0).astype(BF16)
        for g in range(B_KV_HEADS):
            for s in range(2):
                weighted = (page_ref[0, 0, s, g] * pos_ref[s]).astype(BF16)
                pool_scr[s, g] += jnp.dot(weighted, place, preferred_element_type=F32)
            lg_scr[g, lp] = jnp.dot(qg[g], page_ref[0, 0, 2, g].astype(BF16), preferred_element_type=F32)
            vt_scr[g, p, :, c * TK:(c + 1) * TK] = page_ref[0, 0, 3, g].astype(BF16)

    @pl.when(p == pl.num_programs(1) - 1)
    def _():
        past = (nt - 1) * TK
        n_blk = past // B_BLOCK
        nb_w = expand_ref.shape[0]
        own = _own_tile(x[:, COL_B + 2 * G_COLS:COL_B + 4 * G_COLS]).astype(BF16)
        gates = jax.nn.sigmoid(x[:, COL_A + MISC_GL:COL_A + MISC_GL + 3 * B_HEADS])
        blk = lax.broadcasted_iota(jnp.int32, (QS, nb_w), 1)
        tpos = past + lax.broadcasted_iota(jnp.int32, (QS, nb_w), 0)
        cur = tpos // B_BLOCK
        valid_c = ((blk + 1) * B_BLOCK - 1 <= tpos)[:, :nbl]
        forced = jnp.where((blk == 0) | (blk == cur) | (blk == cur - 1), FORCE_SCORE, 0.0)
        i_own = past // TK
        _, causal = _tile_positions(nt)
        own_win = _own_tile(x[:, COL_WIN:COL_WIN + 2 * G_COLS]).astype(BF16)
        o_cmps, imps = [], []
        for g in range(B_KV_HEADS):
            cmp_t = []
            for s in range(2):
                hid = jax.nn.gelu(jnp.dot(w1_ref[s], pool_scr[s, g].astype(BF16), preferred_element_type=F32))
                cmp_t.append(jnp.dot(w2_ref[s], hid.astype(BF16), preferred_element_type=F32).astype(BF16))
            lc = jnp.dot(qg[g], cmp_t[0], preferred_element_type=F32).reshape(B_R, QS, nbl)
            bias = []
            for r in range(B_R):
                cb = cband_ref[g * B_R + r][0:QS, :]
                b = jnp.zeros((QS, nbl), F32)
                for u in range(4):
                    b = jnp.where(blk[:, :nbl] == 2 * i_own + u - 2, cb[:, u:u + 1], b)
                bias.append(b)
            lc = jnp.where(valid_c[None], lc + jnp.stack(bias), NEG).reshape(B_R * QS, nbl)
            (pr,), den, _ = _softmax_pieces([lc])
            pc = (pr / den).reshape(B_R, QS, nbl) * jnp.where(valid_c, 1.0, 0.0)[None]
            o_cmps.append(lax.dot_general(pc.reshape(B_R * QS, nbl).astype(BF16), cmp_t[1], NT,
                                          preferred_element_type=F32))
            imp = jnp.concatenate([jnp.sum(pc, axis=0), jnp.zeros((QS, nb_w - nbl), F32)], axis=1)
            imps.append(jnp.where(blk <= cur, imp + forced, jnp.where(blk <= n_blk, NEG, -jnp.inf)))
        selm = _select_blocks(jnp.concatenate(imps, axis=0), B_TOPN)
        for g in range(B_KV_HEADS):
            gc = slice(g * HEAD_DIM, (g + 1) * HEAD_DIM)
            o_cmp = o_cmps[g]
            sel = jnp.dot(selm[g * QS:(g + 1) * QS].astype(BF16), expand_ref[...],
                          preferred_element_type=F32)
            madd = jnp.stack([jnp.where((sel[:, t * TK:(t + 1) * TK] > 0.5) & causal[t], 0.0, NEG)
                              for t in range(nt)])
            lg_scr[g, nt - 1] = lax.dot_general(qg[g], own[:, gc], NT, preferred_element_type=F32)
            o_sel = _tile_attention(lg_scr.at[g], madd, band_ref[g * B_R:(g + 1) * B_R, 0:QS, :], B_R, vt_scr.at[g],
                                    own[:, G_COLS + g * HEAD_DIM:G_COLS + (g + 1) * HEAD_DIM])
            tiles = [(own_win[:, gc], own_win[:, G_COLS + g * HEAD_DIM:G_COLS + (g + 1) * HEAD_DIM], True)]
            for d in range(1, NSA_WIN_TILES + 1):
                wr = slice((NSA_WIN_TILES - d) * TK, (NSA_WIN_TILES - d + 1) * TK)
                tiles.append((win_ref[0, wr, gc].astype(BF16),
                              win_ref[0, wr, G_COLS + g * HEAD_DIM:G_COLS + (g + 1) * HEAD_DIM].astype(BF16), True))
            o_win = _window_attend(qg[g], B_R, QS, tiles, band_ref, g * B_R)
            for r in range(B_R):
                h = g * B_R + r
                rr = slice(r * QS, (r + 1) * QS)
                o_ref[0, h] = (gates[:, 3 * h:3 * h + 1] * o_cmp[rr] + gates[:, 3 * h + 1:3 * h + 2] * o_sel[rr]
                               + gates[:, 3 * h + 2:3 * h + 3] * o_win[rr])


def _round_up(n, m):
    return -(-n // m) * m


def _sample_expand(n_pages):
    nb_w = _round_up(2 * n_pages + 1, LANES)
    tok_blk = np.arange((n_pages + 1) * TK) // B_BLOCK
    return jnp.asarray(np.arange(nb_w)[:, None] == tok_blk[None, :], BF16)


def _compress_weights_t(cmp_pos, cmp_w1, cmp_w2):
    pos_t = jnp.concatenate([jnp.swapaxes(cmp_pos, 1, 2)] * (PAGE_SIZE // B_BLOCK), axis=2)
    return pos_t, jnp.swapaxes(cmp_w1, 1, 2).astype(BF16), jnp.swapaxes(cmp_w2, 1, 2).astype(BF16)


def _nsa_sample(proj, pool_t, win, page_table, layer, cmp_w, band_b, cband_b, expand):
    ns = proj.shape[0]
    n_pages = page_table.shape[1]
    nt = n_pages + 1
    nbl = _round_up(2 * n_pages, LANES)
    const = lambda shape: pl.BlockSpec(shape, lambda b, p, tbl: (0,) * len(shape))
    pps = _pages_per_step(n_pages)
    grid_spec = pltpu.PrefetchScalarGridSpec(
        num_scalar_prefetch=1, grid=(ns, n_pages // pps),
        in_specs=[pl.BlockSpec((1, QS, EVEN_COLS), lambda b, p, tbl: (b, 0, 0))]
        + _page_specs((1, 1, 4, B_KV_HEADS, HEAD_DIM, PAGE_SIZE), layer, n_pages, pps)
        + [pl.BlockSpec((1, B_WINDOW, 2 * G_COLS), lambda b, p, tbl: (b, 0, 0)),
           const((2, HEAD_DIM, PAGE_SIZE)), const((2, HEAD_DIM, HEAD_DIM)), const((2, HEAD_DIM, HEAD_DIM)),
           const((B_HEADS, TK, 2 * TK)), const((B_HEADS, TK, 4)), const(expand.shape)],
        out_specs=pl.BlockSpec((1, B_HEADS, QS, HEAD_DIM), lambda b, p, tbl: (b, 0, 0, 0)),
        scratch_shapes=[pltpu.VMEM((B_KV_HEADS, nt, B_R * QS, TK), F32),
                        pltpu.VMEM((B_KV_HEADS, n_pages // pps, HEAD_DIM, pps * TK), BF16),
                        pltpu.VMEM((2, B_KV_HEADS, HEAD_DIM, nbl), F32)])
    return pl.pallas_call(
        _nsa_sample_kernel, grid_spec=grid_spec,
        out_shape=jax.ShapeDtypeStruct((ns, B_HEADS, QS, HEAD_DIM), F32),
        compiler_params=_params("parallel", "arbitrary"),
        name="nsa_sample",
    )(page_table.reshape(-1), proj, *([pool_t] * pps), win, *cmp_w, band_b, cband_b, expand)


def _swa_sample_kernel(q_ref, buf_ref, band_ref, sink_ref, o_ref):
    x = q_ref[0]
    q = (x[:, COL_QC:COL_QC + C_MIX] * HEAD_DIM ** -0.5).astype(BF16)
    own = _own_tile(x[:, COL_CWIN:COL_CWIN + 2 * G_COLS]).astype(BF16)
    for g in range(C_KV_HEADS):
        gc = slice(g * HEAD_DIM, (g + 1) * HEAD_DIM)
        vc = slice(G_COLS + g * HEAD_DIM, G_COLS + (g + 1) * HEAD_DIM)
        qg = _heads_to_rows(q[:, g * C_R * HEAD_DIM:(g + 1) * C_R * HEAD_DIM], C_R)
        tiles = [(own[:, gc], own[:, vc], True),
                 (buf_ref[0, :, gc].astype(BF16), buf_ref[0, :, vc].astype(BF16), True)]
        sink = jnp.concatenate([jnp.broadcast_to(sink_ref[g * C_R + r], (QS, 1)) for r in range(C_R)], axis=0)
        o = _window_attend(qg, C_R, QS, tiles, band_ref, g * C_R, sink)
        o_ref[0, g * C_R:(g + 1) * C_R] = o.reshape(C_R, QS, HEAD_DIM)


def _swa_sample(proj, buf, band_c, sinks):
    ns = proj.shape[0]
    return pl.pallas_call(
        _swa_sample_kernel,
        grid=(ns,),
        in_specs=[pl.BlockSpec((1, QS, C_MIX + 2 * G_COLS), lambda b: (b, 0, 0)),
                  pl.BlockSpec((1, C_WINDOW, 2 * G_COLS), lambda b: (b, 0, 0)),
                  _resident((C_HEADS, TK, 2 * TK)), _resident((C_HEADS, 1, 1))],
        out_specs=pl.BlockSpec((1, C_HEADS, QS, HEAD_DIM), lambda b: (b, 0, 0, 0)),
        out_shape=jax.ShapeDtypeStruct((ns, C_HEADS, QS, HEAD_DIM), F32),
        compiler_params=_params("parallel"),
        name="swa_sample",
    )(proj, buf, band_c, sinks)


def _trunk_prompt(x, mem_kv, w):
    bn, t, _ = x.shape
    assert t % TK == 0 and t >= 4 * A_TOPK
    n = bn * t
    a_rows, b_rows, b_states, c_states = [], [], [], []
    kb, kc = min(B_WINDOW, t), min(C_WINDOW, t)
    h = x.reshape(n, D_MODEL)
    for l in range(DEPTH):
        g, bb = w['ln_g'][l], w['ln_b'][l]
        h = _ffn_ln(h, w['ffn_wg'][l, 0], w['ffn_wu'][l, 0], w['ffn_wd'][l, 0], g[0], bb[0])
        if l % 2 == 0:
            e = l // 2
            proj = _linear(h, w['even_w_in_perm'][e]).reshape(bn, t, EVEN_COLS)
            cmp_kv = _compress_prompt(proj, *w['cmp'][e])
            o_a = _dsa_prompt(proj, w['band'][:A_HEADS])
            o_b = _nsa_prompt(proj, cmp_kv, w['band'][A_HEADS:], w['cband'][A_HEADS:])
            h = _heads_out_ln([o_a, o_b], w['even_w_out_heads'][e], h.reshape(bn, t, D_MODEL), g[1], bb[1])
            a_rows.append(proj[:, :, COL_A:COL_A + 3 * HEAD_DIM].reshape(bn, t, 3, HEAD_DIM))
            b_rows.append(proj[:, :, COL_B:COL_B + 4 * G_COLS].reshape(bn, t, 4, B_KV_HEADS, HEAD_DIM))
            b_states.append(proj[:, t - kb:, COL_WIN:COL_WIN + 2 * G_COLS].reshape(bn, kb, 2, B_KV_HEADS, HEAD_DIM))
        else:
            oi = l // 2
            proj = _linear(h, w['odd_w_in'][oi]).reshape(bn, t, C_MIX + 2 * G_COLS)
            o_c = _swa_prompt(proj, w['band'], w['sinks_shifted'][oi])
            h = _heads_out_ln([o_c], w['odd_w_out_heads'][oi], h.reshape(bn, t, D_MODEL), g[1], bb[1])
            c_states.append(proj[:, t - kc:, COL_CWIN:].reshape(bn, kc, 2, C_KV_HEADS, HEAD_DIM))
        h = h.reshape(n, D_MODEL)
        xq = _linear(h, w['x_wq'][l]).reshape(bn, t, X_WIDTH)
        xo = _cross_attn(xq, mem_kv[l].reshape(bn, -1, 2 * X_WIDTH)).reshape(n, X_WIDTH)
        h = _linear_res_ln(xo, w['x_wo'][l], h, g[2], bb[2])
        h = _ffn_ln(h, w['ffn_wg'][l, 1], w['ffn_wu'][l, 1], w['ffn_wd'][l, 1], g[3], bb[3])
    return (h.reshape(bn, t, D_MODEL), jnp.stack(a_rows), jnp.stack(b_rows), jnp.stack(b_states),
            jnp.stack(c_states))


def _shift_in(buf, new):
    return jnp.concatenate([buf, new], axis=1)[:, new.shape[1]:]


def _trunk_sample(x, mem_kv, a_pool, b_pool, b_win, c_win, page_table, w):
    ns, t, _ = x.shape
    assert t <= QS and b_win.shape[2] == B_WINDOW and c_win.shape[2] == C_WINDOW
    assert page_table.shape[1] * PAGE_SIZE >= max(4 * A_TOPK, B_TOPN * B_BLOCK, 3 * TK)
    n = ns * QS
    n_pages = page_table.shape[1]
    a_pool = jnp.transpose(a_pool, (0, 1, 3, 4, 2))
    b_pool = jnp.transpose(b_pool, (0, 1, 3, 4, 5, 2))
    expand = _sample_expand(n_pages)
    a_rows, b_rows, b_states, c_states = [], [], [], []
    h = jnp.pad(x, ((0, 0), (0, QS - t), (0, 0))).reshape(n, D_MODEL)
    for l in range(DEPTH):
        g, bb = w['ln_g'][l], w['ln_b'][l]
        h = _ffn_ln(h, w['ffn_wg'][l, 0], w['ffn_wu'][l, 0], w['ffn_wd'][l, 0], g[0], bb[0])
        if l % 2 == 0:
            e = l // 2
            proj = _linear(h, w['even_w_in_perm'][e]).reshape(ns, QS, EVEN_COLS)
            win = b_win[e].reshape(ns, B_WINDOW, 2 * G_COLS)
            o_a = _dsa_sample(proj, a_pool, page_table, e, w['band'][:A_HEADS])
            o_b = _nsa_sample(proj, b_pool, win, page_table, e, w['cmp_t'][e], w['band'][A_HEADS:],
                              w['cband'][A_HEADS:], expand)
            h = _heads_out_ln([o_a, o_b], w['even_w_out_heads'][e], h.reshape(ns, QS, D_MODEL), g[1], bb[1])
            a_rows.append(proj[:, :t, COL_A:COL_A + 3 * HEAD_DIM].reshape(ns, t, 3, HEAD_DIM))
            b_rows.append(proj[:, :t, COL_B:COL_B + 4 * G_COLS].reshape(ns, t, 4, B_KV_HEADS, HEAD_DIM))
            b_states.append(_shift_in(win, proj[:, :t, COL_WIN:COL_WIN + 2 * G_COLS]).reshape(b_win.shape[1:]))
        else:
            oi = l // 2
            proj = _linear(h, w['odd_w_in'][oi]).reshape(ns, QS, C_MIX + 2 * G_COLS)
            buf = c_win[oi].reshape(ns, C_WINDOW, 2 * G_COLS)
            o_c = _swa_sample(proj, buf, w['band'], w['sinks_shifted'][oi])
            h = _heads_out_ln([o_c], w['odd_w_out_heads'][oi], h.reshape(ns, QS, D_MODEL), g[1], bb[1])
            c_states.append(_shift_in(buf, proj[:, :t, COL_CWIN:]).reshape(c_win.shape[1:]))
        h = h.reshape(n, D_MODEL)
        xq = _linear(h, w['x_wq'][l]).reshape(ns, QS, X_WIDTH)
        xo = _cross_attn(xq, mem_kv[l].reshape(ns, -1, 2 * X_WIDTH)).reshape(n, X_WIDTH)
        h = _linear_res_ln(xo, w['x_wo'][l], h, g[2], bb[2])
        h = _ffn_ln(h, w['ffn_wg'][l, 1], w['ffn_wu'][l, 1], w['ffn_wd'][l, 1], g[3], bb[3])
    return (h.reshape(ns, QS, D_MODEL)[:, :t], jnp.stack(a_rows), jnp.stack(b_rows), jnp.stack(b_states),
            jnp.stack(c_states))


def kernel(x_prompt, x_sample, cache_a_kv, cache_b_kv, cache_b_win, cache_c_win, cache_mem_kv, page_table,
           mem_prompt, ln_g, ln_b, ffn_wg, ffn_wu, ffn_wd, even_w_in, even_w_out, nsa_cmp_pos, nsa_cmp_w1,
           nsa_cmp_w2, odd_w_in, odd_w_out, c_sinks, x_wq, x_wk, x_wv, x_wo, rel_table):
    nb, n_mem = mem_prompt.shape[:2]
    dist_bias = _distance_bias(rel_table)
    w = dict(ln_g=ln_g, ln_b=ln_b, ffn_wg=ffn_wg.astype(BF16), ffn_wu=ffn_wu.astype(BF16),
             ffn_wd=ffn_wd.astype(BF16), odd_w_in=odd_w_in.astype(BF16),
             x_wq=x_wq.astype(BF16), x_wo=x_wo.astype(BF16),
             even_w_in_perm=_permute_even_w_in(even_w_in).astype(BF16),
             even_w_out_heads=even_w_out.astype(BF16).reshape(-1, A_HEADS + B_HEADS, HEAD_DIM, D_MODEL),
             odd_w_out_heads=odd_w_out.astype(BF16).reshape(-1, C_HEADS, HEAD_DIM, D_MODEL),
             band=_band(dist_bias), cband=_cmp_band(dist_bias),
             cmp=[_compress_weights(nsa_cmp_pos[e], nsa_cmp_w1[e], nsa_cmp_w2[e]) for e in range(even_w_in.shape[0])],
             cmp_t=[_compress_weights_t(nsa_cmp_pos[e], nsa_cmp_w1[e], nsa_cmp_w2[e])
                    for e in range(even_w_in.shape[0])],
             sinks_shifted=(c_sinks - rel_table[N_BUCKETS - 1][None, :]).reshape(-1, C_HEADS, 1, 1))
    w_kv = jnp.concatenate([x_wk, x_wv], axis=-1).astype(BF16)
    mem2d = mem_prompt.reshape(nb * n_mem, D_MODEL)
    mem_kv_prompt = jnp.stack([_linear(mem2d, w_kv[l]) for l in range(DEPTH)])
    mem_kv_prompt = mem_kv_prompt.reshape(DEPTH, nb, n_mem, 2, X_HEADS, X_HEAD_DIM)
    y_p, a_p, b_p, bw_p, cw_p = _trunk_prompt(x_prompt, mem_kv_prompt, w)
    y_s, a_s, b_s, bw_s, cw_s = _trunk_sample(x_sample, cache_mem_kv, cache_a_kv, cache_b_kv, cache_b_win,
                                              cache_c_win, page_table, w)
    return (y_p, y_s, a_p, a_s, b_p, b_s, bw_p, bw_s, cw_p, cw_s, mem_kv_prompt)
```
